```python
import jax, jax.numpy as jnp
from jax import lax
import numpy as np

D_MODEL = 2048
BATCH = 4
SEQ = 2048
DEPTH = 4
DEC_BATCH = 128
DEC_SEQ = 1
PAST_LEN = 16384
PAGE_SIZE = 128

HEAD_DIM = 128
H_HGRN = D_MODEL // (4 * HEAD_DIM)
H_MLSTM = (D_MODEL - H_HGRN * HEAD_DIM) // (2 * HEAD_DIM)
H_RET = (D_MODEL - (H_HGRN + H_MLSTM) * HEAD_DIM) // HEAD_DIM
W_HGRN = H_HGRN * HEAD_DIM
W_MLSTM = H_MLSTM * HEAD_DIM
W_RET = H_RET * HEAD_DIM
D_MIX = W_HGRN + W_MLSTM + W_RET
D_IN = 4 * W_HGRN + 4 * W_MLSTM + 2 * H_MLSTM + 4 * W_RET
D_FF = ((8 * D_MODEL // 3 + 127) // 128) * 128
CHUNK = 64
ROPE_BASE = 10000.0
EPS = 1e-6

kernel_name = 'hybrid_hgrn2_mlstm_retention_macaron_step'


def _rmsnorm(x, g):
    xf = x.astype(jnp.float32)
    y = xf * lax.rsqrt(jnp.mean(xf * xf, axis=-1, keepdims=True) + EPS)
    return (y * g.astype(jnp.float32)).astype(x.dtype)


def _head_rmsnorm(o, g):
    b, t, h, d = o.shape
    y = o * lax.rsqrt(jnp.mean(o * o, axis=-1, keepdims=True) + EPS)
    return y.reshape(b, t, h * d) * g.astype(jnp.float32)


def _swiglu(x, w_gate, w_up, w_down):
    return jnp.matmul(jax.nn.silu(jnp.matmul(x, w_gate)) * jnp.matmul(x, w_up), w_down)


def _rope(x, pos):
    d = x.shape[-1]
    inv = ROPE_BASE ** (-jnp.arange(0, d, 2, dtype=jnp.float32) / d)
    ang = pos[:, None] * inv[None, :]
    cos = jnp.cos(ang)[None, :, None, :]
    sin = jnp.sin(ang)[None, :, None, :]
    x1, x2 = x[..., : d // 2], x[..., d // 2:]
    return jnp.concatenate([x1 * cos - x2 * sin, x1 * sin + x2 * cos], axis=-1)


def _chunk_len(t):
    return CHUNK if t % CHUNK == 0 else t


def _to_chunks(x, c):
    b, t = x.shape[:2]
    return jnp.moveaxis(x.reshape((b, t // c, c) + x.shape[2:]), 1, 0)


def _from_chunks(x):
    n, b, c = x.shape[:3]
    return jnp.moveaxis(x, 0, 1).reshape((b, n * c) + x.shape[3:])


def _hgrn2_chunked(q, k, v, log_f, s0):
    c = _chunk_len(q.shape[1])
    causal = jnp.tril(jnp.ones((c, c), dtype=bool))[None, :, :, None, None]

    def step(s, blk):
        qc, kc, vc, lf = blk
        b = jnp.cumsum(lf, axis=1)
        rel = jnp.where(causal, b[:, :, None] - b[:, None, :], -jnp.inf)
        a = jnp.einsum('bthk,bshk,btshk->bths', qc, kc, jnp.exp(rel))
        o = jnp.einsum('bths,bshv->bthv', a, vc) + jnp.einsum('bthk,bhkv->bthv', qc * jnp.exp(b), s)
        b_end = b[:, -1]
        s_new = jnp.exp(b_end)[..., None] * s + jnp.einsum('bshk,bshv->bhkv', kc * jnp.exp(b_end[:, None] - b), vc)
        return s_new, o

    s_fin, o = lax.scan(step, s0, (_to_chunks(q, c), _to_chunks(k, c), _to_chunks(v, c), _to_chunks(log_f, c)))
    return _from_chunks(o), s_fin


def _retention_chunked(q, k, v, log_gamma, s0):
    c = _chunk_len(q.shape[1])
    idx = jnp.arange(c, dtype=jnp.float32)
    causal = jnp.tril(jnp.ones((c, c), dtype=bool))[:, :, None]
    decay = jnp.exp(jnp.where(causal, (idx[:, None] - idx[None, :])[:, :, None] * log_gamma, -jnp.inf))
    q_dec = jnp.exp((idx + 1.0)[:, None] * log_gamma)
    k_dec = jnp.exp((c - 1.0 - idx)[:, None] * log_gamma)
    s_dec = jnp.exp(c * log_gamma)

    def step(s, blk):
        qc, kc, vc = blk
        a = jnp.einsum('bthk,bshk->btsh', qc, kc) * decay
        o = jnp.einsum('btsh,bshv->bthv', a, vc) + jnp.einsum('bthk,bhkv->bthv', qc * q_dec[:, :, None], s)
        s_new = s_dec[:, None, None] * s + jnp.einsum('bshk,bshv->bhkv', kc * k_dec[:, :, None], vc)
        return s_new, o

    s_fin, o = lax.scan(step, s0, (_to_chunks(q, c), _to_chunks(k, c), _to_chunks(v, c)))
    return _from_chunks(o), s_fin


def _mlstm_chunked(q, k, v, log_i, log_f, c0, n0, m0):
    c = _chunk_len(q.shape[1])
    causal = jnp.tril(jnp.ones((c, c), dtype=bool))

    def step(carry, blk):
        cm, nm, m = carry
        qc, kc, vc, li, lf = blk
        bt = jnp.swapaxes(jnp.cumsum(lf, axis=1), 1, 2)
        lit = jnp.swapaxes(li, 1, 2)
        log_d = jnp.where(causal, bt[..., :, None] - bt[..., None, :] + lit[..., None, :], -jnp.inf)
        log_inter = bt + m[..., None]
        m_t = jnp.maximum(log_inter, jnp.max(log_d, axis=-1))
        dmat = jnp.exp(log_d - m_t[..., None])
        w_inter = jnp.exp(log_inter - m_t)
        sc = jnp.einsum('bthk,bshk->bhts', qc, kc) * dmat
        num = jnp.einsum('bhts,bshv->bthv', sc, vc) + jnp.einsum('bthk,bhkv->bthv', qc, cm) * jnp.swapaxes(w_inter, 1, 2)[..., None]
        den = jnp.sum(sc, axis=-1) + w_inter * jnp.einsum('bthk,bhk->bht', qc, nm)
        h = num / jnp.swapaxes(jnp.maximum(jnp.abs(den), jnp.exp(-m_t)), 1, 2)[..., None]
        m_end = m_t[..., -1]
        b_end = bt[..., -1]
        w_k = jnp.exp(b_end[..., None] - bt + lit - m_end[..., None])
        w_c = jnp.exp(b_end + m - m_end)
        c_new = w_c[..., None, None] * cm + jnp.einsum('bshk,bhs,bshv->bhkv', kc, w_k, vc)
        n_new = w_c[..., None] * nm + jnp.einsum('bshk,bhs->bhk', kc, w_k)
        return (c_new, n_new, m_end), h

    (c_f, n_f, m_f), h = lax.scan(step, (c0, n0, m0), (_to_chunks(q, c), _to_chunks(k, c), _to_chunks(v, c), _to_chunks(log_i, c), _to_chunks(log_f, c)))
    return _from_chunks(h), c_f, n_f, m_f


def _mix(h, pos, s_h, s_c, s_n, s_m, s_r, lb, w_in, gate_bias, gn_hgrn, gn_mlstm, gn_ret, w_out):
    f32 = jnp.float32
    bsz, t, _ = h.shape
    d = HEAD_DIM
    sizes = [W_HGRN] * 4 + [W_MLSTM] * 4 + [2 * H_MLSTM] + [W_RET] * 4
    cuts = np.cumsum(sizes)[:-1].tolist()
    z = jnp.matmul(h, w_in).astype(f32)
    hq, hf, hi, hg, mq, mk, mv, mo, mgate, rq, rk, rv, rg = jnp.split(z, cuts, axis=-1)

    def heads(a, n):
        return a.reshape(bsz, t, n, d)

    lbh = lb.astype(f32).reshape(H_HGRN, d)
    fz = heads(hf, H_HGRN)
    log_f = jnp.logaddexp(jnp.log(lbh), jnp.log1p(-lbh) + jax.nn.log_sigmoid(fz))
    k_h = (1.0 - lbh) * jax.nn.sigmoid(-fz)
    o_h, s_h_new = _hgrn2_chunked(jax.nn.silu(heads(hq, H_HGRN)), k_h, heads(hi, H_HGRN), log_f, s_h.astype(f32))
    o_h = _head_rmsnorm(o_h, gn_hgrn) * jax.nn.silu(hg)

    g = mgate + gate_bias.astype(f32)
    li = g[..., :H_MLSTM]
    lf = jax.nn.log_sigmoid(g[..., H_MLSTM:])
    o_m, c_new, n_new, m_new = _mlstm_chunked(heads(mq, H_MLSTM), heads(mk, H_MLSTM) * d ** -0.5, heads(mv, H_MLSTM), li, lf, s_c.astype(f32), s_n.astype(f32), s_m.astype(f32))
    o_m = _head_rmsnorm(o_m, gn_mlstm) * jax.nn.sigmoid(mo)

    log_gamma = jnp.log1p(-jnp.exp2(-5.0 - jnp.arange(H_RET, dtype=f32)))
    q_r = _rope(heads(rq, H_RET), pos)
    k_r = _rope(heads(rk, H_RET), pos) * d ** -0.5
    o_r, s_r_new = _retention_chunked(q_r, k_r, heads(rv, H_RET), log_gamma, s_r.astype(f32))
    o_r = _head_rmsnorm(o_r, gn_ret) * jax.nn.silu(rg)

    o = jnp.concatenate([o_h, o_m, o_r], axis=-1).astype(h.dtype)
    return jnp.matmul(o, w_out), s_h_new, c_new, n_new, m_new, s_r_new


def _trunk(x, pos, st_h, st_c, st_n, st_m, st_r, lb_all, norm_ffn1, ffn1_w_gate, ffn1_w_up, ffn1_w_down, norm_mix, w_in, mlstm_gate_bias, gn_hgrn, gn_mlstm, gn_ret, w_out, norm_ffn2, ffn2_w_gate, ffn2_w_up, ffn2_w_down, norm_final):
    outs_h, outs_c, outs_n, outs_m, outs_r = [], [], [], [], []
    for l in range(DEPTH):
        x = x + 0.5 * _swiglu(_rmsnorm(x, norm_ffn1[l]), ffn1_w_gate[l], ffn1_w_up[l], ffn1_w_down[l])
        y, sh, sc, sn, sm, sr = _mix(_rmsnorm(x, norm_mix[l]), pos, st_h[l], st_c[l], st_n[l], st_m[l], st_r[l], lb_all[l], w_in[l], mlstm_gate_bias[l], gn_hgrn[l], gn_mlstm[l], gn_ret[l], w_out[l])
        x = x + y
        x = x + 0.5 * _swiglu(_rmsnorm(x, norm_ffn2[l]), ffn2_w_gate[l], ffn2_w_up[l], ffn2_w_down[l])
        outs_h.append(sh.astype(x.dtype))
        outs_c.append(sc.astype(x.dtype))
        outs_n.append(sn.astype(x.dtype))
        outs_m.append(sm.astype(x.dtype))
        outs_r.append(sr.astype(x.dtype))
    return _rmsnorm(x, norm_final), jnp.stack(outs_h), jnp.stack(outs_c), jnp.stack(outs_n), jnp.stack(outs_m), jnp.stack(outs_r)


def setup_inputs(seed: int = 0) -> dict:
    key = jax.random.key(seed)
    ks = jax.random.split(key, 26)
    f32 = jnp.float32
    d = HEAD_DIM

    def nrm(k, shape, scale):
        return jax.random.normal(k, shape, f32) * scale

    gate_i = nrm(ks[14], (DEPTH, H_MLSTM), 0.1)
    gate_f = jnp.broadcast_to(jnp.linspace(3.0, 6.0, H_MLSTM, dtype=f32), (DEPTH, H_MLSTM)) + nrm(ks[15], (DEPTH, H_MLSTM), 0.01)
    return {
        'x_prompt': nrm(ks[0], (BATCH, SEQ, D_MODEL), 1.0),
        'x_sample': nrm(ks[1], (DEC_BATCH, DEC_SEQ, D_MODEL), 1.0),
        'state_hgrn': nrm(ks[2], (DEPTH, DEC_BATCH, H_HGRN, d, d), 0.1),
        'state_mlstm_c': nrm(ks[3], (DEPTH, DEC_BATCH, H_MLSTM, d, d), 0.1),
        'state_mlstm_n': nrm(ks[4], (DEPTH, DEC_BATCH, H_MLSTM, d), 0.1),
        'state_mlstm_m': nrm(ks[5], (DEPTH, DEC_BATCH, H_MLSTM), 1.0),
        'state_ret': nrm(ks[6], (DEPTH, DEC_BATCH, H_RET, d, d), 0.3),
        'norm_ffn1': 1.0 + nrm(ks[7], (DEPTH, D_MODEL), 0.01),
        'ffn1_w_gate': nrm(ks[8], (DEPTH, D_MODEL, D_FF), D_MODEL ** -0.5),
        'ffn1_w_up': nrm(ks[9], (DEPTH, D_MODEL, D_FF), D_MODEL ** -0.5),
        'ffn1_w_down': nrm(ks[10], (DEPTH, D_FF, D_MODEL), D_FF ** -0.5),
        'norm_mix': 1.0 + nrm(ks[11], (DEPTH, D_MODEL), 0.01),
        'w_in': nrm(ks[12], (DEPTH, D_MODEL, D_IN), D_MODEL ** -0.5),
        'hgrn_lb_logits': nrm(ks[13], (DEPTH, W_HGRN), 1.0),
        'mlstm_gate_bias': jnp.concatenate([gate_i, gate_f], axis=-1),
        'gn_hgrn': 1.0 + nrm(ks[16], (DEPTH, W_HGRN), 0.01),
        'gn_mlstm': 1.0 + nrm(ks[17], (DEPTH, W_MLSTM), 0.01),
        'gn_ret': 1.0 + nrm(ks[18], (DEPTH, W_RET), 0.01),
        'w_out': nrm(ks[19], (DEPTH, D_MIX, D_MODEL), D_MIX ** -0.5),
        'norm_ffn2': 1.0 + nrm(ks[20], (DEPTH, D_MODEL), 0.01),
        'ffn2_w_gate': nrm(ks[21], (DEPTH, D_MODEL, D_FF), D_MODEL ** -0.5),
        'ffn2_w_up': nrm(ks[22], (DEPTH, D_MODEL, D_FF), D_MODEL ** -0.5),
        'ffn2_w_down': nrm(ks[23], (DEPTH, D_FF, D_MODEL), D_FF ** -0.5),
        'norm_final': 1.0 + nrm(ks[24], (D_MODEL,), 0.01),
    }


def reference(x_prompt, x_sample, state_hgrn, state_mlstm_c, state_mlstm_n, state_mlstm_m, state_ret, norm_ffn1, ffn1_w_gate, ffn1_w_up, ffn1_w_down, norm_mix, w_in, hgrn_lb_logits, mlstm_gate_bias, gn_hgrn, gn_mlstm, gn_ret, w_out, norm_ffn2, ffn2_w_gate, ffn2_w_up, ffn2_w_down, norm_final):
    f32 = jnp.float32
    d = HEAD_DIM
    lb_all = jnp.cumsum(jax.nn.softmax(hgrn_lb_logits.astype(f32), axis=0), axis=0)
    lb_all = lb_all - lb_all[0:1]
    weights = (lb_all, norm_ffn1, ffn1_w_gate, ffn1_w_up, ffn1_w_down, norm_mix, w_in, mlstm_gate_bias, gn_hgrn, gn_mlstm, gn_ret, w_out, norm_ffn2, ffn2_w_gate, ffn2_w_up, ffn2_w_down, norm_final)

    bp = x_prompt.shape[0]
    z_h = jnp.zeros((DEPTH, bp, H_HGRN, d, d), f32)
    z_c = jnp.zeros((DEPTH, bp, H_MLSTM, d, d), f32)
    z_n = jnp.zeros((DEPTH, bp, H_MLSTM, d), f32)
    z_m = jnp.zeros((DEPTH, bp, H_MLSTM), f32)
    z_r = jnp.zeros((DEPTH, bp, H_RET, d, d), f32)
    pos_p = jnp.arange(x_prompt.shape[1], dtype=f32)
    y_prompt, hp, cp, np_, mp, rp = _trunk(x_prompt, pos_p, z_h, z_c, z_n, z_m, z_r, *weights)

    pos_s = PAST_LEN + jnp.arange(x_sample.shape[1], dtype=f32)
    y_sample, hs, cs, ns, ms, rs = _trunk(x_sample, pos_s, state_hgrn, state_mlstm_c, state_mlstm_n, state_mlstm_m, state_ret, *weights)
    return (y_prompt, y_sample, hp, cp, np_, mp, rp, hs, cs, ns, ms, rs)
```

```python
import functools

import jax
import jax.numpy as jnp
from jax import lax
from jax.experimental import pallas as pl
from jax.experimental.pallas import tpu as pltpu

F32 = jnp.float32
BF16 = jnp.bfloat16

HEAD_DIM = 128
LANES = 128
EPS = 1e-6
ROPE_BASE = 10000.0
K_SCALE = HEAD_DIM ** -0.5
PAST_LEN = 16384

BLK = 128
SUB = 16
STEP_ROWS = 8
ROW_TILE = 1040
NORM_TILE = 416
ROPE_TILE = 256
VMEM_LIMIT_BYTES = 56 * 1024 * 1024

_TRANS_B = (((1,), (1,)), ((), ()))


def _params(*semantics):
    return pltpu.CompilerParams(dimension_semantics=semantics, vmem_limit_bytes=VMEM_LIMIT_BYTES)


def _silu(x):
    return x * jax.nn.sigmoid(x)


def _log_sigmoid(x):
    return jnp.minimum(x, 0.0) - jnp.log1p(jnp.exp(-jnp.abs(x)))


def _iota(shape, dim):
    return lax.broadcasted_iota(jnp.int32, shape, dim)


def _seg_cumsum(x, seg):
    pos = _iota(x.shape, 0) & (seg - 1)
    shift = 1
    while shift < seg:
        x = x + jnp.where(pos >= shift, pltpu.roll(x, shift, 0), 0.0)
        shift *= 2
    return x


def _head_norm_gate(o, gain, gate):
    return o * lax.rsqrt(jnp.mean(o * o, axis=-1, keepdims=True) + EPS) * gain * gate


def _pick_lane(x, lane_mask):
    return jnp.sum(jnp.where(lane_mask, x, 0.0), axis=-1, keepdims=True)


def _rmsnorm_kernel(x_ref, g_ref, o_ref):
    x = x_ref[...]
    y = x * lax.rsqrt(jnp.mean(x * x, axis=-1, keepdims=True) + EPS)
    o_ref[...] = (y * g_ref[...]).astype(o_ref.dtype)


def _rmsnorm(x, gains, layer, out_dtype):
    m, d = x.shape
    gains = gains.reshape(-1, 1, d)
    return pl.pallas_call(
        _rmsnorm_kernel,
        out_shape=jax.ShapeDtypeStruct((m, d), out_dtype),
        grid=(m // NORM_TILE,),
        in_specs=[
            pl.BlockSpec((NORM_TILE, d), lambda i: (i, 0)),
            pl.BlockSpec((None, 1, d), lambda i: (layer, 0, 0)),
        ],
        out_specs=pl.BlockSpec((NORM_TILE, d), lambda i: (i, 0)),
        compiler_params=_params("arbitrary"),
        name="rmsnorm",
    )(x, gains)


def _matmul_kernel(*refs, n_w, mode):
    lhs_ref = refs[0]
    w_refs = refs[1:1 + n_w]
    pos = 1 + n_w
    x_ref = None
    if mode == "resid":
        x_ref = refs[pos]
        pos += 1
    o_ref = refs[pos]
    wb_refs = refs[pos + 1:pos + 1 + n_w]

    @pl.when(pl.program_id(1) == 0)
    def _():
        for w_ref, wb_ref in zip(w_refs, wb_refs):
            wb_ref[...] = w_ref[...].astype(BF16)

    lhs = lhs_ref[...]
    acc = [jnp.dot(lhs, wb_ref[...], preferred_element_type=F32) for wb_ref in wb_refs]
    if mode == "swiglu":
        gate, up = acc
        o_ref[...] = (0.5 * _silu(gate) * up).astype(o_ref.dtype)
    elif mode == "resid":
        o_ref[...] = x_ref[...] + acc[0]
    else:
        for c in range(o_ref.shape[0]):
            o_ref[c] = acc[0][:, c * LANES:(c + 1) * LANES]


def _matmul(lhs, weights, layer, *, mode, tn, n_out, col_off=0, resid=None):
    m, k = lhs.shape
    tm = ROW_TILE
    n_w = len(weights)
    grid = (pl.cdiv(n_out, tn), m // tm)
    in_specs = [pl.BlockSpec((tm, k), lambda j, i: (i, 0))]
    in_specs += [pl.BlockSpec((None, k, tn), lambda j, i: (layer, 0, j + col_off)) for _ in weights]
    args = [lhs, *weights]
    if mode == "resid":
        in_specs.append(pl.BlockSpec((tm, tn), lambda j, i: (i, j)))
        args.append(resid)
    if mode == "cols":
        out_shape = jax.ShapeDtypeStruct((n_out // LANES, m, LANES), F32)
        out_spec = pl.BlockSpec((tn // LANES, tm, LANES), lambda j, i: (j, i, 0))
    else:
        out_shape = jax.ShapeDtypeStruct((m, n_out), BF16 if mode == "swiglu" else F32)
        out_spec = pl.BlockSpec((tm, tn), lambda j, i: (i, j))
    return pl.pallas_call(
        functools.partial(_matmul_kernel, n_w=n_w, mode=mode),
        out_shape=out_shape,
        grid=grid,
        in_specs=in_specs,
        out_specs=out_spec,
        scratch_shapes=[pltpu.VMEM((k, tn), BF16) for _ in weights],
        compiler_params=_params("arbitrary", "arbitrary"),
        name="matmul_" + mode,
    )(*args)


def _rope_kernel(inv_ref, cos_ref, sin_ref, *, pos0, step):
    shape = cos_ref.shape
    row = _iota(shape, 0) + pl.program_id(0) * shape[0]
    ang = (pos0 + step * row.astype(F32)) * inv_ref[...]
    sin = jnp.sin(ang)
    cos_ref[...] = jnp.cos(ang)
    sin_ref[...] = jnp.where(_iota(shape, 1) < HEAD_DIM // 2, -sin, sin)


def _rope_tables(rows, tile, pos0, step):
    inv = ROPE_BASE ** (-jnp.arange(0, HEAD_DIM, 2, dtype=F32) / HEAD_DIM)
    inv = jnp.concatenate([inv, inv]).reshape(1, HEAD_DIM)
    out = jax.ShapeDtypeStruct((rows, HEAD_DIM), F32)
    spec = pl.BlockSpec((tile, HEAD_DIM), lambda i: (i, 0))
    return pl.pallas_call(
        functools.partial(_rope_kernel, pos0=float(pos0), step=float(step)),
        out_shape=(out, out),
        grid=(rows // tile,),
        in_specs=[pl.BlockSpec((1, HEAD_DIM), lambda i: (0, 0))],
        out_specs=(spec, spec),
        compiler_params=_params("arbitrary"),
        name="rope_tables",
    )(inv)


def _rope(x, cos, sin):
    return x * cos + pltpu.roll(x, HEAD_DIM // 2, 1) * sin


def _hgrn_lower_bound(logits, layer):
    mx = jnp.max(logits, axis=0, keepdims=True)
    e = jnp.exp(logits - mx)
    p = e / jnp.sum(e, axis=0, keepdims=True)
    if layer == 0:
        return jnp.zeros_like(mx)
    return jnp.sum(p[1:layer + 1], axis=0, keepdims=True)


def _hgrn_gates(fz, lb):
    log_lb = jnp.log(lb)
    y = jnp.log1p(-lb) + _log_sigmoid(fz)
    log_f = jnp.maximum(log_lb, y) + jnp.log1p(jnp.exp(-jnp.abs(log_lb - y)))
    key = (1.0 - lb) / (1.0 + jnp.exp(fz))
    return log_f, key


def _hgrn_prompt_kernel(hq_ref, hf_ref, hi_ref, hg_ref, lbl_ref, gn_ref, o_ref, s_ref, st_ref, *, layer):
    t_len = hq_ref.shape[0]
    n_sub = BLK // SUB
    lb = _hgrn_lower_bound(lbl_ref[...], layer)
    gain = gn_ref[...]
    row = _iota((BLK, LANES), 0)
    col = _iota((BLK, LANES), 1)
    pos = row & (SUB - 1)
    seg0 = row - pos
    st_ref[...] = jnp.zeros_like(st_ref)

    def seg_row(x3, s):
        return jnp.broadcast_to(x3[:, s:s + 1, :], x3.shape).reshape(BLK, LANES)

    def block(i, carry):
        rows = pl.ds(pl.multiple_of(i * BLK, BLK), BLK)
        log_f, k = _hgrn_gates(hf_ref[rows, :], lb)
        q = _silu(hq_ref[rows, :])
        v = hi_ref[rows, :]
        b = _seg_cumsum(log_f, SUB)
        b3 = b.reshape(n_sub, SUB, LANES)
        k3 = k.reshape(n_sub, SUB, LANES)
        b_end = seg_row(b3, SUB - 1)
        q_in = (q * jnp.exp(b)).astype(BF16)
        k_out = k * jnp.exp(b_end - b)
        lam = jnp.exp(b_end)

        a = jnp.zeros((BLK, LANES), F32)
        for s in range(SUB):
            e = jnp.exp(jnp.where(pos >= s, b - seg_row(b3, s), -jnp.inf))
            c = jnp.sum(q * seg_row(k3, s) * e, axis=-1, keepdims=True)
            a = jnp.where(col == seg0 + s, c, a)
        vb = v.astype(BF16)
        o = jnp.dot(a.astype(BF16), vb, preferred_element_type=F32)

        v_t = v.T.astype(BF16)
        st = st_ref[...]
        parts = []
        for n in range(n_sub):
            lo = n * SUB
            parts.append(lax.dot_general(q_in[lo:lo + SUB], st.astype(BF16), _TRANS_B,
                                         preferred_element_type=F32))
            k_n = jnp.where((row >= lo) & (row < lo + SUB), k_out, 0.0).astype(BF16)
            st = st * lam[lo:lo + 1, :] + jnp.dot(v_t, k_n, preferred_element_type=F32)
        st_ref[...] = st
        o = o + jnp.concatenate(parts, axis=0)
        o_ref[rows, :] = _head_norm_gate(o, gain, _silu(hg_ref[rows, :])).astype(o_ref.dtype)
        return carry

    lax.fori_loop(0, t_len // BLK, block, 0)
    s_ref[...] = st_ref[...].T


def _hgrn_prompt(z, lb_logits, gains, layer, batch, t_len):
    n_heads = z.shape[0] // 4
    depth = lb_logits.shape[0]

    def zspec(group):
        return pl.BlockSpec((None, t_len, LANES), lambda b, h: (group * n_heads + h, b, 0))

    return pl.pallas_call(
        functools.partial(_hgrn_prompt_kernel, layer=layer),
        out_shape=(
            jax.ShapeDtypeStruct((batch * t_len, n_heads * HEAD_DIM), BF16),
            jax.ShapeDtypeStruct((batch, n_heads, HEAD_DIM, HEAD_DIM), F32),
        ),
        grid=(batch, n_heads),
        in_specs=[
            zspec(0), zspec(1), zspec(2), zspec(3),
            pl.BlockSpec((depth, LANES), lambda b, h: (0, h)),
            pl.BlockSpec((None, 1, LANES), lambda b, h: (layer, 0, h)),
        ],
        out_specs=(
            pl.BlockSpec((t_len, LANES), lambda b, h: (b, h)),
            pl.BlockSpec((None, None, HEAD_DIM, HEAD_DIM), lambda b, h: (b, h, 0, 0)),
        ),
        scratch_shapes=[pltpu.VMEM((HEAD_DIM, HEAD_DIM), F32)],
        compiler_params=_params("arbitrary", "arbitrary"),
        name="hgrn_prompt",
    )(z, z, z, z, lb_logits, gains)


def _hgrn_step_kernel(z_ref, s_ref, lbl_ref, gn_ref, o_ref, so_ref, f_scr, k_scr, q_scr, *, layer, n_heads):
    g = pl.program_id(0)
    n_rows = z_ref.shape[1]

    @pl.when(g == 0)
    def _():
        lb = _hgrn_lower_bound(lbl_ref[...], layer)
        for h in range(n_heads):
            log_f, k = _hgrn_gates(z_ref[n_heads + h], lb[:, h * LANES:(h + 1) * LANES])
            f_scr[h] = jnp.exp(log_f).T
            k_scr[h] = k.T
            q_scr[h] = _silu(z_ref[h]).T

    lane = _iota((HEAD_DIM, n_rows), 1)

    def body(j, carry):
        b = g * STEP_ROWS + j
        sel = lane == b
        one = pl.ds(b, 1)
        for h in range(n_heads):
            cols = slice(h * LANES, (h + 1) * LANES)
            s_new = _pick_lane(f_scr[h], sel) * s_ref[j, h] + _pick_lane(k_scr[h], sel) * z_ref[2 * n_heads + h, one, :]
            so_ref[j, h] = s_new
            o = jnp.sum(_pick_lane(q_scr[h], sel) * s_new, axis=0, keepdims=True)
            o_ref[j, :, cols] = _head_norm_gate(o, gn_ref[:, cols], _silu(z_ref[3 * n_heads + h, one, :]))
        return carry

    lax.fori_loop(0, STEP_ROWS, body, 0)


def _hgrn_step(z, row_block, state, lb_logits, gains, layer):
    n_heads = z.shape[0] // 4
    n_rows = state.shape[1]
    width = n_heads * HEAD_DIM
    return pl.pallas_call(
        functools.partial(_hgrn_step_kernel, layer=layer, n_heads=n_heads),
        out_shape=(
            jax.ShapeDtypeStruct((n_rows, 1, width), F32),
            jax.ShapeDtypeStruct(state.shape[1:], F32),
        ),
        grid=(n_rows // STEP_ROWS,),
        in_specs=[
            pl.BlockSpec((4 * n_heads, n_rows, LANES), lambda g: (0, row_block, 0)),
            pl.BlockSpec((None, STEP_ROWS, n_heads, HEAD_DIM, HEAD_DIM), lambda g: (layer, g, 0, 0, 0)),
            pl.BlockSpec(lb_logits.shape, lambda g: (0, 0)),
            pl.BlockSpec((None, 1, width), lambda g: (layer, 0, 0)),
        ],
        out_specs=(
            pl.BlockSpec((STEP_ROWS, 1, width), lambda g: (g, 0, 0)),
            pl.BlockSpec((STEP_ROWS, n_heads, HEAD_DIM, HEAD_DIM), lambda g: (g, 0, 0, 0)),
        ),
        scratch_shapes=[pltpu.VMEM((n_heads, HEAD_DIM, n_rows), F32) for _ in range(3)],
        compiler_params=_params("arbitrary"),
        name="hgrn_step",
    )(z, state, lb_logits, gains)


def _ret_prompt_kernel(rq_ref, rk_ref, rv_ref, rg_ref, cos_ref, sin_ref, lg_ref, gn_ref, o_ref, s_ref, st_ref):
    t_len = rq_ref.shape[0]
    log_gamma = lg_ref[pl.program_id(1)]
    gain = gn_ref[...]
    row = _iota((BLK, LANES), 0).astype(F32)
    diff = row - _iota((BLK, LANES), 1).astype(F32)
    decay = jnp.where(diff >= 0.0, jnp.exp(diff * log_gamma), 0.0)
    q_dec = jnp.exp((row + 1.0) * log_gamma)
    k_dec = jnp.exp((BLK - 1.0 - row) * log_gamma)
    s_dec = jnp.exp(jnp.full((1, LANES), BLK, F32) * log_gamma)
    st_ref[...] = jnp.zeros_like(st_ref)

    def chunk(i, carry):
        rows = pl.ds(pl.multiple_of(i * BLK, BLK), BLK)
        cos = cos_ref[rows, :]
        sin = sin_ref[rows, :]
        q = _rope(rq_ref[rows, :], cos, sin)
        k = _rope(rk_ref[rows, :], cos, sin) * K_SCALE
        vb = rv_ref[rows, :].astype(BF16)
        a = lax.dot_general(q.astype(BF16), k.astype(BF16), _TRANS_B, preferred_element_type=F32) * decay
        st = st_ref[...]
        o = jnp.dot(a.astype(BF16), vb, preferred_element_type=F32)
        o = o + jnp.dot((q * q_dec).astype(BF16), st.astype(BF16), preferred_element_type=F32)
        st_ref[...] = s_dec * st + jnp.dot((k * k_dec).T.astype(BF16), vb, preferred_element_type=F32)
        o_ref[rows, :] = _head_norm_gate(o, gain, _silu(rg_ref[rows, :])).astype(o_ref.dtype)
        return carry

    lax.fori_loop(0, t_len // BLK, chunk, 0)
    s_ref[...] = st_ref[...]


def _ret_prompt(z, cos, sin, log_gamma, gains, layer, batch, t_len):
    n_heads = z.shape[0] // 4

    def zspec(group):
        return pl.BlockSpec((None, t_len, LANES), lambda b, h: (group * n_heads + h, b, 0))

    table = pl.BlockSpec((t_len, LANES), lambda b, h: (0, 0))
    return pl.pallas_call(
        _ret_prompt_kernel,
        out_shape=(
            jax.ShapeDtypeStruct((batch * t_len, n_heads * HEAD_DIM), BF16),
            jax.ShapeDtypeStruct((batch, n_heads, HEAD_DIM, HEAD_DIM), F32),
        ),
        grid=(batch, n_heads),
        in_specs=[
            zspec(0), zspec(1), zspec(2), zspec(3), table, table,
            pl.BlockSpec(memory_space=pltpu.SMEM),
            pl.BlockSpec((None, 1, LANES), lambda b, h: (layer, 0, h)),
        ],
        out_specs=(
            pl.BlockSpec((t_len, LANES), lambda b, h: (b, h)),
            pl.BlockSpec((None, None, HEAD_DIM, HEAD_DIM), lambda b, h: (b, h, 0, 0)),
        ),
        scratch_shapes=[pltpu.VMEM((HEAD_DIM, HEAD_DIM), F32)],
        compiler_params=_params("arbitrary", "arbitrary"),
        name="ret_prompt",
    )(z, z, z, z, cos, sin, log_gamma, gains)


def _ret_step_kernel(z_ref, s_ref, cos_ref, sin_ref, lg_ref, gn_ref, o_ref, so_ref, k_scr, q_scr, *, n_heads):
    g = pl.program_id(0)
    n_rows = z_ref.shape[1]

    @pl.when(g == 0)
    def _():
        cos = cos_ref[0:1, :]
        sin = sin_ref[0:1, :]
        for h in range(n_heads):
            q_scr[h] = _rope(z_ref[h], cos, sin).T
            k_scr[h] = (_rope(z_ref[n_heads + h], cos, sin) * K_SCALE).T

    lane = _iota((HEAD_DIM, n_rows), 1)

    def body(j, carry):
        b = g * STEP_ROWS + j
        sel = lane == b
        one = pl.ds(b, 1)
        for h in range(n_heads):
            cols = slice(h * LANES, (h + 1) * LANES)
            gamma = jnp.exp(jnp.full((1, LANES), lg_ref[h], F32))
            s_new = gamma * s_ref[j, h] + _pick_lane(k_scr[h], sel) * z_ref[2 * n_heads + h, one, :]
            so_ref[j, h] = s_new
            o = jnp.sum(_pick_lane(q_scr[h], sel) * s_new, axis=0, keepdims=True)
            o_ref[j, :, cols] = _head_norm_gate(o, gn_ref[:, cols], _silu(z_ref[3 * n_heads + h, one, :]))
        return carry

    lax.fori_loop(0, STEP_ROWS, body, 0)


def _ret_step(z, row_block, state, cos, sin, log_gamma, gains, layer):
    n_heads = z.shape[0] // 4
    n_rows = state.shape[1]
    width = n_heads * HEAD_DIM
    table = pl.BlockSpec(cos.shape, lambda g: (0, 0))
    return pl.pallas_call(
        functools.partial(_ret_step_kernel, n_heads=n_heads),
        out_shape=(
            jax.ShapeDtypeStruct((n_rows, 1, width), F32),
            jax.ShapeDtypeStruct(state.shape[1:], F32),
        ),
        grid=(n_rows // STEP_ROWS,),
        in_specs=[
            pl.BlockSpec((4 * n_heads, n_rows, LANES), lambda g: (0, row_block, 0)),
            pl.BlockSpec((None, STEP_ROWS, n_heads, HEAD_DIM, HEAD_DIM), lambda g: (layer, g, 0, 0, 0)),
            table, table,
            pl.BlockSpec(memory_space=pltpu.SMEM),
            pl.BlockSpec((None, 1, width), lambda g: (layer, 0, 0)),
        ],
        out_specs=(
            pl.BlockSpec((STEP_ROWS, 1, width), lambda g: (g, 0, 0)),
            pl.BlockSpec((STEP_ROWS, n_heads, HEAD_DIM, HEAD_DIM), lambda g: (g, 0, 0, 0)),
        ),
        scratch_shapes=[pltpu.VMEM((n_heads, HEAD_DIM, n_rows), F32) for _ in range(2)],
        compiler_params=_params("arbitrary"),
        name="ret_step",
    )(z, state, cos, sin, log_gamma, gains)


def _mlstm_prompt_kernel(mq_ref, mk_ref, mv_ref, mo_ref, zg_ref, bias_ref, gn_ref,
                         o_ref, c_ref, n_ref, m_ref, c_scr, n_scr, m_scr, *, n_heads):
    t_len = mq_ref.shape[0]
    h = pl.program_id(1)
    gain = gn_ref[...]
    bias = bias_ref[...]
    lane = _iota((BLK, LANES), 1)
    causal = lane <= _iota((BLK, LANES), 0)
    c_scr[...] = jnp.zeros_like(c_scr)
    n_scr[...] = jnp.zeros_like(n_scr)
    m_scr[...] = jnp.zeros_like(m_scr)

    def chunk(i, carry):
        rows = pl.ds(pl.multiple_of(i * BLK, BLK), BLK)
        gates = zg_ref[rows, :] + bias
        log_i = jnp.broadcast_to(_pick_lane(gates, lane == h), (BLK, LANES))
        log_f = jnp.broadcast_to(_log_sigmoid(_pick_lane(gates, lane == h + n_heads)), (BLK, LANES))
        bt = _seg_cumsum(log_f, BLK)
        log_d = jnp.where(causal, bt + (log_i - bt).T, -jnp.inf)
        m_prev = m_scr[...]
        log_inter = bt + m_prev
        m_t = jnp.maximum(log_inter, jnp.max(log_d, axis=-1, keepdims=True))
        d_mat = jnp.exp(log_d - m_t)
        w_inter = jnp.exp(log_inter - m_t)
        q = mq_ref[rows, :]
        k = mk_ref[rows, :] * K_SCALE
        qb = q.astype(BF16)
        vb = mv_ref[rows, :].astype(BF16)
        sc = lax.dot_general(qb, k.astype(BF16), _TRANS_B, preferred_element_type=F32) * d_mat
        cm = c_scr[...]
        nm = n_scr[...]
        num = jnp.dot(sc.astype(BF16), vb, preferred_element_type=F32)
        num = num + jnp.dot(qb, cm.astype(BF16), preferred_element_type=F32) * w_inter
        den = jnp.sum(sc, axis=-1, keepdims=True) + w_inter * jnp.sum(q * nm, axis=-1, keepdims=True)
        hid = num / jnp.maximum(jnp.abs(den), jnp.exp(-m_t))
        m_end = m_t[BLK - 1:BLK, :]
        b_end = bt[BLK - 1:BLK, :]
        kw = k * jnp.exp(b_end - bt + log_i - m_end)
        w_c = jnp.exp(b_end + m_prev - m_end)
        c_scr[...] = w_c * cm + jnp.dot(kw.T.astype(BF16), vb, preferred_element_type=F32)
        n_scr[...] = w_c * nm + jnp.sum(kw, axis=0, keepdims=True)
        m_scr[...] = m_end
        o_ref[rows, :] = _head_norm_gate(hid, gain, jax.nn.sigmoid(mo_ref[rows, :])).astype(o_ref.dtype)
        return carry

    lax.fori_loop(0, t_len // BLK, chunk, 0)
    c_ref[...] = c_scr[...]
    n_ref[...] = n_scr[...]
    m_ref[...] = m_scr[...]


def _mlstm_prompt(z, zg, bias, gains, layer, batch, t_len):
    n_heads = z.shape[0] // 4

    def zspec(group):
        return pl.BlockSpec((None, t_len, LANES), lambda b, h: (group * n_heads + h, b, 0))

    vec = jax.ShapeDtypeStruct((batch, n_heads, 1, LANES), F32)
    vec_spec = pl.BlockSpec((None, None, 1, LANES), lambda b, h: (b, h, 0, 0))
    return pl.pallas_call(
        functools.partial(_mlstm_prompt_kernel, n_heads=n_heads),
        out_shape=(
            jax.ShapeDtypeStruct((batch * t_len, n_heads * HEAD_DIM), BF16),
            jax.ShapeDtypeStruct((batch, n_heads, HEAD_DIM, HEAD_DIM), F32),
            vec, vec,
        ),
        grid=(batch, n_heads),
        in_specs=[
            zspec(0), zspec(1), zspec(2), zspec(3),
            pl.BlockSpec((None, t_len, LANES), lambda b, h: (0, b, 0)),
            pl.BlockSpec((None, 1, LANES), lambda b, h: (layer, 0, 0)),
            pl.BlockSpec((None, 1, LANES), lambda b, h: (layer, 0, h)),
        ],
        out_specs=(
            pl.BlockSpec((t_len, LANES), lambda b, h: (b, h)),
            pl.BlockSpec((None, None, HEAD_DIM, HEAD_DIM), lambda b, h: (b, h, 0, 0)),
            vec_spec, vec_spec,
        ),
        scratch_shapes=[
            pltpu.VMEM((HEAD_DIM, HEAD_DIM), F32),
            pltpu.VMEM((1, LANES), F32),
            pltpu.VMEM((1, LANES), F32),
        ],
        compiler_params=_params("arbitrary", "arbitrary"),
        name="mlstm_prompt",
    )(z, z, z, z, zg, bias, gains)


def _mlstm_step_kernel(z_ref, zg_ref, bias_ref, c_ref, n_ref, m_ref, gn_ref,
                       o_ref, co_ref, no_ref, mo_ref, k_scr, q_scr, wc_scr, wk_scr, em_scr, *, n_heads):
    g = pl.program_id(0)
    n_rows = z_ref.shape[1]

    @pl.when(g == 0)
    def _():
        lanes = _iota((n_rows, LANES), 1)
        gates = zg_ref[...] + bias_ref[...]
        log_f = pltpu.roll(_log_sigmoid(gates), LANES - n_heads, 1)
        m_old = m_ref[...]
        m_new = jnp.maximum(log_f + m_old, gates)
        mo_ref[...] = m_new
        w_c = jnp.exp(log_f + m_old - m_new)
        w_k = jnp.exp(gates - m_new)
        e_m = jnp.exp(-m_new)
        for h in range(n_heads):
            sel = lanes == h
            wc_scr[h] = jnp.broadcast_to(_pick_lane(w_c, sel), (n_rows, LANES))
            wk_scr[h] = jnp.broadcast_to(_pick_lane(w_k, sel), (n_rows, LANES))
            em_scr[h] = jnp.broadcast_to(_pick_lane(e_m, sel), (n_rows, LANES))
            q_scr[h] = z_ref[h].T
            k_scr[h] = (z_ref[n_heads + h] * K_SCALE).T

    lane = _iota((HEAD_DIM, n_rows), 1)

    def body(j, carry):
        b = g * STEP_ROWS + j
        sel = lane == b
        one = pl.ds(b, 1)
        for h in range(n_heads):
            cols = slice(h * LANES, (h + 1) * LANES)
            w_c = wc_scr[h, one, :]
            w_k = wk_scr[h, one, :]
            c_new = w_c * c_ref[j, h] + (w_k * _pick_lane(k_scr[h], sel)) * z_ref[2 * n_heads + h, one, :]
            co_ref[j, h] = c_new
            n_new = w_c * n_ref[j, h:h + 1, :] + w_k * (z_ref[n_heads + h, one, :] * K_SCALE)
            no_ref[j, h:h + 1, :] = n_new
            num = jnp.sum(_pick_lane(q_scr[h], sel) * c_new, axis=0, keepdims=True)
            den = jnp.sum(z_ref[h, one, :] * n_new, axis=-1, keepdims=True)
            hid = num / jnp.maximum(jnp.abs(den), em_scr[h, one, :])
            gate = jax.nn.sigmoid(z_ref[3 * n_heads + h, one, :])
            o_ref[j, :, cols] = _head_norm_gate(hid, gn_ref[:, cols], gate)
        return carry

    lax.fori_loop(0, STEP_ROWS, body, 0)


def _mlstm_step(z, zg, row_block, bias, state_c, state_n, m_pad, gains, layer):
    n_heads = z.shape[0] // 4
    n_rows = state_c.shape[1]
    width = n_heads * HEAD_DIM
    tile = pl.BlockSpec((n_rows, LANES), lambda g: (0, 0))
    return pl.pallas_call(
        functools.partial(_mlstm_step_kernel, n_heads=n_heads),
        out_shape=(
            jax.ShapeDtypeStruct((n_rows, 1, width), F32),
            jax.ShapeDtypeStruct(state_c.shape[1:], F32),
            jax.ShapeDtypeStruct(state_n.shape[1:], F32),
            jax.ShapeDtypeStruct((n_rows, LANES), F32),
        ),
        grid=(n_rows // STEP_ROWS,),
        in_specs=[
            pl.BlockSpec((4 * n_heads, n_rows, LANES), lambda g: (0, row_block, 0)),
            pl.BlockSpec((None, n_rows, LANES), lambda g: (0, row_block, 0)),
            pl.BlockSpec((None, 1, LANES), lambda g: (layer, 0, 0)),
            pl.BlockSpec((None, STEP_ROWS, n_heads, HEAD_DIM, HEAD_DIM), lambda g: (layer, g, 0, 0, 0)),
            pl.BlockSpec((None, STEP_ROWS, n_heads, HEAD_DIM), lambda g: (layer, g, 0, 0)),
            tile,
            pl.BlockSpec((None, 1, width), lambda g: (layer, 0, 0)),
        ],
        out_specs=(
            pl.BlockSpec((STEP_ROWS, 1, width), lambda g: (g, 0, 0)),
            pl.BlockSpec((STEP_ROWS, n_heads, HEAD_DIM, HEAD_DIM), lambda g: (g, 0, 0, 0)),
            pl.BlockSpec((STEP_ROWS, n_heads, HEAD_DIM), lambda g: (g, 0, 0)),
            tile,
        ),
        scratch_shapes=[pltpu.VMEM((n_heads, HEAD_DIM, n_rows), F32) for _ in range(2)]
        + [pltpu.VMEM((n_heads, n_rows, LANES), F32) for _ in range(3)],
        compiler_params=_params("arbitrary"),
        name="mlstm_step",
    )(z, zg, bias, state_c, state_n, m_pad, gains)


def kernel(x_prompt, x_sample, state_hgrn, state_mlstm_c, state_mlstm_n, state_mlstm_m, state_ret, norm_ffn1, ffn1_w_gate, ffn1_w_up, ffn1_w_down, norm_mix, w_in, hgrn_lb_logits, mlstm_gate_bias, gn_hgrn, gn_mlstm, gn_ret, w_out, norm_ffn2, ffn2_w_gate, ffn2_w_up, ffn2_w_down, norm_final):
    bp, t_len, d = x_prompt.shape
    bs, dec_len, _ = x_sample.shape
    depth = w_in.shape[0]
    h_hgrn, h_mlstm, h_ret = state_hgrn.shape[2], state_mlstm_c.shape[2], state_ret.shape[2]
    w_hgrn, w_mlstm, w_ret = h_hgrn * HEAD_DIM, h_mlstm * HEAD_DIM, h_ret * HEAD_DIM
    d_ff = ffn1_w_gate.shape[2]
    n_prompt = bp * t_len
    assert dec_len == 1 and bs == LANES and n_prompt % bs == 0 and t_len % BLK == 0
    assert (n_prompt + bs) % ROW_TILE == 0 and (n_prompt + bs) % NORM_TILE == 0
    row_block = n_prompt // bs

    gate0 = 4 * w_hgrn + 4 * w_mlstm
    ret0 = gate0 + 2 * h_mlstm
    w_in_gate = jnp.pad(w_in[:, :, gate0:ret0], ((0, 0), (0, 0), (0, LANES - 2 * h_mlstm)))
    w_in_ret = w_in[:, :, ret0:]
    bias_pad = jnp.pad(mlstm_gate_bias, ((0, 0), (0, LANES - 2 * h_mlstm))).reshape(depth, 1, LANES)
    m_pad = jnp.pad(state_mlstm_m, ((0, 0), (0, 0), (0, LANES - h_mlstm)))
    gn_h = gn_hgrn.reshape(depth, 1, w_hgrn)
    gn_m = gn_mlstm.reshape(depth, 1, w_mlstm)
    gn_r = gn_ret.reshape(depth, 1, w_ret)
    log_gamma = jnp.log1p(-jnp.exp2(-5.0 - jnp.arange(h_ret, dtype=F32)))
    cos_p, sin_p = _rope_tables(t_len, ROPE_TILE, 0.0, 1.0)
    cos_s, sin_s = _rope_tables(8, 8, PAST_LEN, 0.0)

    x = jnp.concatenate([x_prompt.reshape(n_prompt, d), x_sample.reshape(bs, d)], axis=0)
    new_h_p, new_c_p, new_n_p, new_m_p, new_r_p = [], [], [], [], []
    new_h_s, new_c_s, new_n_s, new_m_s, new_r_s = [], [], [], [], []

    def ffn(x, norm, w_gate, w_up, w_down, layer):
        n = _rmsnorm(x, norm, layer, BF16)
        mid = _matmul(n, [w_gate, w_up], layer, mode="swiglu", tn=512, n_out=d_ff)
        return _matmul(mid, [w_down], layer, mode="resid", tn=256, n_out=d, resid=x)

    for layer in range(depth):
        x = ffn(x, norm_ffn1, ffn1_w_gate, ffn1_w_up, ffn1_w_down, layer)

        n = _rmsnorm(x, norm_mix, layer, BF16)
        z_h = _matmul(n, [w_in], layer, mode="cols", tn=512, n_out=4 * w_hgrn)
        z_m = _matmul(n, [w_in], layer, mode="cols", tn=512, n_out=4 * w_mlstm, col_off=4 * w_hgrn // 512)
        z_g = _matmul(n, [w_in_gate], layer, mode="cols", tn=LANES, n_out=LANES)
        z_r = _matmul(n, [w_in_ret], layer, mode="cols", tn=512, n_out=4 * w_ret)

        o_hp, s_hp = _hgrn_prompt(z_h, hgrn_lb_logits, gn_h, layer, bp, t_len)
        o_mp, c_p, n_p, m_p = _mlstm_prompt(z_m, z_g, bias_pad, gn_m, layer, bp, t_len)
        o_rp, s_rp = _ret_prompt(z_r, cos_p, sin_p, log_gamma, gn_r, layer, bp, t_len)
        o_hs, s_hs = _hgrn_step(z_h, row_block, state_hgrn, hgrn_lb_logits, gn_h, layer)
        o_ms, c_s, n_s, m_s = _mlstm_step(z_m, z_g, row_block, bias_pad, state_mlstm_c, state_mlstm_n, m_pad[layer], gn_m, layer)
        o_rs, s_rs = _ret_step(z_r, row_block, state_ret, cos_s, sin_s, log_gamma, gn_r, layer)
        new_h_p.append(s_hp)
        new_c_p.append(c_p)
        new_n_p.append(n_p[:, :, 0, :])
        new_m_p.append(m_p[:, :, 0, 0])
        new_r_p.append(s_rp)
        new_h_s.append(s_hs)
        new_c_s.append(c_s)
        new_n_s.append(n_s)
        new_m_s.append(m_s[:, :h_mlstm])
        new_r_s.append(s_rs)

        o = jnp.concatenate([
            jnp.concatenate([o_hp, o_mp, o_rp], axis=1),
            jnp.concatenate([o_hs, o_ms, o_rs], axis=2).reshape(bs, -1).astype(BF16),
        ], axis=0)
        x = _matmul(o, [w_out], layer, mode="resid", tn=512, n_out=d, resid=x)

        x = ffn(x, norm_ffn2, ffn2_w_gate, ffn2_w_up, ffn2_w_down, layer)

    y = _rmsnorm(x, norm_final, 0, F32)
    return (
        y[:n_prompt].reshape(bp, t_len, d), y[n_prompt:].reshape(bs, dec_len, d),
        jnp.stack(new_h_p), jnp.stack(new_c_p), jnp.stack(new_n_p), jnp.stack(new_m_p), jnp.stack(new_r_p),
        jnp.stack(new_h_s), jnp.stack(new_c_s), jnp.stack(new_n_s), jnp.stack(new_m_s), jnp.stack(new_r_s),
    )
```

```python
import functools

import jax
import jax.numpy as jnp
from jax import lax
from jax.experimental import pallas as pl
from jax.experimental.pallas import tpu as pltpu

F32 = jnp.float32
BF16 = jnp.bfloat16

HEAD_DIM = 128
LANES = 128
EPS = 1e-6
ROPE_BASE = 10000.0
K_SCALE = HEAD_DIM ** -0.5
PAST_LEN = 16384

BLK = 128
SUB = 16
SEQ_TILE = 512
STEP_ROWS = 16
NORM_TILE = 416
STACK_TILE = 1024
ROPE_TILE = 256
VMEM_LIMIT_BYTES = 56 * 1024 * 1024

_TRANS_B = (((1,), (1,)), ((), ()))


def _params(*semantics):
    return pltpu.CompilerParams(dimension_semantics=semantics, vmem_limit_bytes=VMEM_LIMIT_BYTES)


def _silu(x):
    return x * jax.nn.sigmoid(x)


def _log_sigmoid(x):
    return jnp.minimum(x, 0.0) - jnp.log1p(jnp.exp(-jnp.abs(x)))


def _iota(shape, dim):
    return lax.broadcasted_iota(jnp.int32, shape, dim)


def _seg_cumsum(x, seg):
    pos = _iota(x.shape, 0) & (seg - 1)
    shift = 1
    while shift < seg:
        x = x + jnp.where(pos >= shift, pltpu.roll(x, shift, 0), 0.0)
        shift *= 2
    return x


def _head_norm_gate(o, gain, gate):
    return o * lax.rsqrt(jnp.mean(o * o, axis=-1, keepdims=True) + EPS) * gain * gate


def _pick_lane(x, lane_mask):
    return jnp.sum(jnp.where(lane_mask, x, 0.0), axis=-1, keepdims=True)


def _put_row(tile, j, row):
    return jnp.where(_iota(tile.shape, 0) == j, row, tile)


def _stack_rows_kernel(a_ref, b_ref, o_ref, *, n_a):
    i = pl.program_id(0)

    @pl.when(i < n_a)
    def _():
        o_ref[...] = a_ref[...]

    @pl.when(i == n_a)
    def _():
        o_ref[0:b_ref.shape[0], :] = b_ref[...]


def _stack_rows(a, b):
    (ma, d), mb = a.shape, b.shape[0]
    n_a = ma // STACK_TILE
    return pl.pallas_call(
        functools.partial(_stack_rows_kernel, n_a=n_a),
        out_shape=jax.ShapeDtypeStruct((ma + mb, d), a.dtype),
        grid=(n_a + 1,),
        in_specs=[
            pl.BlockSpec((STACK_TILE, d), lambda i: (jnp.minimum(i, n_a - 1), 0)),
            pl.BlockSpec((mb, d), lambda i: (0, 0)),
        ],
        out_specs=pl.BlockSpec((STACK_TILE, d), lambda i: (i, 0)),
        compiler_params=_params("arbitrary"),
        name="stack_rows",
    )(a, b)


def _rmsnorm_kernel(x_ref, g_ref, o_ref):
    x = x_ref[...]
    y = x * lax.rsqrt(jnp.mean(x * x, axis=-1, keepdims=True) + EPS)
    o_ref[...] = (y * g_ref[...]).astype(o_ref.dtype)


def _rmsnorm(x, gains, layer, out_dtype):
    m, d = x.shape
    gains = gains.reshape(-1, 1, d)
    return pl.pallas_call(
        _rmsnorm_kernel,
        out_shape=jax.ShapeDtypeStruct((m, d), out_dtype),
        grid=(m // NORM_TILE,),
        in_specs=[
            pl.BlockSpec((NORM_TILE, d), lambda i: (i, 0)),
            pl.BlockSpec((None, 1, d), lambda i: (layer, 0, 0)),
        ],
        out_specs=pl.BlockSpec((NORM_TILE, d), lambda i: (i, 0)),
        compiler_params=_params("arbitrary"),
        name="rmsnorm",
    )(x, gains)


def _matmul_kernel(*refs, n_w, mode, lhs_order):
    lhs_ref = refs[0]
    w_refs = refs[1:1 + n_w]
    pos = 1 + n_w
    x_ref = None
    if mode == "resid":
        x_ref = refs[pos]
        pos += 1
    o_ref = refs[pos]
    wb_refs = refs[pos + 1:pos + 1 + n_w]

    @pl.when(pl.program_id(1) == 0)
    def _():
        for w_ref, wb_ref in zip(w_refs, wb_refs):
            wb_ref[...] = w_ref[...].astype(BF16)

    if lhs_order is None:
        lhs = lhs_ref[...]
    else:
        lhs = jnp.concatenate([lhs_ref[c] for c in lhs_order], axis=1)
    acc = [jnp.dot(lhs, wb_ref[...], preferred_element_type=F32) for wb_ref in wb_refs]
    if mode == "swiglu":
        gate, up = acc
        o_ref[...] = (0.5 * _silu(gate) * up).astype(o_ref.dtype)
    elif mode == "resid":
        o_ref[...] = x_ref[...] + acc[0]
    else:
        for c in range(o_ref.shape[0]):
            o_ref[c] = acc[0][:, c * LANES:(c + 1) * LANES]


def _matmul(lhs, weights, layer, *, mode, tm, tn, n_out, col_off=0, resid=None, weight_buffers=2, lhs_order=None):
    if lhs_order is None:
        m, k = lhs.shape
        lhs_spec = pl.BlockSpec((tm, k), lambda j, i: (i, 0))
    else:
        m, k = lhs.shape[1], lhs.shape[0] * LANES
        lhs_spec = pl.BlockSpec((lhs.shape[0], tm, LANES), lambda j, i: (0, i, 0))
    n_w = len(weights)
    grid = (pl.cdiv(n_out, tn), m // tm)
    in_specs = [lhs_spec]
    in_specs += [pl.BlockSpec((None, k, tn), lambda j, i: (layer, 0, j + col_off),
                              pipeline_mode=pl.Buffered(weight_buffers)) for _ in weights]
    args = [lhs, *weights]
    if mode == "resid":
        in_specs.append(pl.BlockSpec((tm, tn), lambda j, i: (i, j)))
        args.append(resid)
    if mode == "cols":
        out_shape = jax.ShapeDtypeStruct((n_out // LANES, m, LANES), F32)
        out_spec = pl.BlockSpec((tn // LANES, tm, LANES), lambda j, i: (j, i, 0))
    else:
        out_shape = jax.ShapeDtypeStruct((m, n_out), BF16 if mode == "swiglu" else F32)
        out_spec = pl.BlockSpec((tm, tn), lambda j, i: (i, j))
    return pl.pallas_call(
        functools.partial(_matmul_kernel, n_w=n_w, mode=mode, lhs_order=lhs_order),
        out_shape=out_shape,
        grid=grid,
        in_specs=in_specs,
        out_specs=out_spec,
        scratch_shapes=[pltpu.VMEM((k, tn), BF16) for _ in weights],
        compiler_params=_params("arbitrary", "arbitrary"),
        name="matmul_" + mode,
    )(*args)


def _rope_kernel(inv_ref, cos_ref, sin_ref, *, pos0, step):
    shape = cos_ref.shape
    row = _iota(shape, 0) + pl.program_id(0) * shape[0]
    ang = (pos0 + step * row.astype(F32)) * inv_ref[...]
    sin = jnp.sin(ang)
    cos_ref[...] = jnp.cos(ang)
    sin_ref[...] = jnp.where(_iota(shape, 1) < HEAD_DIM // 2, -sin, sin)


def _rope_tables(rows, tile, pos0, step):
    inv = ROPE_BASE ** (-jnp.arange(0, HEAD_DIM, 2, dtype=F32) / HEAD_DIM)
    inv = jnp.concatenate([inv, inv]).reshape(1, HEAD_DIM)
    out = jax.ShapeDtypeStruct((rows, HEAD_DIM), F32)
    spec = pl.BlockSpec((tile, HEAD_DIM), lambda i: (i, 0))
    return pl.pallas_call(
        functools.partial(_rope_kernel, pos0=float(pos0), step=float(step)),
        out_shape=(out, out),
        grid=(rows // tile,),
        in_specs=[pl.BlockSpec((1, HEAD_DIM), lambda i: (0, 0))],
        out_specs=(spec, spec),
        compiler_params=_params("arbitrary"),
        name="rope_tables",
    )(inv)


def _rope(x, cos, sin):
    return x * cos + pltpu.roll(x, HEAD_DIM // 2, 1) * sin


def _mixer_call(kernel_fn, name, grid, in_specs, args, outs, scratch, semantics):
    in_specs, args, aliases = list(in_specs), list(args), {}
    for k, (_, _, _, prev) in enumerate(outs):
        if prev is not None:
            aliases[len(args)] = k
            in_specs.append(pl.BlockSpec(memory_space=pl.ANY))
            args.append(prev)
    n_alias = len(aliases)

    def body(*refs):
        n_in = len(args) - n_alias
        kernel_fn(*refs[:n_in], *refs[n_in + n_alias:])

    return pl.pallas_call(
        body,
        out_shape=tuple(jax.ShapeDtypeStruct(s, dt) for s, dt, _, _ in outs),
        grid=grid,
        in_specs=in_specs,
        out_specs=tuple(spec for _, _, spec, _ in outs),
        scratch_shapes=scratch,
        input_output_aliases=aliases,
        compiler_params=_params(*semantics),
        name=name,
    )(*args)


def _state_out(state, layer, prev):
    tail = state.shape[2:]
    zeros = (0,) * len(tail)
    spec = pl.BlockSpec((None, STEP_ROWS) + tail, lambda g: (layer, g) + zeros)
    return (state.shape, F32, spec, prev)


def _hgrn_lower_bound(logits, layer):
    mx = jnp.max(logits, axis=0, keepdims=True)
    e = jnp.exp(logits - mx)
    p = e / jnp.sum(e, axis=0, keepdims=True)
    if layer == 0:
        return jnp.zeros_like(mx)
    return jnp.sum(p[1:layer + 1], axis=0, keepdims=True)


def _hgrn_gates(fz, lb):
    log_lb = jnp.log(lb)
    y = jnp.log1p(-lb) + _log_sigmoid(fz)
    log_f = jnp.maximum(log_lb, y) + jnp.log1p(jnp.exp(-jnp.abs(log_lb - y)))
    key = (1.0 - lb) / (1.0 + jnp.exp(fz))
    return log_f, key


def _hgrn_prompt_kernel(z_ref, lbl_ref, gn_ref, o_ref, s_ref, st_ref, *, layer, n_heads):
    t = pl.program_id(1)
    n_sub = BLK // SUB
    lb = _hgrn_lower_bound(lbl_ref[...], layer)
    row = _iota((BLK, LANES), 0)
    col = _iota((BLK, LANES), 1)
    pos = row & (SUB - 1)
    seg0 = row - pos

    @pl.when(t == 0)
    def _():
        st_ref[...] = jnp.zeros_like(st_ref)

    def seg_row(x3, s):
        return jnp.broadcast_to(x3[:, s:s + 1, :], x3.shape).reshape(BLK, LANES)

    def block(i, carry):
        rows = pl.ds(pl.multiple_of(i * BLK, BLK), BLK)
        for h in range(n_heads):
            cols = slice(h * LANES, (h + 1) * LANES)
            log_f, k = _hgrn_gates(z_ref[n_heads + h, rows, :], lb[:, cols])
            q = _silu(z_ref[h, rows, :])
            v = z_ref[2 * n_heads + h, rows, :]
            b = _seg_cumsum(log_f, SUB)
            b3 = b.reshape(n_sub, SUB, LANES)
            k3 = k.reshape(n_sub, SUB, LANES)
            b_end = seg_row(b3, SUB - 1)
            q_in = (q * jnp.exp(b)).astype(BF16)
            k_out = k * jnp.exp(b_end - b)
            lam = jnp.exp(b_end)

            a = jnp.zeros((BLK, LANES), F32)
            for s in range(SUB):
                e = jnp.exp(jnp.where(pos >= s, b - seg_row(b3, s), -jnp.inf))
                c = jnp.sum(q * seg_row(k3, s) * e, axis=-1, keepdims=True)
                a = jnp.where(col == seg0 + s, c, a)
            o = jnp.dot(a.astype(BF16), v.astype(BF16), preferred_element_type=F32)

            v_t = v.T.astype(BF16)
            st = st_ref[h]
            parts = []
            for n in range(n_sub):
                lo = n * SUB
                parts.append(lax.dot_general(q_in[lo:lo + SUB], st.astype(BF16), _TRANS_B,
                                             preferred_element_type=F32))
                k_n = jnp.where((row >= lo) & (row < lo + SUB), k_out, 0.0).astype(BF16)
                st = st * lam[lo:lo + 1, :] + jnp.dot(v_t, k_n, preferred_element_type=F32)
            st_ref[h] = st
            o = o + jnp.concatenate(parts, axis=0)
            gate = _silu(z_ref[3 * n_heads + h, rows, :])
            o_ref[h, rows, :] = _head_norm_gate(o, gn_ref[:, cols], gate).astype(o_ref.dtype)
        return carry

    lax.fori_loop(0, z_ref.shape[1] // BLK, block, 0)

    @pl.when(t == pl.num_programs(1) - 1)
    def _():
        for h in range(n_heads):
            s_ref[h] = st_ref[h].T


def _hgrn_step_kernel(z_ref, s_ref, lbl_ref, gn_ref, o_ref, so_ref, f_scr, k_scr, q_scr, *, layer, n_heads):
    g = pl.program_id(0)
    n_rows = z_ref.shape[1]

    @pl.when(g == 0)
    def _():
        lb = _hgrn_lower_bound(lbl_ref[...], layer)
        for h in range(n_heads):
            log_f, k = _hgrn_gates(z_ref[n_heads + h], lb[:, h * LANES:(h + 1) * LANES])
            f_scr[h] = jnp.exp(log_f).T
            k_scr[h] = k.T
            q_scr[h] = _silu(z_ref[h]).T

    lane = _iota((HEAD_DIM, n_rows), 1)

    def body(j, outs):
        b = g * STEP_ROWS + j
        sel = lane == b
        one = pl.ds(b, 1)
        new = []
        for h in range(n_heads):
            cols = slice(h * LANES, (h + 1) * LANES)
            s_new = _pick_lane(f_scr[h], sel) * s_ref[j, h] + _pick_lane(k_scr[h], sel) * z_ref[2 * n_heads + h, one, :]
            so_ref[j, h] = s_new
            o = jnp.sum(_pick_lane(q_scr[h], sel) * s_new, axis=0, keepdims=True)
            o = _head_norm_gate(o, gn_ref[:, cols], _silu(z_ref[3 * n_heads + h, one, :]))
            new.append(_put_row(outs[h], j, o))
        return tuple(new)

    zero = jnp.zeros((STEP_ROWS, LANES), F32)
    outs = lax.fori_loop(0, STEP_ROWS, body, (zero,) * n_heads)
    for h in range(n_heads):
        o_ref[h] = outs[h].astype(o_ref.dtype)


def _ret_prompt_kernel(z_ref, cos_ref, sin_ref, lg_ref, gn_ref, o_ref, s_ref, st_ref, dec_ref, *, n_heads):
    t = pl.program_id(1)

    @pl.when((pl.program_id(0) == 0) & (t == 0))
    def _():
        row = _iota((BLK, LANES), 0).astype(F32)
        diff = row - _iota((BLK, LANES), 1).astype(F32)
        for h in range(n_heads):
            lg = lg_ref[h]
            dec_ref[h, 0] = jnp.where(diff >= 0.0, jnp.exp(diff * lg), 0.0)
            dec_ref[h, 1] = jnp.exp((row + 1.0) * lg)
            dec_ref[h, 2] = jnp.exp((BLK - 1.0 - row) * lg)

    @pl.when(t == 0)
    def _():
        st_ref[...] = jnp.zeros_like(st_ref)

    def chunk(i, carry):
        rows = pl.ds(pl.multiple_of(i * BLK, BLK), BLK)
        tab = pl.ds(pl.multiple_of(t * z_ref.shape[1] + i * BLK, BLK), BLK)
        cos = cos_ref[tab, :]
        sin = sin_ref[tab, :]
        for h in range(n_heads):
            cols = slice(h * LANES, (h + 1) * LANES)
            q = _rope(z_ref[h, rows, :], cos, sin)
            k = _rope(z_ref[n_heads + h, rows, :], cos, sin) * K_SCALE
            vb = z_ref[2 * n_heads + h, rows, :].astype(BF16)
            a = lax.dot_general(q.astype(BF16), k.astype(BF16), _TRANS_B, preferred_element_type=F32) * dec_ref[h, 0]
            st = st_ref[h]
            o = jnp.dot(a.astype(BF16), vb, preferred_element_type=F32)
            o = o + jnp.dot((q * dec_ref[h, 1]).astype(BF16), st.astype(BF16), preferred_element_type=F32)
            s_dec = jnp.exp(jnp.full((1, LANES), BLK, F32) * lg_ref[h])
            st_ref[h] = s_dec * st + jnp.dot((k * dec_ref[h, 2]).T.astype(BF16), vb, preferred_element_type=F32)
            gate = _silu(z_ref[3 * n_heads + h, rows, :])
            o_ref[h, rows, :] = _head_norm_gate(o, gn_ref[:, cols], gate).astype(o_ref.dtype)
        return carry

    lax.fori_loop(0, z_ref.shape[1] // BLK, chunk, 0)

    @pl.when(t == pl.num_programs(1) - 1)
    def _():
        s_ref[...] = st_ref[...]


def _ret_step_kernel(z_ref, s_ref, cos_ref, sin_ref, lg_ref, gn_ref, o_ref, so_ref, k_scr, q_scr, *, n_heads):
    g = pl.program_id(0)
    n_rows = z_ref.shape[1]

    @pl.when(g == 0)
    def _():
        cos = cos_ref[0:1, :]
        sin = sin_ref[0:1, :]
        for h in range(n_heads):
            q_scr[h] = _rope(z_ref[h], cos, sin).T
            k_scr[h] = (_rope(z_ref[n_heads + h], cos, sin) * K_SCALE).T

    lane = _iota((HEAD_DIM, n_rows), 1)

    def body(j, outs):
        b = g * STEP_ROWS + j
        sel = lane == b
        one = pl.ds(b, 1)
        new = []
        for h in range(n_heads):
            cols = slice(h * LANES, (h + 1) * LANES)
            gamma = jnp.exp(jnp.full((1, LANES), lg_ref[h], F32))
            s_new = gamma * s_ref[j, h] + _pick_lane(k_scr[h], sel) * z_ref[2 * n_heads + h, one, :]
            so_ref[j, h] = s_new
            o = jnp.sum(_pick_lane(q_scr[h], sel) * s_new, axis=0, keepdims=True)
            o = _head_norm_gate(o, gn_ref[:, cols], _silu(z_ref[3 * n_heads + h, one, :]))
            new.append(_put_row(outs[h], j, o))
        return tuple(new)

    zero = jnp.zeros((STEP_ROWS, LANES), F32)
    outs = lax.fori_loop(0, STEP_ROWS, body, (zero,) * n_heads)
    for h in range(n_heads):
        o_ref[h] = outs[h].astype(o_ref.dtype)


def _mlstm_prompt_kernel(z_ref, zg_ref, bias_ref, gn_ref, o_ref, c_ref, n_ref, m_ref,
                         c_scr, n_scr, m_scr, *, n_heads):
    t = pl.program_id(1)
    lane = _iota((BLK, LANES), 1)
    causal = lane <= _iota((BLK, LANES), 0)

    @pl.when(t == 0)
    def _():
        c_scr[...] = jnp.zeros_like(c_scr)
        n_scr[...] = jnp.zeros_like(n_scr)
        m_scr[...] = jnp.zeros_like(m_scr)

    def chunk(i, carry):
        rows = pl.ds(pl.multiple_of(i * BLK, BLK), BLK)
        gates = zg_ref[rows, :] + bias_ref[...]
        log_f_all = pltpu.roll(_log_sigmoid(gates), LANES - n_heads, 1)
        bt_all = _seg_cumsum(log_f_all, BLK)
        w_t = (gates - bt_all).T
        for h in range(n_heads):
            cols = slice(h * LANES, (h + 1) * LANES)
            sel = lane == h
            log_i = _pick_lane(gates, sel)
            bt = _pick_lane(bt_all, sel)
            log_d = jnp.where(causal, bt + w_t[h:h + 1, :], -jnp.inf)
            m_prev = m_scr[h]
            log_inter = bt + m_prev
            m_t = jnp.maximum(log_inter, jnp.max(log_d, axis=-1, keepdims=True))
            d_mat = jnp.exp(log_d - m_t)
            w_inter = jnp.exp(log_inter - m_t)
            q = z_ref[h, rows, :]
            k = z_ref[n_heads + h, rows, :] * K_SCALE
            qb = q.astype(BF16)
            vb = z_ref[2 * n_heads + h, rows, :].astype(BF16)
            sc = lax.dot_general(qb, k.astype(BF16), _TRANS_B, preferred_element_type=F32) * d_mat
            cm = c_scr[h]
            nm = n_scr[h]
            num = jnp.dot(sc.astype(BF16), vb, preferred_element_type=F32)
            num = num + jnp.dot(qb, cm.astype(BF16), preferred_element_type=F32) * w_inter
            den = jnp.sum(sc, axis=-1, keepdims=True) + w_inter * jnp.sum(q * nm, axis=-1, keepdims=True)
            hid = num / jnp.maximum(jnp.abs(den), jnp.exp(-m_t))
            m_end = m_t[BLK - 1:BLK, :]
            b_end = bt[BLK - 1:BLK, :]
            kw = k * jnp.exp(b_end - bt + log_i - m_end)
            w_c = jnp.exp(b_end + m_prev - m_end)
            c_scr[h] = w_c * cm + jnp.dot(kw.T.astype(BF16), vb, preferred_element_type=F32)
            n_scr[h] = w_c * nm + jnp.sum(kw, axis=0, keepdims=True)
            m_scr[h] = m_end
            gate = jax.nn.sigmoid(z_ref[3 * n_heads + h, rows, :])
            o_ref[h, rows, :] = _head_norm_gate(hid, gn_ref[:, cols], gate).astype(o_ref.dtype)
        return carry

    lax.fori_loop(0, z_ref.shape[1] // BLK, chunk, 0)

    @pl.when(t == pl.num_programs(1) - 1)
    def _():
        c_ref[...] = c_scr[...]
        n_ref[...] = n_scr[...]
        m_ref[...] = m_scr[...]


def _mlstm_step_kernel(z_ref, zg_ref, bias_ref, c_ref, n_ref, m_ref, gn_ref,
                       o_ref, co_ref, no_ref, mo_ref, k_scr, q_scr, wc_scr, wk_scr, em_scr, *, n_heads):
    g = pl.program_id(0)
    n_rows = z_ref.shape[1]

    @pl.when(g == 0)
    def _():
        lanes = _iota((n_rows, LANES), 1)
        gates = zg_ref[...] + bias_ref[...]
        log_f = pltpu.roll(_log_sigmoid(gates), LANES - n_heads, 1)
        m_old = m_ref[...]
        m_new = jnp.maximum(log_f + m_old, gates)
        mo_ref[...] = m_new
        w_c = jnp.exp(log_f + m_old - m_new)
        w_k = jnp.exp(gates - m_new)
        e_m = jnp.exp(-m_new)
        for h in range(n_heads):
            sel = lanes == h
            wc_scr[h] = jnp.broadcast_to(_pick_lane(w_c, sel), (n_rows, LANES))
            wk_scr[h] = jnp.broadcast_to(_pick_lane(w_k, sel), (n_rows, LANES))
            em_scr[h] = jnp.broadcast_to(_pick_lane(e_m, sel), (n_rows, LANES))
            q_scr[h] = z_ref[h].T
            k_scr[h] = (z_ref[n_heads + h] * K_SCALE).T

    lane = _iota((HEAD_DIM, n_rows), 1)

    def body(j, outs):
        b = g * STEP_ROWS + j
        sel = lane == b
        one = pl.ds(b, 1)
        new = []
        for h in range(n_heads):
            cols = slice(h * LANES, (h + 1) * LANES)
            w_c = wc_scr[h, one, :]
            w_k = wk_scr[h, one, :]
            c_new = w_c * c_ref[j, h] + (w_k * _pick_lane(k_scr[h], sel)) * z_ref[2 * n_heads + h, one, :]
            co_ref[j, h] = c_new
            n_new = w_c * n_ref[j, h:h + 1, :] + w_k * (z_ref[n_heads + h, one, :] * K_SCALE)
            no_ref[j, h:h + 1, :] = n_new
            num = jnp.sum(_pick_lane(q_scr[h], sel) * c_new, axis=0, keepdims=True)
            den = jnp.sum(z_ref[h, one, :] * n_new, axis=-1, keepdims=True)
            hid = num / jnp.maximum(jnp.abs(den), em_scr[h, one, :])
            gate = jax.nn.sigmoid(z_ref[3 * n_heads + h, one, :])
            new.append(_put_row(outs[h], j, _head_norm_gate(hid, gn_ref[:, cols], gate)))
        return tuple(new)

    zero = jnp.zeros((STEP_ROWS, LANES), F32)
    outs = lax.fori_loop(0, STEP_ROWS, body, (zero,) * n_heads)
    for h in range(n_heads):
        o_ref[h] = outs[h].astype(o_ref.dtype)


def kernel(x_prompt, x_sample, state_hgrn, state_mlstm_c, state_mlstm_n, state_mlstm_m, state_ret, norm_ffn1, ffn1_w_gate, ffn1_w_up, ffn1_w_down, norm_mix, w_in, hgrn_lb_logits, mlstm_gate_bias, gn_hgrn, gn_mlstm, gn_ret, w_out, norm_ffn2, ffn2_w_gate, ffn2_w_up, ffn2_w_down, norm_final):
    bp, t_len, d = x_prompt.shape
    bs, dec_len, _ = x_sample.shape
    depth = w_in.shape[0]
    h_hgrn, h_mlstm, h_ret = state_hgrn.shape[2], state_mlstm_c.shape[2], state_ret.shape[2]
    w_hgrn, w_mlstm, w_ret = h_hgrn * HEAD_DIM, h_mlstm * HEAD_DIM, h_ret * HEAD_DIM
    d_ff = ffn1_w_gate.shape[2]
    n_prompt = bp * t_len
    m_rows = n_prompt + bs
    seq_tile = min(SEQ_TILE, t_len)
    n_seq = t_len // seq_tile
    assert dec_len == 1 and bs == LANES and n_prompt % bs == 0 and t_len % seq_tile == 0 and seq_tile % BLK == 0
    row_block = n_prompt // bs
    step_block = n_prompt // STEP_ROWS
    tm = m_rows // 8 if m_rows % 128 == 0 else m_rows
    assert m_rows % tm == 0 and tm % 16 == 0 and m_rows % NORM_TILE == 0 and n_prompt % STACK_TILE == 0
    o_mlstm, o_ret, o_hgrn = 0, h_mlstm, h_mlstm + h_ret
    assert o_ret % h_ret == 0 and o_hgrn % h_hgrn == 0
    n_slabs = h_hgrn + h_mlstm + h_ret
    o_order = tuple(range(o_hgrn, n_slabs)) + tuple(range(o_mlstm, o_ret)) + tuple(range(o_ret, o_hgrn))

    gate0 = 4 * w_hgrn + 4 * w_mlstm
    ret0 = gate0 + 2 * h_mlstm
    w_in_ret = jnp.concatenate(
        [w_in[:, :, ret0:], jnp.pad(w_in[:, :, gate0:ret0], ((0, 0), (0, 0), (0, LANES - 2 * h_mlstm)))], axis=2)
    gate_slab = 4 * h_ret
    bias_pad = jnp.pad(mlstm_gate_bias, ((0, 0), (0, LANES - 2 * h_mlstm))).reshape(depth, 1, LANES)
    m_pad = jnp.pad(state_mlstm_m, ((0, 0), (0, 0), (0, LANES - h_mlstm)))
    gn_h = gn_hgrn.reshape(depth, 1, w_hgrn)
    gn_m = gn_mlstm.reshape(depth, 1, w_mlstm)
    gn_r = gn_ret.reshape(depth, 1, w_ret)
    log_gamma = jnp.log1p(-jnp.exp2(-5.0 - jnp.arange(h_ret, dtype=F32)))
    cos_p, sin_p = _rope_tables(t_len, min(ROPE_TILE, t_len), 0.0, 1.0)
    cos_s, sin_s = _rope_tables(8, 8, PAST_LEN, 0.0)

    smem = pl.BlockSpec(memory_space=pltpu.SMEM)
    o3_shape = (n_slabs, m_rows, LANES)
    head_sq = (HEAD_DIM, HEAD_DIM)

    def z_prompt(n_blocks):
        return pl.BlockSpec((n_blocks, seq_tile, LANES), lambda b, t: (0, b * n_seq + t, 0))

    def o_prompt(heads, slab0):
        return pl.BlockSpec((heads, seq_tile, LANES), lambda b, t: (slab0 // heads, b * n_seq + t, 0))

    def o_step(heads, slab0):
        return pl.BlockSpec((heads, STEP_ROWS, LANES), lambda g: (slab0 // heads, step_block + g, 0))

    x = _stack_rows(x_prompt.reshape(n_prompt, d), x_sample.reshape(bs, d))
    new_h_p, new_c_p, new_n_p, new_m_p, new_r_p = [], [], [], [], []
    new_n_s, new_m_s = [], []
    st_h = st_c = st_r = None

    def ffn(x, norm, w_gate, w_up, w_down, layer):
        n = _rmsnorm(x, norm, layer, BF16)
        mid = _matmul(n, [w_gate, w_up], layer, mode="swiglu", tm=2 * tm, tn=512, n_out=d_ff)
        return _matmul(mid, [w_down], layer, mode="resid", tm=tm, tn=512, n_out=d, resid=x, weight_buffers=1)

    for layer in range(depth):
        x = ffn(x, norm_ffn1, ffn1_w_gate, ffn1_w_up, ffn1_w_down, layer)

        n = _rmsnorm(x, norm_mix, layer, BF16)
        z_h = _matmul(n, [w_in], layer, mode="cols", tm=tm, tn=1024, n_out=4 * w_hgrn)
        z_m = _matmul(n, [w_in], layer, mode="cols", tm=tm, tn=1024, n_out=4 * w_mlstm, col_off=4 * w_hgrn // 1024)
        z_r = _matmul(n, [w_in_ret], layer, mode="cols", tm=tm, tn=640, n_out=4 * w_ret + LANES)

        def gain2(width):
            return pl.BlockSpec((None, 1, width), lambda b, t: (layer, 0, 0))

        def gain1(width):
            return pl.BlockSpec((None, 1, width), lambda g: (layer, 0, 0))

        o3, s_hp = _mixer_call(
            functools.partial(_hgrn_prompt_kernel, layer=layer, n_heads=h_hgrn), "hgrn_prompt", (bp, n_seq),
            [z_prompt(4 * h_hgrn), pl.BlockSpec(hgrn_lb_logits.shape, lambda b, t: (0, 0)), gain2(w_hgrn)],
            [z_h, hgrn_lb_logits, gn_h],
            [(o3_shape, BF16, o_prompt(h_hgrn, o_hgrn), None),
             ((bp, h_hgrn) + head_sq, F32, pl.BlockSpec((None, h_hgrn) + head_sq, lambda b, t: (b, 0, 0, 0)), None)],
            [pltpu.VMEM((h_hgrn,) + head_sq, F32)], ("arbitrary", "arbitrary"))

        vec = ((bp, h_mlstm, 1, LANES), F32, pl.BlockSpec((None, h_mlstm, 1, LANES), lambda b, t: (b, 0, 0, 0)), None)
        o3, c_p, n_p, m_p = _mixer_call(
            functools.partial(_mlstm_prompt_kernel, n_heads=h_mlstm), "mlstm_prompt", (bp, n_seq),
            [z_prompt(4 * h_mlstm),
             pl.BlockSpec((None, seq_tile, LANES), lambda b, t: (gate_slab, b * n_seq + t, 0)),
             pl.BlockSpec((None, 1, LANES), lambda b, t: (layer, 0, 0)), gain2(w_mlstm)],
            [z_m, z_r, bias_pad, gn_m],
            [(o3_shape, BF16, o_prompt(h_mlstm, o_mlstm), o3),
             ((bp, h_mlstm) + head_sq, F32, pl.BlockSpec((None, h_mlstm) + head_sq, lambda b, t: (b, 0, 0, 0)), None),
             vec, vec],
            [pltpu.VMEM((h_mlstm,) + head_sq, F32), pltpu.VMEM((h_mlstm, 1, LANES), F32),
             pltpu.VMEM((h_mlstm, 1, LANES), F32)], ("arbitrary", "arbitrary"))

        table = pl.BlockSpec((t_len, LANES), lambda b, t: (0, 0))
        o3, s_rp = _mixer_call(
            functools.partial(_ret_prompt_kernel, n_heads=h_ret), "ret_prompt", (bp, n_seq),
            [z_prompt(4 * h_ret), table, table, smem, gain2(w_ret)],
            [z_r, cos_p, sin_p, log_gamma, gn_r],
            [(o3_shape, BF16, o_prompt(h_ret, o_ret), o3),
             ((bp, h_ret) + head_sq, F32, pl.BlockSpec((None, h_ret) + head_sq, lambda b, t: (b, 0, 0, 0)), None)],
            [pltpu.VMEM((h_ret,) + head_sq, F32), pltpu.VMEM((h_ret, 3, BLK, LANES), F32)],
            ("arbitrary", "arbitrary"))

        def z_step(n_blocks):
            return pl.BlockSpec((n_blocks, bs, LANES), lambda g: (0, row_block, 0))

        def state_in(state):
            tail = state.shape[2:]
            zeros = (0,) * len(tail)
            return pl.BlockSpec((None, STEP_ROWS) + tail, lambda g: (layer, g) + zeros)

        tposed = pltpu.VMEM((h_hgrn, HEAD_DIM, bs), F32)
        o3, st_h = _mixer_call(
            functools.partial(_hgrn_step_kernel, layer=layer, n_heads=h_hgrn), "hgrn_step", (bs // STEP_ROWS,),
            [z_step(4 * h_hgrn), state_in(state_hgrn), pl.BlockSpec(hgrn_lb_logits.shape, lambda g: (0, 0)), gain1(w_hgrn)],
            [z_h, state_hgrn, hgrn_lb_logits, gn_h],
            [(o3_shape, BF16, o_step(h_hgrn, o_hgrn), o3), _state_out(state_hgrn, layer, st_h)],
            [tposed, tposed, tposed], ("arbitrary",))

        tposed = pltpu.VMEM((h_mlstm, HEAD_DIM, bs), F32)
        rowrep = pltpu.VMEM((h_mlstm, bs, LANES), F32)
        tile = pl.BlockSpec((bs, LANES), lambda g: (0, 0))
        o3, st_c, n_s, m_s = _mixer_call(
            functools.partial(_mlstm_step_kernel, n_heads=h_mlstm), "mlstm_step", (bs // STEP_ROWS,),
            [z_step(4 * h_mlstm), pl.BlockSpec((None, bs, LANES), lambda g: (gate_slab, row_block, 0)),
             pl.BlockSpec((None, 1, LANES), lambda g: (layer, 0, 0)),
             state_in(state_mlstm_c), state_in(state_mlstm_n), tile, gain1(w_mlstm)],
            [z_m, z_r, bias_pad, state_mlstm_c, state_mlstm_n, m_pad[layer], gn_m],
            [(o3_shape, BF16, o_step(h_mlstm, o_mlstm), o3), _state_out(state_mlstm_c, layer, st_c),
             (state_mlstm_n.shape[1:], F32, pl.BlockSpec((STEP_ROWS, h_mlstm, HEAD_DIM), lambda g: (g, 0, 0)), None),
             ((bs, LANES), F32, tile, None)],
            [tposed, tposed, rowrep, rowrep, rowrep], ("arbitrary",))

        tposed = pltpu.VMEM((h_ret, HEAD_DIM, bs), F32)
        table = pl.BlockSpec(cos_s.shape, lambda g: (0, 0))
        o3, st_r = _mixer_call(
            functools.partial(_ret_step_kernel, n_heads=h_ret), "ret_step", (bs // STEP_ROWS,),
            [z_step(4 * h_ret), state_in(state_ret), table, table, smem, gain1(w_ret)],
            [z_r, state_ret, cos_s, sin_s, log_gamma, gn_r],
            [(o3_shape, BF16, o_step(h_ret, o_ret), o3), _state_out(state_ret, layer, st_r)],
            [tposed, tposed], ("arbitrary",))

        new_h_p.append(s_hp)
        new_c_p.append(c_p)
        new_n_p.append(n_p[:, :, 0, :])
        new_m_p.append(m_p[:, :, 0, 0])
        new_r_p.append(s_rp)
        new_n_s.append(n_s)
        new_m_s.append(m_s[:, :h_mlstm])

        x = _matmul(o3, [w_out], layer, mode="resid", tm=tm, tn=1024, n_out=d, resid=x, lhs_order=o_order)

        x = ffn(x, norm_ffn2, ffn2_w_gate, ffn2_w_up, ffn2_w_down, layer)

    y = _rmsnorm(x, norm_final, 0, F32)
    return (
        y[:n_prompt].reshape(bp, t_len, d), y[n_prompt:].reshape(bs, dec_len, d),
        jnp.stack(new_h_p), jnp.stack(new_c_p), jnp.stack(new_n_p), jnp.stack(new_m_p), jnp.stack(new_r_p),
        st_h, st_c, jnp.stack(new_n_s), jnp.stack(new_m_s), st_r,
    )
```

```python
import functools

import jax
import jax.numpy as jnp
from jax import lax
from jax.experimental import pallas as pl
from jax.experimental.pallas import tpu as pltpu

F32 = jnp.float32
BF16 = jnp.bfloat16

HEAD_DIM = 128
LANES = 128
EPS = 1e-6
ROPE_BASE = 10000.0
K_SCALE = HEAD_DIM ** -0.5
PAST_LEN = 16384

BLK = 128
SUB = 16
SEQ_TILE = 512
STEP_ROWS = 16
STEP_UNROLL = 2
NORM_TILE = 416
STACK_TILE = 1024
ROPE_TILE = 256
VMEM_LIMIT_BYTES = 56 * 1024 * 1024

DECAY_SPAN_MAX = 60.0

_TRANS_B = (((1,), (1,)), ((), ()))
_TRANS_A = (((0,), (0,)), ((), ()))


def _params(*semantics):
    return pltpu.CompilerParams(dimension_semantics=semantics, vmem_limit_bytes=VMEM_LIMIT_BYTES)


def _silu(x):
    return x * jax.nn.sigmoid(x)


def _log_sigmoid(x):
    return jnp.minimum(x, 0.0) - jnp.log1p(jnp.exp(-jnp.abs(x)))


def _iota(shape, dim):
    return lax.broadcasted_iota(jnp.int32, shape, dim)


def _seg_cumsum(x, seg):
    pos = _iota(x.shape, 0) & (seg - 1)
    shift = 1
    while shift < seg:
        x = x + jnp.where(pos >= shift, pltpu.roll(x, shift, 0), 0.0)
        shift *= 2
    return x


def _head_norm_gate(o, gain, gate):
    return o * lax.rsqrt(jnp.mean(o * o, axis=-1, keepdims=True) + EPS) * gain * gate


def _pick_lane(x, lane_mask):
    return jnp.sum(jnp.where(lane_mask, x, 0.0), axis=-1, keepdims=True)


def _put_row(tile, j, row):
    return jnp.where(_iota(tile.shape, 0) == j, row, tile)


def _stack_rows_kernel(a_ref, b_ref, o_ref, *, n_a):
    i = pl.program_id(0)

    @pl.when(i < n_a)
    def _():
        o_ref[...] = a_ref[...]

    @pl.when(i == n_a)
    def _():
        o_ref[0:b_ref.shape[0], :] = b_ref[...]


def _stack_rows(a, b):
    (ma, d), mb = a.shape, b.shape[0]
    n_a = ma // STACK_TILE
    return pl.pallas_call(
        functools.partial(_stack_rows_kernel, n_a=n_a),
        out_shape=jax.ShapeDtypeStruct((ma + mb, d), a.dtype),
        grid=(n_a + 1,),
        in_specs=[
            pl.BlockSpec((STACK_TILE, d), lambda i: (jnp.minimum(i, n_a - 1), 0)),
            pl.BlockSpec((mb, d), lambda i: (0, 0)),
        ],
        out_specs=pl.BlockSpec((STACK_TILE, d), lambda i: (i, 0)),
        compiler_params=_params("arbitrary"),
        name="stack_rows",
    )(a, b)


def _rmsnorm_kernel(x_ref, g_ref, o_ref):
    x = x_ref[...]
    y = x * lax.rsqrt(jnp.mean(x * x, axis=-1, keepdims=True) + EPS)
    o_ref[...] = (y * g_ref[...]).astype(o_ref.dtype)


def _rmsnorm(x, gains, layer, out_dtype):
    m, d = x.shape
    gains = gains.reshape(-1, 1, d)
    return pl.pallas_call(
        _rmsnorm_kernel,
        out_shape=jax.ShapeDtypeStruct((m, d), out_dtype),
        grid=(m // NORM_TILE,),
        in_specs=[
            pl.BlockSpec((NORM_TILE, d), lambda i: (i, 0)),
            pl.BlockSpec((None, 1, d), lambda i: (layer, 0, 0)),
        ],
        out_specs=pl.BlockSpec((NORM_TILE, d), lambda i: (i, 0)),
        compiler_params=_params("arbitrary"),
        name="rmsnorm",
    )(x, gains)


def _matmul_kernel(*refs, n_w, mode, lhs_order):
    lhs_ref = refs[0]
    w_refs = refs[1:1 + n_w]
    pos = 1 + n_w
    x_ref = None
    if mode == "resid":
        x_ref = refs[pos]
        pos += 1
    o_ref = refs[pos]

    if lhs_order is None:
        lhs = lhs_ref[...]
    else:
        lhs = jnp.concatenate([lhs_ref[c] for c in lhs_order], axis=1)
    acc = [jnp.dot(lhs, w_ref[...].astype(BF16), preferred_element_type=F32) for w_ref in w_refs]
    if mode == "swiglu":
        gate, up = acc
        o_ref[...] = (0.5 * _silu(gate) * up).astype(o_ref.dtype)
    elif mode == "resid":
        o_ref[...] = x_ref[...] + acc[0]
    else:
        for c in range(o_ref.shape[0]):
            o_ref[c] = acc[0][:, c * LANES:(c + 1) * LANES]


def _matmul(lhs, weights, layer, *, mode, tm, tn, n_out, col_off=0, resid=None, weight_buffers=2, lhs_order=None):
    if lhs_order is None:
        m, k = lhs.shape
        lhs_spec = pl.BlockSpec((tm, k), lambda j, i: (i, 0))
    else:
        m, k = lhs.shape[1], lhs.shape[0] * LANES
        lhs_spec = pl.BlockSpec((lhs.shape[0], tm, LANES), lambda j, i: (0, i, 0))
    n_w = len(weights)
    grid = (pl.cdiv(n_out, tn), m // tm)
    in_specs = [lhs_spec]
    in_specs += [pl.BlockSpec((None, k, tn), lambda j, i: (layer, 0, j + col_off),
                              pipeline_mode=pl.Buffered(weight_buffers)) for _ in weights]
    args = [lhs, *weights]
    if mode == "resid":
        in_specs.append(pl.BlockSpec((tm, tn), lambda j, i: (i, j)))
        args.append(resid)
    if mode == "cols":
        out_shape = jax.ShapeDtypeStruct((n_out // LANES, m, LANES), F32)
        out_spec = pl.BlockSpec((tn // LANES, tm, LANES), lambda j, i: (j, i, 0))
    else:
        out_shape = jax.ShapeDtypeStruct((m, n_out), BF16 if mode == "swiglu" else F32)
        out_spec = pl.BlockSpec((tm, tn), lambda j, i: (i, j))
    return pl.pallas_call(
        functools.partial(_matmul_kernel, n_w=n_w, mode=mode, lhs_order=lhs_order),
        out_shape=out_shape,
        grid=grid,
        in_specs=in_specs,
        out_specs=out_spec,
        compiler_params=_params("arbitrary", "arbitrary"),
        name="matmul_" + mode,
    )(*args)


def _rope_kernel(inv_ref, cos_ref, sin_ref, *, pos0, step):
    shape = cos_ref.shape
    row = _iota(shape, 0) + pl.program_id(0) * shape[0]
    ang = (pos0 + step * row.astype(F32)) * inv_ref[...]
    sin = jnp.sin(ang)
    cos_ref[...] = jnp.cos(ang)
    sin_ref[...] = jnp.where(_iota(shape, 1) < HEAD_DIM // 2, -sin, sin)


def _rope_tables(rows, tile, pos0, step):
    inv = ROPE_BASE ** (-jnp.arange(0, HEAD_DIM, 2, dtype=F32) / HEAD_DIM)
    inv = jnp.concatenate([inv, inv]).reshape(1, HEAD_DIM)
    out = jax.ShapeDtypeStruct((rows, HEAD_DIM), F32)
    spec = pl.BlockSpec((tile, HEAD_DIM), lambda i: (i, 0))
    return pl.pallas_call(
        functools.partial(_rope_kernel, pos0=float(pos0), step=float(step)),
        out_shape=(out, out),
        grid=(rows // tile,),
        in_specs=[pl.BlockSpec((1, HEAD_DIM), lambda i: (0, 0))],
        out_specs=(spec, spec),
        compiler_params=_params("arbitrary"),
        name="rope_tables",
    )(inv)


def _rope(x, cos, sin):
    return x * cos + pltpu.roll(x, HEAD_DIM // 2, 1) * sin


def _mixer_call(kernel_fn, name, grid, in_specs, args, outs, scratch, semantics):
    in_specs, args, aliases = list(in_specs), list(args), {}
    for k, (_, _, _, prev) in enumerate(outs):
        if prev is not None:
            aliases[len(args)] = k
            in_specs.append(pl.BlockSpec(memory_space=pl.ANY))
            args.append(prev)
    n_alias = len(aliases)

    def body(*refs):
        n_in = len(args) - n_alias
        kernel_fn(*refs[:n_in], *refs[n_in + n_alias:])

    return pl.pallas_call(
        body,
        out_shape=tuple(jax.ShapeDtypeStruct(s, dt) for s, dt, _, _ in outs),
        grid=grid,
        in_specs=in_specs,
        out_specs=tuple(spec for _, _, spec, _ in outs),
        scratch_shapes=scratch,
        input_output_aliases=aliases,
        compiler_params=_params(*semantics),
        name=name,
    )(*args)


def _state_out(state, layer, prev):
    tail = state.shape[2:]
    zeros = (0,) * len(tail)
    spec = pl.BlockSpec((None, STEP_ROWS) + tail, lambda g: (layer, g) + zeros)
    return (state.shape, F32, spec, prev)


def _hgrn_lower_bound(logits, layer):
    mx = jnp.max(logits, axis=0, keepdims=True)
    e = jnp.exp(logits - mx)
    p = e / jnp.sum(e, axis=0, keepdims=True)
    if layer == 0:
        return jnp.zeros_like(mx)
    return jnp.sum(p[1:layer + 1], axis=0, keepdims=True)


def _hgrn_gates(fz, lb):
    log_lb = jnp.log(lb)
    y = jnp.log1p(-lb) + _log_sigmoid(fz)
    log_f = jnp.maximum(log_lb, y) + jnp.log1p(jnp.exp(-jnp.abs(log_lb - y)))
    key = (1.0 - lb) / (1.0 + jnp.exp(fz))
    return log_f, key


def _hgrn_prompt_kernel(z_ref, lbl_ref, gn_ref, o_ref, s_ref, st_ref, q_scr, k_scr, b_scr, a_scr, *, layer, n_heads):
    t = pl.program_id(1)
    n_sub = BLK // SUB
    lb = _hgrn_lower_bound(lbl_ref[...], layer)
    row = _iota((BLK, LANES), 0)
    col = _iota((BLK, LANES), 1)
    pos = row & (SUB - 1)
    seg0 = row - pos

    @pl.when(t == 0)
    def _():
        st_ref[...] = jnp.zeros_like(st_ref)

    def seg_row(x, s):
        x3 = x.reshape(n_sub, SUB, LANES)
        return jnp.broadcast_to(x3[:, s:s + 1, :], x3.shape).reshape(BLK, LANES)

    def block(i, carry):
        rows = pl.ds(pl.multiple_of(i * BLK, BLK), BLK)
        span = jnp.zeros((BLK, LANES), F32)
        for h in range(n_heads):
            log_f, k = _hgrn_gates(z_ref[n_heads + h, rows, :], lb[:, h * LANES:(h + 1) * LANES])
            b = _seg_cumsum(log_f, SUB)
            q_scr[h] = _silu(z_ref[h, rows, :])
            k_scr[h] = k
            b_scr[h] = b
            span = jnp.maximum(span, -seg_row(b, SUB - 1))
        mild = jnp.max(span) <= DECAY_SPAN_MAX

        @pl.when(mild)
        def _():
            for h in range(n_heads):
                b = b_scr[h]
                b_end = seg_row(b, SUB - 1)
                q_end = (q_scr[h] * jnp.exp(b - b_end)).astype(BF16)
                k_end = (k_scr[h] * jnp.exp(b_end - b)).astype(BF16)
                a = lax.dot_general(q_end, k_end, _TRANS_B, preferred_element_type=F32)
                a_scr[h] = jnp.where((col >= seg0) & (col <= row), a, 0.0)

        @pl.when(jnp.logical_not(mild))
        def _():
            for h in range(n_heads):
                q, k, b = q_scr[h], k_scr[h], b_scr[h]
                a = jnp.zeros((BLK, LANES), F32)
                for s in range(SUB):
                    e = jnp.exp(jnp.where(pos >= s, b - seg_row(b, s), -jnp.inf))
                    c = jnp.sum(q * seg_row(k, s) * e, axis=-1, keepdims=True)
                    a = jnp.where(col == seg0 + s, c, a)
                a_scr[h] = a

        for h in range(n_heads):
            cols = slice(h * LANES, (h + 1) * LANES)
            b = b_scr[h]
            b_end = seg_row(b, SUB - 1)
            vb = z_ref[2 * n_heads + h, rows, :].astype(BF16)
            q_in = (q_scr[h] * jnp.exp(b)).astype(BF16)
            k_out = (k_scr[h] * jnp.exp(b_end - b)).astype(BF16)
            lam = jnp.exp(b_end)
            o = jnp.dot(a_scr[h].astype(BF16), vb, preferred_element_type=F32)
            subs = [slice(n * SUB, (n + 1) * SUB) for n in range(n_sub)]
            ups = [lax.dot_general(vb[sl], k_out[sl], _TRANS_A, preferred_element_type=F32) for sl in subs]
            states = [st_ref[h]]
            for n in range(n_sub):
                states.append(states[n] * lam[n * SUB:n * SUB + 1, :] + ups[n])
            st_ref[h] = states[n_sub]
            parts = [lax.dot_general(q_in[sl], states[n].astype(BF16), _TRANS_B, preferred_element_type=F32)
                     for n, sl in enumerate(subs)]
            o = o + jnp.concatenate(parts, axis=0)
            gate = _silu(z_ref[3 * n_heads + h, rows, :])
            o_ref[h, rows, :] = _head_norm_gate(o, gn_ref[:, cols], gate).astype(o_ref.dtype)
        return carry

    lax.fori_loop(0, z_ref.shape[1] // BLK, block, 0)

    @pl.when(t == pl.num_programs(1) - 1)
    def _():
        for h in range(n_heads):
            s_ref[h] = st_ref[h].T


def _hgrn_step_kernel(z_ref, s_ref, lbl_ref, gn_ref, o_ref, so_ref, f_scr, k_scr, q_scr, *, layer, n_heads):
    g = pl.program_id(0)
    n_rows = z_ref.shape[1]

    @pl.when(g == 0)
    def _():
        lb = _hgrn_lower_bound(lbl_ref[...], layer)
        for h in range(n_heads):
            log_f, k = _hgrn_gates(z_ref[n_heads + h], lb[:, h * LANES:(h + 1) * LANES])
            f_scr[h] = jnp.exp(log_f).T
            k_scr[h] = k.T
            q_scr[h] = _silu(z_ref[h]).T

    lane = _iota((HEAD_DIM, n_rows), 1)

    def body(j, outs):
        b = g * STEP_ROWS + j
        sel = lane == b
        one = pl.ds(b, 1)
        new = []
        for h in range(n_heads):
            s_new = _pick_lane(f_scr[h], sel) * s_ref[j, h] + _pick_lane(k_scr[h], sel) * z_ref[2 * n_heads + h, one, :]
            so_ref[j, h] = s_new
            o = jnp.sum(_pick_lane(q_scr[h], sel) * s_new, axis=0, keepdims=True)
            new.append(_put_row(outs[h], j, o))
        return tuple(new)

    zero = jnp.zeros((STEP_ROWS, LANES), F32)
    outs = lax.fori_loop(0, STEP_ROWS, body, (zero,) * n_heads, unroll=STEP_UNROLL)
    grp = pl.ds(pl.multiple_of(g * STEP_ROWS, STEP_ROWS), STEP_ROWS)
    for h in range(n_heads):
        gate = _silu(z_ref[3 * n_heads + h, grp, :])
        o_ref[h] = _head_norm_gate(outs[h], gn_ref[:, h * LANES:(h + 1) * LANES], gate).astype(o_ref.dtype)


def _ret_prompt_kernel(z_ref, cos_ref, sin_ref, lg_ref, gn_ref, o_ref, s_ref, st_ref, dec_ref, *, n_heads):
    t = pl.program_id(1)

    @pl.when((pl.program_id(0) == 0) & (t == 0))
    def _():
        row = _iota((BLK, LANES), 0).astype(F32)
        diff = row - _iota((BLK, LANES), 1).astype(F32)
        for h in range(n_heads):
            lg = lg_ref[h]
            dec_ref[h, 0] = jnp.where(diff >= 0.0, jnp.exp(diff * lg), 0.0)
            dec_ref[h, 1] = jnp.exp((row + 1.0) * lg)
            dec_ref[h, 2] = jnp.exp((BLK - 1.0 - row) * lg)

    @pl.when(t == 0)
    def _():
        st_ref[...] = jnp.zeros_like(st_ref)

    def chunk(i, carry):
        rows = pl.ds(pl.multiple_of(i * BLK, BLK), BLK)
        tab = pl.ds(pl.multiple_of(t * z_ref.shape[1] + i * BLK, BLK), BLK)
        cos = cos_ref[tab, :]
        sin = sin_ref[tab, :]
        for h in range(n_heads):
            cols = slice(h * LANES, (h + 1) * LANES)
            q = _rope(z_ref[h, rows, :], cos, sin)
            k = _rope(z_ref[n_heads + h, rows, :], cos, sin) * K_SCALE
            vb = z_ref[2 * n_heads + h, rows, :].astype(BF16)
            a = lax.dot_general(q.astype(BF16), k.astype(BF16), _TRANS_B, preferred_element_type=F32) * dec_ref[h, 0]
            st = st_ref[h]
            o = jnp.dot(a.astype(BF16), vb, preferred_element_type=F32)
            o = o + jnp.dot((q * dec_ref[h, 1]).astype(BF16), st.astype(BF16), preferred_element_type=F32)
            s_dec = jnp.exp(jnp.full((1, LANES), BLK, F32) * lg_ref[h])
            st_ref[h] = s_dec * st + jnp.dot((k * dec_ref[h, 2]).T.astype(BF16), vb, preferred_element_type=F32)
            gate = _silu(z_ref[3 * n_heads + h, rows, :])
            o_ref[h, rows, :] = _head_norm_gate(o, gn_ref[:, cols], gate).astype(o_ref.dtype)
        return carry

    lax.fori_loop(0, z_ref.shape[1] // BLK, chunk, 0)

    @pl.when(t == pl.num_programs(1) - 1)
    def _():
        s_ref[...] = st_ref[...]


def _ret_step_kernel(z_ref, s_ref, cos_ref, sin_ref, lg_ref, gn_ref, o_ref, so_ref, k_scr, q_scr, *, n_heads):
    g = pl.program_id(0)
    n_rows = z_ref.shape[1]

    @pl.when(g == 0)
    def _():
        cos = cos_ref[0:1, :]
        sin = sin_ref[0:1, :]
        for h in range(n_heads):
            q_scr[h] = _rope(z_ref[h], cos, sin).T
            k_scr[h] = (_rope(z_ref[n_heads + h], cos, sin) * K_SCALE).T

    lane = _iota((HEAD_DIM, n_rows), 1)

    def body(j, outs):
        b = g * STEP_ROWS + j
        sel = lane == b
        one = pl.ds(b, 1)
        new = []
        for h in range(n_heads):
            gamma = jnp.exp(jnp.full((1, LANES), lg_ref[h], F32))
            s_new = gamma * s_ref[j, h] + _pick_lane(k_scr[h], sel) * z_ref[2 * n_heads + h, one, :]
            so_ref[j, h] = s_new
            o = jnp.sum(_pick_lane(q_scr[h], sel) * s_new, axis=0, keepdims=True)
            new.append(_put_row(outs[h], j, o))
        return tuple(new)

    zero = jnp.zeros((STEP_ROWS, LANES), F32)
    outs = lax.fori_loop(0, STEP_ROWS, body, (zero,) * n_heads, unroll=STEP_UNROLL)
    grp = pl.ds(pl.multiple_of(g * STEP_ROWS, STEP_ROWS), STEP_ROWS)
    for h in range(n_heads):
        gate = _silu(z_ref[3 * n_heads + h, grp, :])
        o_ref[h] = _head_norm_gate(outs[h], gn_ref[:, h * LANES:(h + 1) * LANES], gate).astype(o_ref.dtype)


def _mlstm_prompt_kernel(z_ref, zg_ref, bias_ref, gn_ref, o_ref, c_ref, n_ref, m_ref,
                         c_scr, n_scr, m_scr, *, n_heads):
    t = pl.program_id(1)
    lane = _iota((BLK, LANES), 1)
    causal = lane <= _iota((BLK, LANES), 0)

    @pl.when(t == 0)
    def _():
        c_scr[...] = jnp.zeros_like(c_scr)
        n_scr[...] = jnp.zeros_like(n_scr)
        m_scr[...] = jnp.zeros_like(m_scr)

    def chunk(i, carry):
        rows = pl.ds(pl.multiple_of(i * BLK, BLK), BLK)
        gates = zg_ref[rows, :] + bias_ref[...]
        log_f_all = pltpu.roll(_log_sigmoid(gates), LANES - n_heads, 1)
        bt_all = _seg_cumsum(log_f_all, BLK)
        w_t = (gates - bt_all).T
        for h in range(n_heads):
            cols = slice(h * LANES, (h + 1) * LANES)
            sel = lane == h
            log_i = _pick_lane(gates, sel)
            bt = _pick_lane(bt_all, sel)
            log_d = jnp.where(causal, bt + w_t[h:h + 1, :], -jnp.inf)
            m_prev = m_scr[h]
            log_inter = bt + m_prev
            m_t = jnp.maximum(log_inter, jnp.max(log_d, axis=-1, keepdims=True))
            d_mat = jnp.exp(log_d - m_t)
            w_inter = jnp.exp(log_inter - m_t)
            q = z_ref[h, rows, :]
            k = z_ref[n_heads + h, rows, :] * K_SCALE
            qb = q.astype(BF16)
            vb = z_ref[2 * n_heads + h, rows, :].astype(BF16)
            sc = lax.dot_general(qb, k.astype(BF16), _TRANS_B, preferred_element_type=F32) * d_mat
            cm = c_scr[h]
            nm = n_scr[h]
            num = jnp.dot(sc.astype(BF16), vb, preferred_element_type=F32)
            num = num + jnp.dot(qb, cm.astype(BF16), preferred_element_type=F32) * w_inter
            den = jnp.sum(sc, axis=-1, keepdims=True) + w_inter * jnp.sum(q * nm, axis=-1, keepdims=True)
            hid = num / jnp.maximum(jnp.abs(den), jnp.exp(-m_t))
            m_end = m_t[BLK - 1:BLK, :]
            b_end = bt[BLK - 1:BLK, :]
            kw = k * jnp.exp(b_end - bt + log_i - m_end)
            w_c = jnp.exp(b_end + m_prev - m_end)
            c_scr[h] = w_c * cm + jnp.dot(kw.T.astype(BF16), vb, preferred_element_type=F32)
            n_scr[h] = w_c * nm + jnp.sum(kw, axis=0, keepdims=True)
            m_scr[h] = m_end
            gate = jax.nn.sigmoid(z_ref[3 * n_heads + h, rows, :])
            o_ref[h, rows, :] = _head_norm_gate(hid, gn_ref[:, cols], gate).astype(o_ref.dtype)
        return carry

    lax.fori_loop(0, z_ref.shape[1] // BLK, chunk, 0)

    @pl.when(t == pl.num_programs(1) - 1)
    def _():
        c_ref[...] = c_scr[...]
        n_ref[...] = n_scr[...]
        m_ref[...] = m_scr[...]


def _mlstm_step_kernel(z_ref, zg_ref, bias_ref, c_ref, n_ref, m_ref, gn_ref,
                       o_ref, co_ref, no_ref, mo_ref, k_scr, q_scr, wc_scr, wk_scr, em_scr, *, n_heads):
    g = pl.program_id(0)
    n_rows = z_ref.shape[1]

    @pl.when(g == 0)
    def _():
        lanes = _iota((n_rows, LANES), 1)
        gates = zg_ref[...] + bias_ref[...]
        log_f = pltpu.roll(_log_sigmoid(gates), LANES - n_heads, 1)
        m_old = m_ref[...]
        m_new = jnp.maximum(log_f + m_old, gates)
        mo_ref[...] = m_new
        w_c = jnp.exp(log_f + m_old - m_new)
        w_k = jnp.exp(gates - m_new)
        e_m = jnp.exp(-m_new)
        for h in range(n_heads):
            sel = lanes == h
            wc_scr[h] = jnp.broadcast_to(_pick_lane(w_c, sel), (n_rows, LANES))
            wk_scr[h] = jnp.broadcast_to(_pick_lane(w_k, sel), (n_rows, LANES))
            em_scr[h] = jnp.broadcast_to(_pick_lane(e_m, sel), (n_rows, LANES))
            q_scr[h] = z_ref[h].T
            k_scr[h] = (z_ref[n_heads + h] * K_SCALE).T

    lane = _iota((HEAD_DIM, n_rows), 1)

    def body(j, outs):
        b = g * STEP_ROWS + j
        sel = lane == b
        one = pl.ds(b, 1)
        new = []
        for h in range(n_heads):
            w_c = wc_scr[h, one, :]
            w_k = wk_scr[h, one, :]
            c_new = w_c * c_ref[j, h] + (w_k * _pick_lane(k_scr[h], sel)) * z_ref[2 * n_heads + h, one, :]
            co_ref[j, h] = c_new
            n_new = w_c * n_ref[j, h:h + 1, :] + w_k * (z_ref[n_heads + h, one, :] * K_SCALE)
            no_ref[j, h:h + 1, :] = n_new
            num = jnp.sum(_pick_lane(q_scr[h], sel) * c_new, axis=0, keepdims=True)
            new.append(_put_row(outs[2 * h], j, num))
            new.append(_put_row(outs[2 * h + 1], j, n_new))
        return tuple(new)

    zero = jnp.zeros((STEP_ROWS, LANES), F32)
    outs = lax.fori_loop(0, STEP_ROWS, body, (zero,) * (2 * n_heads), unroll=STEP_UNROLL)
    grp = pl.ds(pl.multiple_of(g * STEP_ROWS, STEP_ROWS), STEP_ROWS)
    for h in range(n_heads):
        num, n_new = outs[2 * h], outs[2 * h + 1]
        den = jnp.sum(z_ref[h, grp, :] * n_new, axis=-1, keepdims=True)
        hid = num / jnp.maximum(jnp.abs(den), em_scr[h, grp, :])
        gate = jax.nn.sigmoid(z_ref[3 * n_heads + h, grp, :])
        o_ref[h] = _head_norm_gate(hid, gn_ref[:, h * LANES:(h + 1) * LANES], gate).astype(o_ref.dtype)


def kernel(x_prompt, x_sample, state_hgrn, state_mlstm_c, state_mlstm_n, state_mlstm_m, state_ret, norm_ffn1, ffn1_w_gate, ffn1_w_up, ffn1_w_down, norm_mix, w_in, hgrn_lb_logits, mlstm_gate_bias, gn_hgrn, gn_mlstm, gn_ret, w_out, norm_ffn2, ffn2_w_gate, ffn2_w_up, ffn2_w_down, norm_final):
    bp, t_len, d = x_prompt.shape
    bs, dec_len, _ = x_sample.shape
    depth = w_in.shape[0]
    h_hgrn, h_mlstm, h_ret = state_hgrn.shape[2], state_mlstm_c.shape[2], state_ret.shape[2]
    w_hgrn, w_mlstm, w_ret = h_hgrn * HEAD_DIM, h_mlstm * HEAD_DIM, h_ret * HEAD_DIM
    d_ff = ffn1_w_gate.shape[2]
    n_prompt = bp * t_len
    m_rows = n_prompt + bs
    seq_tile = min(SEQ_TILE, t_len)
    n_seq = t_len // seq_tile
    assert dec_len == 1 and bs == LANES and n_prompt % bs == 0 and t_len % seq_tile == 0 and seq_tile % BLK == 0
    row_block = n_prompt // bs
    step_block = n_prompt // STEP_ROWS
    tm = m_rows // 8 if m_rows % 128 == 0 else m_rows
    assert m_rows % tm == 0 and tm % 16 == 0 and m_rows % NORM_TILE == 0 and n_prompt % STACK_TILE == 0
    o_mlstm, o_ret, o_hgrn = 0, h_mlstm, h_mlstm + h_ret
    assert o_ret % h_ret == 0 and o_hgrn % h_hgrn == 0
    n_slabs = h_hgrn + h_mlstm + h_ret
    o_order = tuple(range(o_hgrn, n_slabs)) + tuple(range(o_mlstm, o_ret)) + tuple(range(o_ret, o_hgrn))

    gate0 = 4 * w_hgrn + 4 * w_mlstm
    ret0 = gate0 + 2 * h_mlstm
    w_in_ret = jnp.concatenate(
        [w_in[:, :, ret0:], jnp.pad(w_in[:, :, gate0:ret0], ((0, 0), (0, 0), (0, LANES - 2 * h_mlstm)))],
        axis=2).astype(BF16)
    w_in_hm = w_in[:, :, :gate0].astype(BF16)
    gate_slab = 4 * h_ret
    bias_pad = jnp.pad(mlstm_gate_bias, ((0, 0), (0, LANES - 2 * h_mlstm))).reshape(depth, 1, LANES)
    m_pad = jnp.pad(state_mlstm_m, ((0, 0), (0, 0), (0, LANES - h_mlstm)))
    gn_h = gn_hgrn.reshape(depth, 1, w_hgrn)
    gn_m = gn_mlstm.reshape(depth, 1, w_mlstm)
    gn_r = gn_ret.reshape(depth, 1, w_ret)
    log_gamma = jnp.log1p(-jnp.exp2(-5.0 - jnp.arange(h_ret, dtype=F32)))
    cos_p, sin_p = _rope_tables(t_len, min(ROPE_TILE, t_len), 0.0, 1.0)
    cos_s, sin_s = _rope_tables(8, 8, PAST_LEN, 0.0)

    smem = pl.BlockSpec(memory_space=pltpu.SMEM)
    o3_shape = (n_slabs, m_rows, LANES)
    head_sq = (HEAD_DIM, HEAD_DIM)

    def z_prompt(n_blocks):
        return pl.BlockSpec((n_blocks, seq_tile, LANES), lambda b, t: (0, b * n_seq + t, 0))

    def o_prompt(heads, slab0):
        return pl.BlockSpec((heads, seq_tile, LANES), lambda b, t: (slab0 // heads, b * n_seq + t, 0))

    def o_step(heads, slab0):
        return pl.BlockSpec((heads, STEP_ROWS, LANES), lambda g: (slab0 // heads, step_block + g, 0))

    x = _stack_rows(x_prompt.reshape(n_prompt, d), x_sample.reshape(bs, d))
    new_h_p, new_c_p, new_n_p, new_m_p, new_r_p = [], [], [], [], []
    new_n_s, new_m_s = [], []
    st_h = st_c = st_r = None

    def ffn(x, norm, w_gate, w_up, w_down, layer):
        n = _rmsnorm(x, norm, layer, BF16)
        mid = _matmul(n, [w_gate, w_up], layer, mode="swiglu", tm=2 * tm, tn=512, n_out=d_ff)
        return _matmul(mid, [w_down], layer, mode="resid", tm=tm, tn=512, n_out=d, resid=x, weight_buffers=1)

    for layer in range(depth):
        x = ffn(x, norm_ffn1, ffn1_w_gate, ffn1_w_up, ffn1_w_down, layer)

        n = _rmsnorm(x, norm_mix, layer, BF16)
        z_h = _matmul(n, [w_in_hm], layer, mode="cols", tm=tm, tn=1024, n_out=4 * w_hgrn)
        z_m = _matmul(n, [w_in_hm], layer, mode="cols", tm=tm, tn=1024, n_out=4 * w_mlstm, col_off=4 * w_hgrn // 1024)
        z_r = _matmul(n, [w_in_ret], layer, mode="cols", tm=tm, tn=640, n_out=4 * w_ret + LANES)

        def gain2(width):
            return pl.BlockSpec((None, 1, width), lambda b, t: (layer, 0, 0))

        def gain1(width):
            return pl.BlockSpec((None, 1, width), lambda g: (layer, 0, 0))

        o3, s_hp = _mixer_call(
            functools.partial(_hgrn_prompt_kernel, layer=layer, n_heads=h_hgrn), "hgrn_prompt", (bp, n_seq),
            [z_prompt(4 * h_hgrn), pl.BlockSpec(hgrn_lb_logits.shape, lambda b, t: (0, 0)), gain2(w_hgrn)],
            [z_h, hgrn_lb_logits, gn_h],
            [(o3_shape, BF16, o_prompt(h_hgrn, o_hgrn), None),
             ((bp, h_hgrn) + head_sq, F32, pl.BlockSpec((None, h_hgrn) + head_sq, lambda b, t: (b, 0, 0, 0)), None)],
            [pltpu.VMEM((h_hgrn,) + head_sq, F32)] + [pltpu.VMEM((h_hgrn, BLK, LANES), F32) for _ in range(4)],
            ("arbitrary", "arbitrary"))

        vec = ((bp, h_mlstm, 1, LANES), F32, pl.BlockSpec((None, h_mlstm, 1, LANES), lambda b, t: (b, 0, 0, 0)), None)
        o3, c_p, n_p, m_p = _mixer_call(
            functools.partial(_mlstm_prompt_kernel, n_heads=h_mlstm), "mlstm_prompt", (bp, n_seq),
            [z_prompt(4 * h_mlstm),
             pl.BlockSpec((None, seq_tile, LANES), lambda b, t: (gate_slab, b * n_seq + t, 0)),
             pl.BlockSpec((None, 1, LANES), lambda b, t: (layer, 0, 0)), gain2(w_mlstm)],
            [z_m, z_r, bias_pad, gn_m],
            [(o3_shape, BF16, o_prompt(h_mlstm, o_mlstm), o3),
             ((bp, h_mlstm) + head_sq, F32, pl.BlockSpec((None, h_mlstm) + head_sq, lambda b, t: (b, 0, 0, 0)), None),
             vec, vec],
            [pltpu.VMEM((h_mlstm,) + head_sq, F32), pltpu.VMEM((h_mlstm, 1, LANES), F32),
             pltpu.VMEM((h_mlstm, 1, LANES), F32)], ("arbitrary", "arbitrary"))

        table = pl.BlockSpec((t_len, LANES), lambda b, t: (0, 0))
        o3, s_rp = _mixer_call(
            functools.partial(_ret_prompt_kernel, n_heads=h_ret), "ret_prompt", (bp, n_seq),
            [z_prompt(4 * h_ret), table, table, smem, gain2(w_ret)],
            [z_r, cos_p, sin_p, log_gamma, gn_r],
            [(o3_shape, BF16, o_prompt(h_ret, o_ret), o3),
             ((bp, h_ret) + head_sq, F32, pl.BlockSpec((None, h_ret) + head_sq, lambda b, t: (b, 0, 0, 0)), None)],
            [pltpu.VMEM((h_ret,) + head_sq, F32), pltpu.VMEM((h_ret, 3, BLK, LANES), F32)],
            ("arbitrary", "arbitrary"))

        def z_step(n_blocks):
            return pl.BlockSpec((n_blocks, bs, LANES), lambda g: (0, row_block, 0))

        def state_in(state):
            tail = state.shape[2:]
            zeros = (0,) * len(tail)
            return pl.BlockSpec((None, STEP_ROWS) + tail, lambda g: (layer, g) + zeros)

        tposed = pltpu.VMEM((h_hgrn, HEAD_DIM, bs), F32)
        o3, st_h = _mixer_call(
            functools.partial(_hgrn_step_kernel, layer=layer, n_heads=h_hgrn), "hgrn_step", (bs // STEP_ROWS,),
            [z_step(4 * h_hgrn), state_in(state_hgrn), pl.BlockSpec(hgrn_lb_logits.shape, lambda g: (0, 0)), gain1(w_hgrn)],
            [z_h, state_hgrn, hgrn_lb_logits, gn_h],
            [(o3_shape, BF16, o_step(h_hgrn, o_hgrn), o3), _state_out(state_hgrn, layer, st_h)],
            [tposed, tposed, tposed], ("arbitrary",))

        tposed = pltpu.VMEM((h_mlstm, HEAD_DIM, bs), F32)
        rowrep = pltpu.VMEM((h_mlstm, bs, LANES), F32)
        tile = pl.BlockSpec((bs, LANES), lambda g: (0, 0))
        o3, st_c, n_s, m_s = _mixer_call(
            functools.partial(_mlstm_step_kernel, n_heads=h_mlstm), "mlstm_step", (bs // STEP_ROWS,),
            [z_step(4 * h_mlstm), pl.BlockSpec((None, bs, LANES), lambda g: (gate_slab, row_block, 0)),
             pl.BlockSpec((None, 1, LANES), lambda g: (layer, 0, 0)),
             state_in(state_mlstm_c), state_in(state_mlstm_n), tile, gain1(w_mlstm)],
            [z_m, z_r, bias_pad, state_mlstm_c, state_mlstm_n, m_pad[layer], gn_m],
            [(o3_shape, BF16, o_step(h_mlstm, o_mlstm), o3), _state_out(state_mlstm_c, layer, st_c),
             (state_mlstm_n.shape[1:], F32, pl.BlockSpec((STEP_ROWS, h_mlstm, HEAD_DIM), lambda g: (g, 0, 0)), None),
             ((bs, LANES), F32, tile, None)],
            [tposed, tposed, rowrep, rowrep, rowrep], ("arbitrary",))

        tposed = pltpu.VMEM((h_ret, HEAD_DIM, bs), F32)
        table = pl.BlockSpec(cos_s.shape, lambda g: (0, 0))
        o3, st_r = _mixer_call(
            functools.partial(_ret_step_kernel, n_heads=h_ret), "ret_step", (bs // STEP_ROWS,),
            [z_step(4 * h_ret), state_in(state_ret), table, table, smem, gain1(w_ret)],
            [z_r, state_ret, cos_s, sin_s, log_gamma, gn_r],
            [(o3_shape, BF16, o_step(h_ret, o_ret), o3), _state_out(state_ret, layer, st_r)],
            [tposed, tposed], ("arbitrary",))

        new_h_p.append(s_hp)
        new_c_p.append(c_p)
        new_n_p.append(n_p[:, :, 0, :])
        new_m_p.append(m_p[:, :, 0, 0])
        new_r_p.append(s_rp)
        new_n_s.append(n_s)
        new_m_s.append(m_s[:, :h_mlstm])

        x = _matmul(o3, [w_out], layer, mode="resid", tm=tm, tn=1024, n_out=d, resid=x, lhs_order=o_order)

        x = ffn(x, norm_ffn2, ffn2_w_gate, ffn2_w_up, ffn2_w_down, layer)

    y = _rmsnorm(x, norm_final, 0, F32)
    return (
        y[:n_prompt].reshape(bp, t_len, d), y[n_prompt:].reshape(bs, dec_len, d),
        jnp.stack(new_h_p), jnp.stack(new_c_p), jnp.stack(new_n_p), jnp.stack(new_m_p), jnp.stack(new_r_p),
        st_h, st_c, jnp.stack(new_n_s), jnp.stack(new_m_s), st_r,
    )
```

```python
import functools

import jax
import jax.numpy as jnp
from jax import lax
from jax.experimental import pallas as pl
from jax.experimental.pallas import tpu as pltpu

F32 = jnp.float32
BF16 = jnp.bfloat16

HEAD_DIM = 128
LANES = 128
EPS = 1e-6
ROPE_BASE = 10000.0
K_SCALE = HEAD_DIM ** -0.5
PAST_LEN = 16384

BLK = 128
SUB = 16
SEQ_TILE = 512
STEP_ROWS = 16
STEP_UNROLL = 2
NORM_TILE = 416
STACK_TILE = 1024
ROPE_TILE = 256
VMEM_LIMIT_BYTES = 56 * 1024 * 1024

DECAY_SPAN_MAX = 60.0

_TRANS_B = (((1,), (1,)), ((), ()))
_TRANS_A = (((0,), (0,)), ((), ()))


def _params(*semantics):
    return pltpu.CompilerParams(dimension_semantics=semantics, vmem_limit_bytes=VMEM_LIMIT_BYTES)


def _silu(x):
    return x * jax.nn.sigmoid(x)


def _log_sigmoid(x):
    return jnp.minimum(x, 0.0) - jnp.log1p(jnp.exp(-jnp.abs(x)))


def _iota(shape, dim):
    return lax.broadcasted_iota(jnp.int32, shape, dim)


def _seg_cumsum(x, seg):
    pos = _iota(x.shape, 0) & (seg - 1)
    shift = 1
    while shift < seg:
        x = x + jnp.where(pos >= shift, pltpu.roll(x, shift, 0), 0.0)
        shift *= 2
    return x


def _head_norm_gate(o, gain, gate):
    return o * lax.rsqrt(jnp.mean(o * o, axis=-1, keepdims=True) + EPS) * gain * gate


def _pick_lane(x, lane_mask):
    return jnp.sum(jnp.where(lane_mask, x, 0.0), axis=-1, keepdims=True)


def _put_row(tile, j, row):
    return jnp.where(_iota(tile.shape, 0) == j, row, tile)


def _stack_rows_kernel(a_ref, b_ref, g_ref, o_ref, n_ref, *, n_a):
    i = pl.program_id(0)

    def emit(x, rows):
        o_ref[rows, :] = x
        y = x * lax.rsqrt(jnp.mean(x * x, axis=-1, keepdims=True) + EPS)
        n_ref[rows, :] = (y * g_ref[...]).astype(n_ref.dtype)

    @pl.when(i < n_a)
    def _():
        emit(a_ref[...], slice(None))

    @pl.when(i == n_a)
    def _():
        emit(b_ref[...], slice(0, b_ref.shape[0]))


def _stack_rows(a, b, gain):
    (ma, d), mb = a.shape, b.shape[0]
    n_a = ma // STACK_TILE
    tile = pl.BlockSpec((STACK_TILE, d), lambda i: (i, 0))
    return pl.pallas_call(
        functools.partial(_stack_rows_kernel, n_a=n_a),
        out_shape=(jax.ShapeDtypeStruct((ma + mb, d), a.dtype), jax.ShapeDtypeStruct((ma + mb, d), BF16)),
        grid=(n_a + 1,),
        in_specs=[
            pl.BlockSpec((STACK_TILE, d), lambda i: (jnp.minimum(i, n_a - 1), 0)),
            pl.BlockSpec((mb, d), lambda i: (0, 0)),
            pl.BlockSpec((1, d), lambda i: (0, 0)),
        ],
        out_specs=(tile, tile),
        compiler_params=_params("arbitrary"),
        name="stack_rows",
    )(a, b, gain.reshape(1, d))


def _rmsnorm_kernel(x_ref, g_ref, o_ref):
    x = x_ref[...]
    y = x * lax.rsqrt(jnp.mean(x * x, axis=-1, keepdims=True) + EPS)
    o_ref[...] = (y * g_ref[...]).astype(o_ref.dtype)


def _rmsnorm(x, gains, layer, out_dtype):
    m, d = x.shape
    gains = gains.reshape(-1, 1, d)
    return pl.pallas_call(
        _rmsnorm_kernel,
        out_shape=jax.ShapeDtypeStruct((m, d), out_dtype),
        grid=(m // NORM_TILE,),
        in_specs=[
            pl.BlockSpec((NORM_TILE, d), lambda i: (i, 0)),
            pl.BlockSpec((None, 1, d), lambda i: (layer, 0, 0)),
        ],
        out_specs=pl.BlockSpec((NORM_TILE, d), lambda i: (i, 0)),
        compiler_params=_params("arbitrary"),
        name="rmsnorm",
    )(x, gains)


def _rmsnorm_split_kernel(x_ref, g_ref, a_ref, b_ref, *, n_a):
    i = pl.program_id(0)
    x = x_ref[...]
    y = x * lax.rsqrt(jnp.mean(x * x, axis=-1, keepdims=True) + EPS) * g_ref[...]

    @pl.when(i < n_a)
    def _():
        a_ref[...] = y

    @pl.when(i == n_a)
    def _():
        b_ref[...] = y[0:b_ref.shape[0], :]


def _rmsnorm_split(x, gain, rows_a):
    m, d = x.shape
    n_a = rows_a // STACK_TILE
    rows_b = m - rows_a
    return pl.pallas_call(
        functools.partial(_rmsnorm_split_kernel, n_a=n_a),
        out_shape=(jax.ShapeDtypeStruct((rows_a, d), F32), jax.ShapeDtypeStruct((rows_b, d), F32)),
        grid=(n_a + 1,),
        in_specs=[
            pl.BlockSpec((STACK_TILE, d), lambda i: (i, 0)),
            pl.BlockSpec((1, d), lambda i: (0, 0)),
        ],
        out_specs=(
            pl.BlockSpec((STACK_TILE, d), lambda i: (jnp.minimum(i, n_a - 1), 0)),
            pl.BlockSpec((rows_b, d), lambda i: (0, 0)),
        ),
        compiler_params=_params("arbitrary"),
        name="rmsnorm_split",
    )(x, gain.reshape(1, d))


def _matmul_kernel(*refs, n_w, mode, lhs_order):
    lhs_ref = refs[0]
    w_refs = refs[1:1 + n_w]
    rest = refs[1 + n_w:]

    if lhs_order is None:
        lhs = lhs_ref[...]
    else:
        lhs = jnp.concatenate([lhs_ref[c] for c in lhs_order], axis=1)
    acc = [jnp.dot(lhs, w_ref[...].astype(BF16), preferred_element_type=F32) for w_ref in w_refs]
    if mode == "swiglu":
        (o_ref,) = rest
        gate, up = acc
        o_ref[...] = (0.5 * _silu(gate) * up).astype(o_ref.dtype)
    elif mode == "resid":
        x_ref, o_ref = rest
        o_ref[...] = x_ref[...] + acc[0]
    elif mode == "resid_norm":
        x_ref, g_ref, o_ref, n_ref = rest
        x = x_ref[...] + acc[0]
        o_ref[...] = x
        y = x * lax.rsqrt(jnp.mean(x * x, axis=-1, keepdims=True) + EPS)
        n_ref[...] = (y * g_ref[...]).astype(n_ref.dtype)
    else:
        (o_ref,) = rest
        for c in range(o_ref.shape[0]):
            o_ref[c] = acc[0][:, c * LANES:(c + 1) * LANES]


def _matmul(lhs, weights, layer, *, mode, tm, tn, n_out, col_off=0, resid=None, norm=None, weight_buffers=2,
            lhs_order=None):
    if lhs_order is None:
        m, k = lhs.shape
        lhs_spec = pl.BlockSpec((tm, k), lambda j, i: (i, 0))
    else:
        m, k = lhs.shape[1], lhs.shape[0] * LANES
        lhs_spec = pl.BlockSpec((lhs.shape[0], tm, LANES), lambda j, i: (0, i, 0))
    n_w = len(weights)
    grid = (pl.cdiv(n_out, tn), m // tm)
    in_specs = [lhs_spec]
    in_specs += [pl.BlockSpec((None, k, tn), lambda j, i: (layer, 0, j + col_off),
                              pipeline_mode=pl.Buffered(weight_buffers)) for _ in weights]
    args = [lhs, *weights]
    tile = pl.BlockSpec((tm, tn), lambda j, i: (i, j))
    if mode in ("resid", "resid_norm"):
        in_specs.append(tile)
        args.append(resid)
    if mode == "resid_norm":
        assert tn == n_out
        gains, norm_layer = norm
        in_specs.append(pl.BlockSpec((None, 1, tn), lambda j, i: (norm_layer, 0, 0)))
        args.append(gains.reshape(-1, 1, tn))
        out_shape = (jax.ShapeDtypeStruct((m, n_out), F32), jax.ShapeDtypeStruct((m, n_out), BF16))
        out_spec = (tile, tile)
    elif mode == "cols":
        out_shape = jax.ShapeDtypeStruct((n_out // LANES, m, LANES), F32)
        out_spec = pl.BlockSpec((tn // LANES, tm, LANES), lambda j, i: (j, i, 0))
    else:
        out_shape = jax.ShapeDtypeStruct((m, n_out), BF16 if mode == "swiglu" else F32)
        out_spec = tile
    return pl.pallas_call(
        functools.partial(_matmul_kernel, n_w=n_w, mode=mode, lhs_order=lhs_order),
        out_shape=out_shape,
        grid=grid,
        in_specs=in_specs,
        out_specs=out_spec,
        compiler_params=_params("arbitrary", "arbitrary"),
        name="matmul_" + mode,
    )(*args)


def _rope_kernel(inv_ref, cos_ref, sin_ref, *, pos0, step):
    shape = cos_ref.shape
    row = _iota(shape, 0) + pl.program_id(0) * shape[0]
    ang = (pos0 + step * row.astype(F32)) * inv_ref[...]
    sin = jnp.sin(ang)
    cos_ref[...] = jnp.cos(ang)
    sin_ref[...] = jnp.where(_iota(shape, 1) < HEAD_DIM // 2, -sin, sin)


def _rope_tables(rows, tile, pos0, step):
    inv = ROPE_BASE ** (-jnp.arange(0, HEAD_DIM, 2, dtype=F32) / HEAD_DIM)
    inv = jnp.concatenate([inv, inv]).reshape(1, HEAD_DIM)
    out = jax.ShapeDtypeStruct((rows, HEAD_DIM), F32)
    spec = pl.BlockSpec((tile, HEAD_DIM), lambda i: (i, 0))
    return pl.pallas_call(
        functools.partial(_rope_kernel, pos0=float(pos0), step=float(step)),
        out_shape=(out, out),
        grid=(rows // tile,),
        in_specs=[pl.BlockSpec((1, HEAD_DIM), lambda i: (0, 0))],
        out_specs=(spec, spec),
        compiler_params=_params("arbitrary"),
        name="rope_tables",
    )(inv)


def _rope(x, cos, sin):
    return x * cos + pltpu.roll(x, HEAD_DIM // 2, 1) * sin


def _mixer_call(kernel_fn, name, grid, in_specs, args, outs, scratch, semantics):
    in_specs, args, aliases = list(in_specs), list(args), {}
    for k, (_, _, _, prev) in enumerate(outs):
        if prev is not None:
            aliases[len(args)] = k
            in_specs.append(pl.BlockSpec(memory_space=pl.ANY))
            args.append(prev)
    n_alias = len(aliases)

    def body(*refs):
        n_in = len(args) - n_alias
        kernel_fn(*refs[:n_in], *refs[n_in + n_alias:])

    return pl.pallas_call(
        body,
        out_shape=tuple(jax.ShapeDtypeStruct(s, dt) for s, dt, _, _ in outs),
        grid=grid,
        in_specs=in_specs,
        out_specs=tuple(spec for _, _, spec, _ in outs),
        scratch_shapes=scratch,
        input_output_aliases=aliases,
        compiler_params=_params(*semantics),
        name=name,
    )(*args)


def _state_out(state, layer, prev):
    tail = state.shape[2:]
    zeros = (0,) * len(tail)
    spec = pl.BlockSpec((None, STEP_ROWS) + tail, lambda g: (layer, g) + zeros)
    return (state.shape, F32, spec, prev)


def _hgrn_lower_bound(logits, layer):
    mx = jnp.max(logits, axis=0, keepdims=True)
    e = jnp.exp(logits - mx)
    p = e / jnp.sum(e, axis=0, keepdims=True)
    if layer == 0:
        return jnp.zeros_like(mx)
    return jnp.sum(p[1:layer + 1], axis=0, keepdims=True)


def _hgrn_gates(fz, lb):
    log_lb = jnp.log(lb)
    y = jnp.log1p(-lb) + _log_sigmoid(fz)
    log_f = jnp.maximum(log_lb, y) + jnp.log1p(jnp.exp(-jnp.abs(log_lb - y)))
    key = (1.0 - lb) / (1.0 + jnp.exp(fz))
    return log_f, key


def _hgrn_prompt_kernel(z_ref, lbl_ref, gn_ref, o_ref, s_ref, st_ref, q_scr, k_scr, b_scr, a_scr, *, layer, n_heads):
    t = pl.program_id(1)
    n_sub = BLK // SUB
    lb = _hgrn_lower_bound(lbl_ref[...], layer)
    row = _iota((BLK, LANES), 0)
    col = _iota((BLK, LANES), 1)
    pos = row & (SUB - 1)
    seg0 = row - pos

    @pl.when(t == 0)
    def _():
        st_ref[...] = jnp.zeros_like(st_ref)

    def seg_row(x, s):
        x3 = x.reshape(n_sub, SUB, LANES)
        return jnp.broadcast_to(x3[:, s:s + 1, :], x3.shape).reshape(BLK, LANES)

    def block(i, carry):
        rows = pl.ds(pl.multiple_of(i * BLK, BLK), BLK)
        span = jnp.zeros((BLK, LANES), F32)
        for h in range(n_heads):
            log_f, k = _hgrn_gates(z_ref[n_heads + h, rows, :], lb[:, h * LANES:(h + 1) * LANES])
            b = _seg_cumsum(log_f, SUB)
            q_scr[h] = _silu(z_ref[h, rows, :])
            k_scr[h] = k
            b_scr[h] = b
            span = jnp.maximum(span, -seg_row(b, SUB - 1))
        mild = jnp.max(span) <= DECAY_SPAN_MAX

        @pl.when(mild)
        def _():
            for h in range(n_heads):
                b = b_scr[h]
                b_end = seg_row(b, SUB - 1)
                q_end = (q_scr[h] * jnp.exp(b - b_end)).astype(BF16)
                k_end = (k_scr[h] * jnp.exp(b_end - b)).astype(BF16)
                a = lax.dot_general(q_end, k_end, _TRANS_B, preferred_element_type=F32)
                a_scr[h] = jnp.where((col >= seg0) & (col <= row), a, 0.0)

        @pl.when(jnp.logical_not(mild))
        def _():
            for h in range(n_heads):
                q, k, b = q_scr[h], k_scr[h], b_scr[h]
                a = jnp.zeros((BLK, LANES), F32)
                for s in range(SUB):
                    e = jnp.exp(jnp.where(pos >= s, b - seg_row(b, s), -jnp.inf))
                    c = jnp.sum(q * seg_row(k, s) * e, axis=-1, keepdims=True)
                    a = jnp.where(col == seg0 + s, c, a)
                a_scr[h] = a

        for h in range(n_heads):
            cols = slice(h * LANES, (h + 1) * LANES)
            b = b_scr[h]
            b_end = seg_row(b, SUB - 1)
            vb = z_ref[2 * n_heads + h, rows, :].astype(BF16)
            q_in = (q_scr[h] * jnp.exp(b)).astype(BF16)
            k_out = (k_scr[h] * jnp.exp(b_end - b)).astype(BF16)
            lam = jnp.exp(b_end)
            o = jnp.dot(a_scr[h].astype(BF16), vb, preferred_element_type=F32)
            subs = [slice(n * SUB, (n + 1) * SUB) for n in range(n_sub)]
            ups = [lax.dot_general(vb[sl], k_out[sl], _TRANS_A, preferred_element_type=F32) for sl in subs]
            states = [st_ref[h]]
            for n in range(n_sub):
                states.append(states[n] * lam[n * SUB:n * SUB + 1, :] + ups[n])
            st_ref[h] = states[n_sub]
            parts = [lax.dot_general(q_in[sl], states[n].astype(BF16), _TRANS_B, preferred_element_type=F32)
                     for n, sl in enumerate(subs)]
            o = o + jnp.concatenate(parts, axis=0)
            gate = _silu(z_ref[3 * n_heads + h, rows, :])
            o_ref[h, rows, :] = _head_norm_gate(o, gn_ref[:, cols], gate).astype(o_ref.dtype)
        return carry

    lax.fori_loop(0, z_ref.shape[1] // BLK, block, 0)

    @pl.when(t == pl.num_programs(1) - 1)
    def _():
        for h in range(n_heads):
            s_ref[h] = st_ref[h].T


def _hgrn_step_kernel(z_ref, s_ref, lbl_ref, gn_ref, o_ref, so_ref, f_scr, k_scr, q_scr, *, layer, n_heads):
    g = pl.program_id(0)
    n_rows = z_ref.shape[1]

    @pl.when(g == 0)
    def _():
        lb = _hgrn_lower_bound(lbl_ref[...], layer)
        for h in range(n_heads):
            log_f, k = _hgrn_gates(z_ref[n_heads + h], lb[:, h * LANES:(h + 1) * LANES])
            f_scr[h] = jnp.exp(log_f).T
            k_scr[h] = k.T
            q_scr[h] = _silu(z_ref[h]).T

    lane = _iota((HEAD_DIM, n_rows), 1)

    def body(j, outs):
        b = g * STEP_ROWS + j
        sel = lane == b
        one = pl.ds(b, 1)
        new = []
        for h in range(n_heads):
            s_new = _pick_lane(f_scr[h], sel) * s_ref[j, h] + _pick_lane(k_scr[h], sel) * z_ref[2 * n_heads + h, one, :]
            so_ref[j, h] = s_new
            o = jnp.sum(_pick_lane(q_scr[h], sel) * s_new, axis=0, keepdims=True)
            new.append(_put_row(outs[h], j, o))
        return tuple(new)

    zero = jnp.zeros((STEP_ROWS, LANES), F32)
    outs = lax.fori_loop(0, STEP_ROWS, body, (zero,) * n_heads, unroll=STEP_UNROLL)
    grp = pl.ds(pl.multiple_of(g * STEP_ROWS, STEP_ROWS), STEP_ROWS)
    for h in range(n_heads):
        gate = _silu(z_ref[3 * n_heads + h, grp, :])
        o_ref[h] = _head_norm_gate(outs[h], gn_ref[:, h * LANES:(h + 1) * LANES], gate).astype(o_ref.dtype)


def _ret_prompt_kernel(z_ref, cos_ref, sin_ref, lg_ref, gn_ref, o_ref, s_ref, st_ref, dec_ref, *, n_heads):
    t = pl.program_id(1)

    @pl.when((pl.program_id(0) == 0) & (t == 0))
    def _():
        row = _iota((BLK, LANES), 0).astype(F32)
        diff = row - _iota((BLK, LANES), 1).astype(F32)
        for h in range(n_heads):
            lg = lg_ref[h]
            dec_ref[h, 0] = jnp.where(diff >= 0.0, jnp.exp(diff * lg), 0.0)
            dec_ref[h, 1] = jnp.exp((row + 1.0) * lg)
            dec_ref[h, 2] = jnp.exp((BLK - 1.0 - row) * lg)

    @pl.when(t == 0)
    def _():
        st_ref[...] = jnp.zeros_like(st_ref)

    def chunk(i, carry):
        rows = pl.ds(pl.multiple_of(i * BLK, BLK), BLK)
        tab = pl.ds(pl.multiple_of(t * z_ref.shape[1] + i * BLK, BLK), BLK)
        cos = cos_ref[tab, :]
        sin = sin_ref[tab, :]
        for h in range(n_heads):
            cols = slice(h * LANES, (h + 1) * LANES)
            q = _rope(z_ref[h, rows, :], cos, sin)
            k = _rope(z_ref[n_heads + h, rows, :], cos, sin) * K_SCALE
            vb = z_ref[2 * n_heads + h, rows, :].astype(BF16)
            a = lax.dot_general(q.astype(BF16), k.astype(BF16), _TRANS_B, preferred_element_type=F32) * dec_ref[h, 0]
            st = st_ref[h]
            o = jnp.dot(a.astype(BF16), vb, preferred_element_type=F32)
            o = o + jnp.dot((q * dec_ref[h, 1]).astype(BF16), st.astype(BF16), preferred_element_type=F32)
            s_dec = jnp.exp(jnp.full((1, LANES), BLK, F32) * lg_ref[h])
            st_ref[h] = s_dec * st + jnp.dot((k * dec_ref[h, 2]).T.astype(BF16), vb, preferred_element_type=F32)
            gate = _silu(z_ref[3 * n_heads + h, rows, :])
            o_ref[h, rows, :] = _head_norm_gate(o, gn_ref[:, cols], gate).astype(o_ref.dtype)
        return carry

    lax.fori_loop(0, z_ref.shape[1] // BLK, chunk, 0)

    @pl.when(t == pl.num_programs(1) - 1)
    def _():
        s_ref[...] = st_ref[...]


def _ret_step_kernel(z_ref, s_ref, cos_ref, sin_ref, lg_ref, gn_ref, o_ref, so_ref, k_scr, q_scr, *, n_heads):
    g = pl.program_id(0)
    n_rows = z_ref.shape[1]

    @pl.when(g == 0)
    def _():
        cos = cos_ref[0:1, :]
        sin = sin_ref[0:1, :]
        for h in range(n_heads):
            q_scr[h] = _rope(z_ref[h], cos, sin).T
            k_scr[h] = (_rope(z_ref[n_heads + h], cos, sin) * K_SCALE).T

    lane = _iota((HEAD_DIM, n_rows), 1)

    def body(j, outs):
        b = g * STEP_ROWS + j
        sel = lane == b
        one = pl.ds(b, 1)
        new = []
        for h in range(n_heads):
            gamma = jnp.exp(jnp.full((1, LANES), lg_ref[h], F32))
            s_new = gamma * s_ref[j, h] + _pick_lane(k_scr[h], sel) * z_ref[2 * n_heads + h, one, :]
            so_ref[j, h] = s_new
            o = jnp.sum(_pick_lane(q_scr[h], sel) * s_new, axis=0, keepdims=True)
            new.append(_put_row(outs[h], j, o))
        return tuple(new)

    zero = jnp.zeros((STEP_ROWS, LANES), F32)
    outs = lax.fori_loop(0, STEP_ROWS, body, (zero,) * n_heads, unroll=STEP_UNROLL)
    grp = pl.ds(pl.multiple_of(g * STEP_ROWS, STEP_ROWS), STEP_ROWS)
    for h in range(n_heads):
        gate = _silu(z_ref[3 * n_heads + h, grp, :])
        o_ref[h] = _head_norm_gate(outs[h], gn_ref[:, h * LANES:(h + 1) * LANES], gate).astype(o_ref.dtype)


def _mlstm_prompt_kernel(z_ref, zg_ref, bias_ref, gn_ref, o_ref, c_ref, n_ref, m_ref,
                         c_scr, n_scr, m_scr, *, n_heads):
    t = pl.program_id(1)
    lane = _iota((BLK, LANES), 1)
    causal = lane <= _iota((BLK, LANES), 0)

    @pl.when(t == 0)
    def _():
        c_scr[...] = jnp.zeros_like(c_scr)
        n_scr[...] = jnp.zeros_like(n_scr)
        m_scr[...] = jnp.zeros_like(m_scr)

    def chunk(i, carry):
        rows = pl.ds(pl.multiple_of(i * BLK, BLK), BLK)
        gates = zg_ref[rows, :] + bias_ref[...]
        log_f_all = pltpu.roll(_log_sigmoid(gates), LANES - n_heads, 1)
        bt_all = _seg_cumsum(log_f_all, BLK)
        w_t = (gates - bt_all).T
        for h in range(n_heads):
            cols = slice(h * LANES, (h + 1) * LANES)
            sel = lane == h
            log_i = _pick_lane(gates, sel)
            bt = _pick_lane(bt_all, sel)
            log_d = jnp.where(causal, bt + w_t[h:h + 1, :], -jnp.inf)
            m_prev = m_scr[h]
            log_inter = bt + m_prev
            m_t = jnp.maximum(log_inter, jnp.max(log_d, axis=-1, keepdims=True))
            d_mat = jnp.exp(log_d - m_t)
            w_inter = jnp.exp(log_inter - m_t)
            q = z_ref[h, rows, :]
            k = z_ref[n_heads + h, rows, :] * K_SCALE
            qb = q.astype(BF16)
            vb = z_ref[2 * n_heads + h, rows, :].astype(BF16)
            sc = lax.dot_general(qb, k.astype(BF16), _TRANS_B, preferred_element_type=F32) * d_mat
            cm = c_scr[h]
            nm = n_scr[h]
            num = jnp.dot(sc.astype(BF16), vb, preferred_element_type=F32)
            num = num + jnp.dot(qb, cm.astype(BF16), preferred_element_type=F32) * w_inter
            den = jnp.sum(sc, axis=-1, keepdims=True) + w_inter * jnp.sum(q * nm, axis=-1, keepdims=True)
            hid = num / jnp.maximum(jnp.abs(den), jnp.exp(-m_t))
            m_end = m_t[BLK - 1:BLK, :]
            b_end = bt[BLK - 1:BLK, :]
            kw = k * jnp.exp(b_end - bt + log_i - m_end)
            w_c = jnp.exp(b_end + m_prev - m_end)
            c_scr[h] = w_c * cm + jnp.dot(kw.T.astype(BF16), vb, preferred_element_type=F32)
            n_scr[h] = w_c * nm + jnp.sum(kw, axis=0, keepdims=True)
            m_scr[h] = m_end
            gate = jax.nn.sigmoid(z_ref[3 * n_heads + h, rows, :])
            o_ref[h, rows, :] = _head_norm_gate(hid, gn_ref[:, cols], gate).astype(o_ref.dtype)
        return carry

    lax.fori_loop(0, z_ref.shape[1] // BLK, chunk, 0)

    @pl.when(t == pl.num_programs(1) - 1)
    def _():
        c_ref[...] = c_scr[...]
        n_ref[...] = n_scr[...]
        m_ref[...] = m_scr[...]


def _mlstm_step_kernel(z_ref, zg_ref, bias_ref, c_ref, n_ref, m_ref, gn_ref,
                       o_ref, co_ref, no_ref, mo_ref, k_scr, q_scr, wc_scr, wk_scr, em_scr, *, n_heads):
    g = pl.program_id(0)
    n_rows = z_ref.shape[1]

    @pl.when(g == 0)
    def _():
        lanes = _iota((n_rows, LANES), 1)
        gates = zg_ref[...] + bias_ref[...]
        log_f = pltpu.roll(_log_sigmoid(gates), LANES - n_heads, 1)
        m_old = m_ref[...]
        m_new = jnp.maximum(log_f + m_old, gates)
        mo_ref[...] = m_new
        w_c = jnp.exp(log_f + m_old - m_new)
        w_k = jnp.exp(gates - m_new)
        e_m = jnp.exp(-m_new)
        for h in range(n_heads):
            sel = lanes == h
            wc_scr[h] = jnp.broadcast_to(_pick_lane(w_c, sel), (n_rows, LANES))
            wk_scr[h] = jnp.broadcast_to(_pick_lane(w_k, sel), (n_rows, LANES))
            em_scr[h] = jnp.broadcast_to(_pick_lane(e_m, sel), (n_rows, LANES))
            q_scr[h] = z_ref[h].T
            k_scr[h] = (z_ref[n_heads + h] * K_SCALE).T

    lane = _iota((HEAD_DIM, n_rows), 1)

    def body(j, outs):
        b = g * STEP_ROWS + j
        sel = lane == b
        one = pl.ds(b, 1)
        new = []
        for h in range(n_heads):
            w_c = wc_scr[h, one, :]
            w_k = wk_scr[h, one, :]
            c_new = w_c * c_ref[j, h] + (w_k * _pick_lane(k_scr[h], sel)) * z_ref[2 * n_heads + h, one, :]
            co_ref[j, h] = c_new
            n_new = w_c * n_ref[j, h:h + 1, :] + w_k * (z_ref[n_heads + h, one, :] * K_SCALE)
            no_ref[j, h:h + 1, :] = n_new
            num = jnp.sum(_pick_lane(q_scr[h], sel) * c_new, axis=0, keepdims=True)
            new.append(_put_row(outs[2 * h], j, num))
            new.append(_put_row(outs[2 * h + 1], j, n_new))
        return tuple(new)

    zero = jnp.zeros((STEP_ROWS, LANES), F32)
    outs = lax.fori_loop(0, STEP_ROWS, body, (zero,) * (2 * n_heads), unroll=STEP_UNROLL)
    grp = pl.ds(pl.multiple_of(g * STEP_ROWS, STEP_ROWS), STEP_ROWS)
    for h in range(n_heads):
        num, n_new = outs[2 * h], outs[2 * h + 1]
        den = jnp.sum(z_ref[h, grp, :] * n_new, axis=-1, keepdims=True)
        hid = num / jnp.maximum(jnp.abs(den), em_scr[h, grp, :])
        gate = jax.nn.sigmoid(z_ref[3 * n_heads + h, grp, :])
        o_ref[h] = _head_norm_gate(hid, gn_ref[:, h * LANES:(h + 1) * LANES], gate).astype(o_ref.dtype)


def kernel(x_prompt, x_sample, state_hgrn, state_mlstm_c, state_mlstm_n, state_mlstm_m, state_ret, norm_ffn1, ffn1_w_gate, ffn1_w_up, ffn1_w_down, norm_mix, w_in, hgrn_lb_logits, mlstm_gate_bias, gn_hgrn, gn_mlstm, gn_ret, w_out, norm_ffn2, ffn2_w_gate, ffn2_w_up, ffn2_w_down, norm_final):
    bp, t_len, d = x_prompt.shape
    bs, dec_len, _ = x_sample.shape
    depth = w_in.shape[0]
    h_hgrn, h_mlstm, h_ret = state_hgrn.shape[2], state_mlstm_c.shape[2], state_ret.shape[2]
    w_hgrn, w_mlstm, w_ret = h_hgrn * HEAD_DIM, h_mlstm * HEAD_DIM, h_ret * HEAD_DIM
    d_ff = ffn1_w_gate.shape[2]
    n_prompt = bp * t_len
    m_rows = n_prompt + bs
    seq_tile = min(SEQ_TILE, t_len)
    n_seq = t_len // seq_tile
    assert dec_len == 1 and bs == LANES and n_prompt % bs == 0 and t_len % seq_tile == 0 and seq_tile % BLK == 0
    row_block = n_prompt // bs
    step_block = n_prompt // STEP_ROWS
    tm = m_rows // 8 if m_rows % 128 == 0 else m_rows
    assert m_rows % tm == 0 and tm % 16 == 0 and m_rows % NORM_TILE == 0 and n_prompt % STACK_TILE == 0
    o_mlstm, o_ret, o_hgrn = 0, h_mlstm, h_mlstm + h_ret
    assert o_ret % h_ret == 0 and o_hgrn % h_hgrn == 0
    n_slabs = h_hgrn + h_mlstm + h_ret
    o_order = tuple(range(o_hgrn, n_slabs)) + tuple(range(o_mlstm, o_ret)) + tuple(range(o_ret, o_hgrn))

    gate0 = 4 * w_hgrn + 4 * w_mlstm
    ret0 = gate0 + 2 * h_mlstm
    w_in_ret = jnp.concatenate(
        [w_in[:, :, ret0:], jnp.pad(w_in[:, :, gate0:ret0], ((0, 0), (0, 0), (0, LANES - 2 * h_mlstm)))],
        axis=2).astype(BF16)
    w_in_hm = w_in[:, :, :gate0].astype(BF16)
    gate_slab = 4 * h_ret
    bias_pad = jnp.pad(mlstm_gate_bias, ((0, 0), (0, LANES - 2 * h_mlstm))).reshape(depth, 1, LANES)
    m_pad = jnp.pad(state_mlstm_m, ((0, 0), (0, 0), (0, LANES - h_mlstm)))
    gn_h = gn_hgrn.reshape(depth, 1, w_hgrn)
    gn_m = gn_mlstm.reshape(depth, 1, w_mlstm)
    gn_r = gn_ret.reshape(depth, 1, w_ret)
    log_gamma = jnp.log1p(-jnp.exp2(-5.0 - jnp.arange(h_ret, dtype=F32)))
    cos_p, sin_p = _rope_tables(t_len, min(ROPE_TILE, t_len), 0.0, 1.0)
    cos_s, sin_s = _rope_tables(8, 8, PAST_LEN, 0.0)

    smem = pl.BlockSpec(memory_space=pltpu.SMEM)
    o3_shape = (n_slabs, m_rows, LANES)
    head_sq = (HEAD_DIM, HEAD_DIM)

    def z_prompt(n_blocks):
        return pl.BlockSpec((n_blocks, seq_tile, LANES), lambda b, t: (0, b * n_seq + t, 0))

    def o_prompt(heads, slab0):
        return pl.BlockSpec((heads, seq_tile, LANES), lambda b, t: (slab0 // heads, b * n_seq + t, 0))

    def o_step(heads, slab0):
        return pl.BlockSpec((heads, STEP_ROWS, LANES), lambda g: (slab0 // heads, step_block + g, 0))

    x, n = _stack_rows(x_prompt.reshape(n_prompt, d), x_sample.reshape(bs, d), norm_ffn1[0])
    new_h_p, new_c_p, new_n_p, new_m_p, new_r_p = [], [], [], [], []
    new_n_s, new_m_s = [], []
    st_h = st_c = st_r = None

    def ffn(x, n, w_gate, w_up, w_down, layer):
        mid = _matmul(n, [w_gate, w_up], layer, mode="swiglu", tm=2 * tm, tn=512, n_out=d_ff)
        return _matmul(mid, [w_down], layer, mode="resid", tm=tm // 2, tn=1024, n_out=d, resid=x, weight_buffers=1)

    for layer in range(depth):
        if layer > 0:
            n = _rmsnorm(x, norm_ffn1, layer, BF16)
        x = ffn(x, n, ffn1_w_gate, ffn1_w_up, ffn1_w_down, layer)

        n = _rmsnorm(x, norm_mix, layer, BF16)
        z_h = _matmul(n, [w_in_hm], layer, mode="cols", tm=2 * tm, tn=1024, n_out=4 * w_hgrn)
        z_m = _matmul(n, [w_in_hm], layer, mode="cols", tm=2 * tm, tn=1024, n_out=4 * w_mlstm,
                      col_off=4 * w_hgrn // 1024)
        z_r = _matmul(n, [w_in_ret], layer, mode="cols", tm=2 * tm, tn=640, n_out=4 * w_ret + LANES)

        def gain2(width):
            return pl.BlockSpec((None, 1, width), lambda b, t: (layer, 0, 0))

        def gain1(width):
            return pl.BlockSpec((None, 1, width), lambda g: (layer, 0, 0))

        o3, s_hp = _mixer_call(
            functools.partial(_hgrn_prompt_kernel, layer=layer, n_heads=h_hgrn), "hgrn_prompt", (bp, n_seq),
            [z_prompt(4 * h_hgrn), pl.BlockSpec(hgrn_lb_logits.shape, lambda b, t: (0, 0)), gain2(w_hgrn)],
            [z_h, hgrn_lb_logits, gn_h],
            [(o3_shape, BF16, o_prompt(h_hgrn, o_hgrn), None),
             ((bp, h_hgrn) + head_sq, F32, pl.BlockSpec((None, h_hgrn) + head_sq, lambda b, t: (b, 0, 0, 0)), None)],
            [pltpu.VMEM((h_hgrn,) + head_sq, F32)] + [pltpu.VMEM((h_hgrn, BLK, LANES), F32) for _ in range(4)],
            ("arbitrary", "arbitrary"))

        vec = ((bp, h_mlstm, 1, LANES), F32, pl.BlockSpec((None, h_mlstm, 1, LANES), lambda b, t: (b, 0, 0, 0)), None)
        o3, c_p, n_p, m_p = _mixer_call(
            functools.partial(_mlstm_prompt_kernel, n_heads=h_mlstm), "mlstm_prompt", (bp, n_seq),
            [z_prompt(4 * h_mlstm),
             pl.BlockSpec((None, seq_tile, LANES), lambda b, t: (gate_slab, b * n_seq + t, 0)),
             pl.BlockSpec((None, 1, LANES), lambda b, t: (layer, 0, 0)), gain2(w_mlstm)],
            [z_m, z_r, bias_pad, gn_m],
            [(o3_shape, BF16, o_prompt(h_mlstm, o_mlstm), o3),
             ((bp, h_mlstm) + head_sq, F32, pl.BlockSpec((None, h_mlstm) + head_sq, lambda b, t: (b, 0, 0, 0)), None),
             vec, vec],
            [pltpu.VMEM((h_mlstm,) + head_sq, F32), pltpu.VMEM((h_mlstm, 1, LANES), F32),
             pltpu.VMEM((h_mlstm, 1, LANES), F32)], ("arbitrary", "arbitrary"))

        table = pl.BlockSpec((t_len, LANES), lambda b, t: (0, 0))
        o3, s_rp = _mixer_call(
            functools.partial(_ret_prompt_kernel, n_heads=h_ret), "ret_prompt", (bp, n_seq),
            [z_prompt(4 * h_ret), table, table, smem, gain2(w_ret)],
            [z_r, cos_p, sin_p, log_gamma, gn_r],
            [(o3_shape, BF16, o_prompt(h_ret, o_ret), o3),
             ((bp, h_ret) + head_sq, F32, pl.BlockSpec((None, h_ret) + head_sq, lambda b, t: (b, 0, 0, 0)), None)],
            [pltpu.VMEM((h_ret,) + head_sq, F32), pltpu.VMEM((h_ret, 3, BLK, LANES), F32)],
            ("arbitrary", "arbitrary"))

        def z_step(n_blocks):
            return pl.BlockSpec((n_blocks, bs, LANES), lambda g: (0, row_block, 0))

        def state_in(state):
            tail = state.shape[2:]
            zeros = (0,) * len(tail)
            return pl.BlockSpec((None, STEP_ROWS) + tail, lambda g: (layer, g) + zeros)

        tposed = pltpu.VMEM((h_hgrn, HEAD_DIM, bs), F32)
        o3, st_h = _mixer_call(
            functools.partial(_hgrn_step_kernel, layer=layer, n_heads=h_hgrn), "hgrn_step", (bs // STEP_ROWS,),
            [z_step(4 * h_hgrn), state_in(state_hgrn), pl.BlockSpec(hgrn_lb_logits.shape, lambda g: (0, 0)), gain1(w_hgrn)],
            [z_h, state_hgrn, hgrn_lb_logits, gn_h],
            [(o3_shape, BF16, o_step(h_hgrn, o_hgrn), o3), _state_out(state_hgrn, layer, st_h)],
            [tposed, tposed, tposed], ("arbitrary",))

        tposed = pltpu.VMEM((h_mlstm, HEAD_DIM, bs), F32)
        rowrep = pltpu.VMEM((h_mlstm, bs, LANES), F32)
        tile = pl.BlockSpec((bs, LANES), lambda g: (0, 0))
        o3, st_c, n_s, m_s = _mixer_call(
            functools.partial(_mlstm_step_kernel, n_heads=h_mlstm), "mlstm_step", (bs // STEP_ROWS,),
            [z_step(4 * h_mlstm), pl.BlockSpec((None, bs, LANES), lambda g: (gate_slab, row_block, 0)),
             pl.BlockSpec((None, 1, LANES), lambda g: (layer, 0, 0)),
             state_in(state_mlstm_c), state_in(state_mlstm_n), tile, gain1(w_mlstm)],
            [z_m, z_r, bias_pad, state_mlstm_c, state_mlstm_n, m_pad[layer], gn_m],
            [(o3_shape, BF16, o_step(h_mlstm, o_mlstm), o3), _state_out(state_mlstm_c, layer, st_c),
             (state_mlstm_n.shape[1:], F32, pl.BlockSpec((STEP_ROWS, h_mlstm, HEAD_DIM), lambda g: (g, 0, 0)), None),
             ((bs, LANES), F32, tile, None)],
            [tposed, tposed, rowrep, rowrep, rowrep], ("arbitrary",))

        tposed = pltpu.VMEM((h_ret, HEAD_DIM, bs), F32)
        table = pl.BlockSpec(cos_s.shape, lambda g: (0, 0))
        o3, st_r = _mixer_call(
            functools.partial(_ret_step_kernel, n_heads=h_ret), "ret_step", (bs // STEP_ROWS,),
            [z_step(4 * h_ret), state_in(state_ret), table, table, smem, gain1(w_ret)],
            [z_r, state_ret, cos_s, sin_s, log_gamma, gn_r],
            [(o3_shape, BF16, o_step(h_ret, o_ret), o3), _state_out(state_ret, layer, st_r)],
            [tposed, tposed], ("arbitrary",))

        new_h_p.append(s_hp)
        new_c_p.append(c_p)
        new_n_p.append(n_p[:, :, 0, :])
        new_m_p.append(m_p[:, :, 0, 0])
        new_r_p.append(s_rp)
        new_n_s.append(n_s)
        new_m_s.append(m_s[:, :h_mlstm])

        x, n = _matmul(o3, [w_out], layer, mode="resid_norm", tm=tm // 2, tn=d, n_out=d, resid=x,
                       norm=(norm_ffn2, layer), weight_buffers=1, lhs_order=o_order)

        x = ffn(x, n, ffn2_w_gate, ffn2_w_up, ffn2_w_down, layer)

    y_p, y_s = _rmsnorm_split(x, norm_final, n_prompt)
    return (
        y_p.reshape(bp, t_len, d), y_s.reshape(bs, dec_len, d),
        jnp.stack(new_h_p), jnp.stack(new_c_p), jnp.stack(new_n_p), jnp.stack(new_m_p), jnp.stack(new_r_p),
        st_h, st_c, jnp.stack(new_n_s), jnp.stack(new_m_s), st_r,
    )
```

```python
import functools

import jax
import jax.numpy as jnp
from jax import lax
from jax.experimental import pallas as pl
from jax.experimental.pallas import tpu as pltpu

F32 = jnp.float32
BF16 = jnp.bfloat16

HEAD_DIM = 128
LANES = 128
EPS = 1e-6
ROPE_BASE = 10000.0
K_SCALE = HEAD_DIM ** -0.5
PAST_LEN = 16384

BLK = 128
SUB = 16
SEQ_TILE = 512
STEP_ROWS = 16
STEP_UNROLL = 4
STACK_TILE = 1024
ROPE_TILE = 256
VMEM_LIMIT_BYTES = 56 * 1024 * 1024

DECAY_SPAN_MAX = 60.0

_TRANS_B = (((1,), (1,)), ((), ()))
_TRANS_A = (((0,), (0,)), ((), ()))


def _params(*semantics):
    return pltpu.CompilerParams(dimension_semantics=semantics, vmem_limit_bytes=VMEM_LIMIT_BYTES)


def _silu(x):
    return x * jax.nn.sigmoid(x)


def _log_sigmoid(x):
    return jnp.minimum(x, 0.0) - jnp.log1p(jnp.exp(-jnp.abs(x)))


def _iota(shape, dim):
    return lax.broadcasted_iota(jnp.int32, shape, dim)


def _seg_cumsum(x, seg):
    pos = _iota(x.shape, 0) & (seg - 1)
    shift = 1
    while shift < seg:
        x = x + jnp.where(pos >= shift, pltpu.roll(x, shift, 0), 0.0)
        shift *= 2
    return x


def _head_norm_gate(o, gain, gate):
    return o * lax.rsqrt(jnp.mean(o * o, axis=-1, keepdims=True) + EPS) * gain * gate


def _pick_lane(x, lane_mask):
    return jnp.sum(jnp.where(lane_mask, x, 0.0), axis=-1, keepdims=True)


def _put_row(tile, j, new):
    return jnp.where(_iota(tile.shape, 0) == j, new, tile)


def _stack_rows_kernel(a_ref, b_ref, g_ref, o_ref, n_ref, *, n_a):
    i = pl.program_id(0)

    def emit(x, rows):
        o_ref[rows, :] = x
        y = x * lax.rsqrt(jnp.mean(x * x, axis=-1, keepdims=True) + EPS)
        n_ref[rows, :] = (y * g_ref[...]).astype(n_ref.dtype)

    @pl.when(i < n_a)
    def _():
        emit(a_ref[...], slice(None))

    @pl.when(i == n_a)
    def _():
        emit(b_ref[...], slice(0, b_ref.shape[0]))


def _stack_rows(a, b, gain):
    (ma, d), mb = a.shape, b.shape[0]
    n_a = ma // STACK_TILE
    tile = pl.BlockSpec((STACK_TILE, d), lambda i: (i, 0))
    return pl.pallas_call(
        functools.partial(_stack_rows_kernel, n_a=n_a),
        out_shape=(jax.ShapeDtypeStruct((ma + mb, d), a.dtype), jax.ShapeDtypeStruct((ma + mb, d), BF16)),
        grid=(n_a + 1,),
        in_specs=[
            pl.BlockSpec((STACK_TILE, d), lambda i: (jnp.minimum(i, n_a - 1), 0)),
            pl.BlockSpec((mb, d), lambda i: (0, 0)),
            pl.BlockSpec((1, d), lambda i: (0, 0)),
        ],
        out_specs=(tile, tile),
        compiler_params=_params("arbitrary"),
        name="stack_rows",
    )(a, b, gain.reshape(1, d))


def _rmsnorm_split_kernel(x_ref, g_ref, a_ref, b_ref, *, n_a):
    i = pl.program_id(0)
    x = x_ref[...]
    y = x * lax.rsqrt(jnp.mean(x * x, axis=-1, keepdims=True) + EPS) * g_ref[...]

    @pl.when(i < n_a)
    def _():
        a_ref[...] = y

    @pl.when(i == n_a)
    def _():
        b_ref[...] = y[0:b_ref.shape[0], :]


def _rmsnorm_split(x, gain, rows_a):
    m, d = x.shape
    n_a = rows_a // STACK_TILE
    rows_b = m - rows_a
    return pl.pallas_call(
        functools.partial(_rmsnorm_split_kernel, n_a=n_a),
        out_shape=(jax.ShapeDtypeStruct((rows_a, d), F32), jax.ShapeDtypeStruct((rows_b, d), F32)),
        grid=(n_a + 1,),
        in_specs=[
            pl.BlockSpec((STACK_TILE, d), lambda i: (i, 0)),
            pl.BlockSpec((1, d), lambda i: (0, 0)),
        ],
        out_specs=(
            pl.BlockSpec((STACK_TILE, d), lambda i: (jnp.minimum(i, n_a - 1), 0)),
            pl.BlockSpec((rows_b, d), lambda i: (0, 0)),
        ),
        compiler_params=_params("arbitrary"),
        name="rmsnorm_split",
    )(x, gain.reshape(1, d))


def _matmul_kernel(*refs, n_w, mode, lhs_order, scaled):
    lhs_ref = refs[0]
    w_refs = refs[1:1 + n_w]
    rest = refs[1 + n_w:]

    if lhs_order is None:
        lhs = lhs_ref[...]
    else:
        lhs = jnp.concatenate([lhs_ref[c] for c in lhs_order], axis=1)
    acc = [jnp.dot(lhs, w_ref[...].astype(BF16), preferred_element_type=F32) for w_ref in w_refs]
    if scaled:
        ssq_ref, rest = rest[0], rest[1:]
        total = ssq_ref[0]
        for part in range(1, ssq_ref.shape[0]):
            total = total + ssq_ref[part]
        r = lax.rsqrt(total[:, 0:1] * (1.0 / lhs.shape[1]) + EPS)
        acc = [a * r for a in acc]
    if mode == "swiglu":
        (o_ref,) = rest
        gate, up = acc
        o_ref[...] = (0.5 * _silu(gate) * up).astype(o_ref.dtype)
    elif mode == "resid":
        x_ref, o_ref = rest
        o_ref[...] = x_ref[...] + acc[0]
    elif mode == "resid_gain":
        x_ref, g_ref, o_ref, xg_ref, ssq_ref = rest
        x = x_ref[...] + acc[0]
        o_ref[...] = x
        xg_ref[...] = (x * g_ref[...]).astype(xg_ref.dtype)
        ssq_ref[...] = jnp.broadcast_to(jnp.sum(x * x, axis=-1, keepdims=True), ssq_ref.shape)
    elif mode == "resid_norm":
        x_ref, g_ref, o_ref, n_ref = rest
        x = x_ref[...] + acc[0]
        o_ref[...] = x
        y = x * lax.rsqrt(jnp.mean(x * x, axis=-1, keepdims=True) + EPS)
        n_ref[...] = (y * g_ref[...]).astype(n_ref.dtype)
    else:
        (o_ref,) = rest
        for c in range(o_ref.shape[0]):
            o_ref[c] = acc[0][:, c * LANES:(c + 1) * LANES]


def _matmul(lhs, weights, layer, *, mode, tm, tn, n_out, col_off=0, resid=None, norm=None, row_ssq=None,
            weight_buffers=2, lhs_order=None):
    if lhs_order is None:
        m, k = lhs.shape
        lhs_spec = pl.BlockSpec((tm, k), lambda j, i: (i, 0))
    else:
        m, k = lhs.shape[1], lhs.shape[0] * LANES
        lhs_spec = pl.BlockSpec((lhs.shape[0], tm, LANES), lambda j, i: (0, i, 0))
    n_w = len(weights)
    n_tiles = pl.cdiv(n_out, tn)
    grid = (n_tiles, m // tm)
    in_specs = [lhs_spec]
    in_specs += [pl.BlockSpec((None, k, tn), lambda j, i: (layer, 0, j + col_off),
                              pipeline_mode=pl.Buffered(weight_buffers)) for _ in weights]
    args = [lhs, *weights]
    if row_ssq is not None:
        in_specs.append(pl.BlockSpec((row_ssq.shape[0], tm, LANES), lambda j, i: (0, i, 0)))
        args.append(row_ssq)
    tile = pl.BlockSpec((tm, tn), lambda j, i: (i, j))
    if mode in ("resid", "resid_gain", "resid_norm"):
        in_specs.append(tile)
        args.append(resid)
    if mode in ("resid_gain", "resid_norm"):
        gains, norm_layer = norm
        in_specs.append(pl.BlockSpec((None, 1, tn), lambda j, i: (norm_layer, 0, j)))
        args.append(gains.reshape(-1, 1, n_out))
    if mode == "resid_gain":
        out_shape = (jax.ShapeDtypeStruct((m, n_out), F32), jax.ShapeDtypeStruct((m, n_out), BF16),
                     jax.ShapeDtypeStruct((n_tiles, m, LANES), F32))
        out_spec = (tile, tile, pl.BlockSpec((None, tm, LANES), lambda j, i: (j, i, 0)))
    elif mode == "resid_norm":
        assert tn == n_out
        out_shape = (jax.ShapeDtypeStruct((m, n_out), F32), jax.ShapeDtypeStruct((m, n_out), BF16))
        out_spec = (tile, tile)
    elif mode == "cols":
        out_shape = jax.ShapeDtypeStruct((n_out // LANES, m, LANES), F32)
        out_spec = pl.BlockSpec((tn // LANES, tm, LANES), lambda j, i: (j, i, 0))
    else:
        out_shape = jax.ShapeDtypeStruct((m, n_out), BF16 if mode == "swiglu" else F32)
        out_spec = tile
    return pl.pallas_call(
        functools.partial(_matmul_kernel, n_w=n_w, mode=mode, lhs_order=lhs_order, scaled=row_ssq is not None),
        out_shape=out_shape,
        grid=grid,
        in_specs=in_specs,
        out_specs=out_spec,
        compiler_params=_params("arbitrary", "arbitrary"),
        name="matmul_" + mode,
    )(*args)


def _rope_kernel(inv_ref, cos_ref, sin_ref, *, pos0, step):
    shape = cos_ref.shape
    row = _iota(shape, 0) + pl.program_id(0) * shape[0]
    ang = (pos0 + step * row.astype(F32)) * inv_ref[...]
    sin = jnp.sin(ang)
    cos_ref[...] = jnp.cos(ang)
    sin_ref[...] = jnp.where(_iota(shape, 1) < HEAD_DIM // 2, -sin, sin)


def _rope_tables(rows, tile, pos0, step):
    inv = ROPE_BASE ** (-jnp.arange(0, HEAD_DIM, 2, dtype=F32) / HEAD_DIM)
    inv = jnp.concatenate([inv, inv]).reshape(1, HEAD_DIM)
    out = jax.ShapeDtypeStruct((rows, HEAD_DIM), F32)
    spec = pl.BlockSpec((tile, HEAD_DIM), lambda i: (i, 0))
    return pl.pallas_call(
        functools.partial(_rope_kernel, pos0=float(pos0), step=float(step)),
        out_shape=(out, out),
        grid=(rows // tile,),
        in_specs=[pl.BlockSpec((1, HEAD_DIM), lambda i: (0, 0))],
        out_specs=(spec, spec),
        compiler_params=_params("arbitrary"),
        name="rope_tables",
    )(inv)


def _rope(x, cos, sin):
    return x * cos + pltpu.roll(x, HEAD_DIM // 2, 1) * sin


def _mixer_call(kernel_fn, name, grid, in_specs, args, outs, scratch, semantics):
    in_specs, args, aliases = list(in_specs), list(args), {}
    for k, (_, _, _, prev) in enumerate(outs):
        if prev is not None:
            aliases[len(args)] = k
            in_specs.append(pl.BlockSpec(memory_space=pl.ANY))
            args.append(prev)
    n_alias = len(aliases)

    def body(*refs):
        n_in = len(args) - n_alias
        kernel_fn(*refs[:n_in], *refs[n_in + n_alias:])

    return pl.pallas_call(
        body,
        out_shape=tuple(jax.ShapeDtypeStruct(s, dt) for s, dt, _, _ in outs),
        grid=grid,
        in_specs=in_specs,
        out_specs=tuple(spec for _, _, spec, _ in outs),
        scratch_shapes=scratch,
        input_output_aliases=aliases,
        compiler_params=_params(*semantics),
        name=name,
    )(*args)


def _state_out(state, layer, prev):
    tail = state.shape[2:]
    zeros = (0,) * len(tail)
    spec = pl.BlockSpec((None, STEP_ROWS) + tail, lambda g: (layer, g) + zeros)
    return (state.shape, F32, spec, prev)


def _hgrn_lower_bound(logits, layer):
    mx = jnp.max(logits, axis=0, keepdims=True)
    e = jnp.exp(logits - mx)
    p = e / jnp.sum(e, axis=0, keepdims=True)
    if layer == 0:
        return jnp.zeros_like(mx)
    return jnp.sum(p[1:layer + 1], axis=0, keepdims=True)


def _hgrn_gates(fz, lb):
    log_lb = jnp.log(lb)
    y = jnp.log1p(-lb) + _log_sigmoid(fz)
    log_f = jnp.maximum(log_lb, y) + jnp.log1p(jnp.exp(-jnp.abs(log_lb - y)))
    key = (1.0 - lb) / (1.0 + jnp.exp(fz))
    return log_f, key


def _hgrn_prompt_kernel(z_ref, lbl_ref, gn_ref, o_ref, s_ref, st_ref, q_scr, k_scr, b_scr, a_scr, *, layer, n_heads):
    t = pl.program_id(1)
    n_sub = BLK // SUB
    lb = _hgrn_lower_bound(lbl_ref[...], layer)
    row = _iota((BLK, LANES), 0)
    col = _iota((BLK, LANES), 1)
    pos = row & (SUB - 1)
    seg0 = row - pos

    @pl.when(t == 0)
    def _():
        st_ref[...] = jnp.zeros_like(st_ref)

    def seg_row(x, s):
        x3 = x.reshape(n_sub, SUB, LANES)
        return jnp.broadcast_to(x3[:, s:s + 1, :], x3.shape).reshape(BLK, LANES)

    def block(i, carry):
        rows = pl.ds(pl.multiple_of(i * BLK, BLK), BLK)
        span = jnp.zeros((BLK, LANES), F32)
        for h in range(n_heads):
            log_f, k = _hgrn_gates(z_ref[n_heads + h, rows, :], lb[:, h * LANES:(h + 1) * LANES])
            b = _seg_cumsum(log_f, SUB)
            q_scr[h] = _silu(z_ref[h, rows, :])
            k_scr[h] = k
            b_scr[h] = b
            span = jnp.maximum(span, -seg_row(b, SUB - 1))
        mild = jnp.max(span) <= DECAY_SPAN_MAX

        @pl.when(mild)
        def _():
            for h in range(n_heads):
                b = b_scr[h]
                b_end = seg_row(b, SUB - 1)
                q_end = (q_scr[h] * jnp.exp(b - b_end)).astype(BF16)
                k_end = (k_scr[h] * jnp.exp(b_end - b)).astype(BF16)
                a = lax.dot_general(q_end, k_end, _TRANS_B, preferred_element_type=F32)
                a_scr[h] = jnp.where((col >= seg0) & (col <= row), a, 0.0)

        @pl.when(jnp.logical_not(mild))
        def _():
            for h in range(n_heads):
                q, k, b = q_scr[h], k_scr[h], b_scr[h]
                a = jnp.zeros((BLK, LANES), F32)
                for s in range(SUB):
                    e = jnp.exp(jnp.where(pos >= s, b - seg_row(b, s), -jnp.inf))
                    c = jnp.sum(q * seg_row(k, s) * e, axis=-1, keepdims=True)
                    a = jnp.where(col == seg0 + s, c, a)
                a_scr[h] = a

        for h in range(n_heads):
            cols = slice(h * LANES, (h + 1) * LANES)
            b = b_scr[h]
            b_end = seg_row(b, SUB - 1)
            vb = z_ref[2 * n_heads + h, rows, :].astype(BF16)
            q_in = (q_scr[h] * jnp.exp(b)).astype(BF16)
            k_out = (k_scr[h] * jnp.exp(b_end - b)).astype(BF16)
            lam = jnp.exp(b_end)
            o = jnp.dot(a_scr[h].astype(BF16), vb, preferred_element_type=F32)
            subs = [slice(n * SUB, (n + 1) * SUB) for n in range(n_sub)]
            ups = [lax.dot_general(vb[sl], k_out[sl], _TRANS_A, preferred_element_type=F32) for sl in subs]
            states = [st_ref[h]]
            for n in range(n_sub):
                states.append(states[n] * lam[n * SUB:n * SUB + 1, :] + ups[n])
            st_ref[h] = states[n_sub]
            parts = [lax.dot_general(q_in[sl], states[n].astype(BF16), _TRANS_B, preferred_element_type=F32)
                     for n, sl in enumerate(subs)]
            o = o + jnp.concatenate(parts, axis=0)
            gate = _silu(z_ref[3 * n_heads + h, rows, :])
            o_ref[h, rows, :] = _head_norm_gate(o, gn_ref[:, cols], gate).astype(o_ref.dtype)
        return carry

    lax.fori_loop(0, z_ref.shape[1] // BLK, block, 0)

    @pl.when(t == pl.num_programs(1) - 1)
    def _():
        for h in range(n_heads):
            s_ref[h] = st_ref[h].T


def _hgrn_step_kernel(z_ref, s_ref, lbl_ref, gn_ref, o_ref, so_ref, f_scr, k_scr, *, layer, n_heads):
    g = pl.program_id(0)
    n_rows = z_ref.shape[1]

    @pl.when(g == 0)
    def _():
        lb = _hgrn_lower_bound(lbl_ref[...], layer)
        for h in range(n_heads):
            log_f, k = _hgrn_gates(z_ref[n_heads + h], lb[:, h * LANES:(h + 1) * LANES])
            f_scr[h] = jnp.exp(log_f).T
            k_scr[h] = k.T

    lane = _iota((HEAD_DIM, n_rows), 1)
    grp = pl.ds(pl.multiple_of(g * STEP_ROWS, STEP_ROWS), STEP_ROWS)
    q = [_silu(z_ref[h, grp, :]).astype(BF16) for h in range(n_heads)]

    def body(j, outs):
        b = g * STEP_ROWS + j
        sel = lane == b
        one = pl.ds(b, 1)
        new = []
        for h in range(n_heads):
            s_new = _pick_lane(f_scr[h], sel) * s_ref[j, h] + _pick_lane(k_scr[h], sel) * z_ref[2 * n_heads + h, one, :]
            so_ref[j, h] = s_new
            new.append(_put_row(outs[h], j, jnp.dot(q[h], s_new.astype(BF16), preferred_element_type=F32)))
        return tuple(new)

    zero = jnp.zeros((STEP_ROWS, LANES), F32)
    outs = lax.fori_loop(0, STEP_ROWS, body, (zero,) * n_heads, unroll=STEP_UNROLL)
    for h in range(n_heads):
        gate = _silu(z_ref[3 * n_heads + h, grp, :])
        o_ref[h] = _head_norm_gate(outs[h], gn_ref[:, h * LANES:(h + 1) * LANES], gate).astype(o_ref.dtype)


def _ret_prompt_kernel(z_ref, cos_ref, sin_ref, lg_ref, gn_ref, o_ref, s_ref, st_ref, dec_ref, *, n_heads):
    t = pl.program_id(1)

    @pl.when((pl.program_id(0) == 0) & (t == 0))
    def _():
        row = _iota((BLK, LANES), 0).astype(F32)
        diff = row - _iota((BLK, LANES), 1).astype(F32)
        for h in range(n_heads):
            lg = lg_ref[h]
            dec_ref[h, 0] = jnp.where(diff >= 0.0, jnp.exp(diff * lg), 0.0)
            dec_ref[h, 1] = jnp.exp((row + 1.0) * lg)
            dec_ref[h, 2] = jnp.exp((BLK - 1.0 - row) * lg)

    @pl.when(t == 0)
    def _():
        st_ref[...] = jnp.zeros_like(st_ref)

    def chunk(i, carry):
        rows = pl.ds(pl.multiple_of(i * BLK, BLK), BLK)
        tab = pl.ds(pl.multiple_of(t * z_ref.shape[1] + i * BLK, BLK), BLK)
        cos = cos_ref[tab, :]
        sin = sin_ref[tab, :]
        for h in range(n_heads):
            cols = slice(h * LANES, (h + 1) * LANES)
            q = _rope(z_ref[h, rows, :], cos, sin)
            k = _rope(z_ref[n_heads + h, rows, :], cos, sin) * K_SCALE
            vb = z_ref[2 * n_heads + h, rows, :].astype(BF16)
            a = lax.dot_general(q.astype(BF16), k.astype(BF16), _TRANS_B, preferred_element_type=F32) * dec_ref[h, 0]
            st = st_ref[h]
            o = jnp.dot(a.astype(BF16), vb, preferred_element_type=F32)
            o = o + jnp.dot((q * dec_ref[h, 1]).astype(BF16), st.astype(BF16), preferred_element_type=F32)
            s_dec = jnp.exp(jnp.full((1, LANES), BLK, F32) * lg_ref[h])
            st_ref[h] = s_dec * st + jnp.dot((k * dec_ref[h, 2]).T.astype(BF16), vb, preferred_element_type=F32)
            gate = _silu(z_ref[3 * n_heads + h, rows, :])
            o_ref[h, rows, :] = _head_norm_gate(o, gn_ref[:, cols], gate).astype(o_ref.dtype)
        return carry

    lax.fori_loop(0, z_ref.shape[1] // BLK, chunk, 0)

    @pl.when(t == pl.num_programs(1) - 1)
    def _():
        s_ref[...] = st_ref[...]


def _ret_step_kernel(z_ref, s_ref, cos_ref, sin_ref, lg_ref, gn_ref, o_ref, so_ref, k_scr, *, n_heads):
    g = pl.program_id(0)
    n_rows = z_ref.shape[1]
    cos = cos_ref[0:1, :]
    sin = sin_ref[0:1, :]

    @pl.when(g == 0)
    def _():
        for h in range(n_heads):
            k_scr[h] = (_rope(z_ref[n_heads + h], cos, sin) * K_SCALE).T

    lane = _iota((HEAD_DIM, n_rows), 1)
    grp = pl.ds(pl.multiple_of(g * STEP_ROWS, STEP_ROWS), STEP_ROWS)
    q = [_rope(z_ref[h, grp, :], cos, sin).astype(BF16) for h in range(n_heads)]

    def body(j, outs):
        b = g * STEP_ROWS + j
        sel = lane == b
        one = pl.ds(b, 1)
        new = []
        for h in range(n_heads):
            gamma = jnp.exp(jnp.full((1, LANES), lg_ref[h], F32))
            s_new = gamma * s_ref[j, h] + _pick_lane(k_scr[h], sel) * z_ref[2 * n_heads + h, one, :]
            so_ref[j, h] = s_new
            new.append(_put_row(outs[h], j, jnp.dot(q[h], s_new.astype(BF16), preferred_element_type=F32)))
        return tuple(new)

    zero = jnp.zeros((STEP_ROWS, LANES), F32)
    outs = lax.fori_loop(0, STEP_ROWS, body, (zero,) * n_heads, unroll=STEP_UNROLL)
    for h in range(n_heads):
        gate = _silu(z_ref[3 * n_heads + h, grp, :])
        o_ref[h] = _head_norm_gate(outs[h], gn_ref[:, h * LANES:(h + 1) * LANES], gate).astype(o_ref.dtype)


def _mlstm_prompt_kernel(z_ref, zg_ref, bias_ref, gn_ref, o_ref, c_ref, n_ref, m_ref,
                         c_scr, n_scr, m_scr, *, n_heads):
    t = pl.program_id(1)
    lane = _iota((BLK, LANES), 1)
    causal = lane <= _iota((BLK, LANES), 0)

    @pl.when(t == 0)
    def _():
        c_scr[...] = jnp.zeros_like(c_scr)
        n_scr[...] = jnp.zeros_like(n_scr)
        m_scr[...] = jnp.zeros_like(m_scr)

    def chunk(i, carry):
        rows = pl.ds(pl.multiple_of(i * BLK, BLK), BLK)
        gates = zg_ref[rows, :] + bias_ref[...]
        log_f_all = pltpu.roll(_log_sigmoid(gates), LANES - n_heads, 1)
        bt_all = _seg_cumsum(log_f_all, BLK)
        w_t = (gates - bt_all).T
        for h in range(n_heads):
            cols = slice(h * LANES, (h + 1) * LANES)
            sel = lane == h
            log_i = _pick_lane(gates, sel)
            bt = _pick_lane(bt_all, sel)
            log_d = jnp.where(causal, bt + w_t[h:h + 1, :], -jnp.inf)
            m_prev = m_scr[h]
            log_inter = bt + m_prev
            m_t = jnp.maximum(log_inter, jnp.max(log_d, axis=-1, keepdims=True))
            d_mat = jnp.exp(log_d - m_t)
            w_inter = jnp.exp(log_inter - m_t)
            q = z_ref[h, rows, :]
            k = z_ref[n_heads + h, rows, :] * K_SCALE
            qb = q.astype(BF16)
            vb = z_ref[2 * n_heads + h, rows, :].astype(BF16)
            sc = lax.dot_general(qb, k.astype(BF16), _TRANS_B, preferred_element_type=F32) * d_mat
            cm = c_scr[h]
            nm = n_scr[h]
            num = jnp.dot(sc.astype(BF16), vb, preferred_element_type=F32)
            num = num + jnp.dot(qb, cm.astype(BF16), preferred_element_type=F32) * w_inter
            den = jnp.sum(sc, axis=-1, keepdims=True) + w_inter * jnp.sum(q * nm, axis=-1, keepdims=True)
            hid = num / jnp.maximum(jnp.abs(den), jnp.exp(-m_t))
            m_end = m_t[BLK - 1:BLK, :]
            b_end = bt[BLK - 1:BLK, :]
            kw = k * jnp.exp(b_end - bt + log_i - m_end)
            w_c = jnp.exp(b_end + m_prev - m_end)
            c_scr[h] = w_c * cm + jnp.dot(kw.T.astype(BF16), vb, preferred_element_type=F32)
            n_scr[h] = w_c * nm + jnp.sum(kw, axis=0, keepdims=True)
            m_scr[h] = m_end
            gate = jax.nn.sigmoid(z_ref[3 * n_heads + h, rows, :])
            o_ref[h, rows, :] = _head_norm_gate(hid, gn_ref[:, cols], gate).astype(o_ref.dtype)
        return carry

    lax.fori_loop(0, z_ref.shape[1] // BLK, chunk, 0)

    @pl.when(t == pl.num_programs(1) - 1)
    def _():
        c_ref[...] = c_scr[...]
        n_ref[...] = n_scr[...]
        m_ref[...] = m_scr[...]


def _mlstm_step_kernel(z_ref, zg_ref, bias_ref, c_ref, n_ref, m_ref, gn_ref,
                       o_ref, co_ref, no_ref, mo_ref, k_scr, wc_scr, wk_scr, em_scr, *, n_heads):
    g = pl.program_id(0)
    n_rows = z_ref.shape[1]

    @pl.when(g == 0)
    def _():
        lanes = _iota((n_rows, LANES), 1)
        gates = zg_ref[...] + bias_ref[...]
        log_f = pltpu.roll(_log_sigmoid(gates), LANES - n_heads, 1)
        m_old = m_ref[...]
        m_new = jnp.maximum(log_f + m_old, gates)
        mo_ref[...] = m_new
        w_c = jnp.exp(log_f + m_old - m_new)
        w_k = jnp.exp(gates - m_new)
        e_m = jnp.exp(-m_new)
        for h in range(n_heads):
            sel = lanes == h
            wc_scr[h] = jnp.broadcast_to(_pick_lane(w_c, sel), (n_rows, LANES))
            wk_scr[h] = jnp.broadcast_to(_pick_lane(w_k, sel), (n_rows, LANES))
            em_scr[h] = jnp.broadcast_to(_pick_lane(e_m, sel), (n_rows, LANES))
            k_scr[h] = (z_ref[n_heads + h] * K_SCALE).T

    lane = _iota((HEAD_DIM, n_rows), 1)
    grp = pl.ds(pl.multiple_of(g * STEP_ROWS, STEP_ROWS), STEP_ROWS)
    q = [z_ref[h, grp, :].astype(BF16) for h in range(n_heads)]

    def body(j, outs):
        b = g * STEP_ROWS + j
        sel = lane == b
        one = pl.ds(b, 1)
        new = []
        for h in range(n_heads):
            w_c = wc_scr[h, one, :]
            w_k = wk_scr[h, one, :]
            c_new = w_c * c_ref[j, h] + (w_k * _pick_lane(k_scr[h], sel)) * z_ref[2 * n_heads + h, one, :]
            co_ref[j, h] = c_new
            n_new = w_c * n_ref[j, h:h + 1, :] + w_k * (z_ref[n_heads + h, one, :] * K_SCALE)
            no_ref[j, h:h + 1, :] = n_new
            num = jnp.dot(q[h], c_new.astype(BF16), preferred_element_type=F32)
            new.append(_put_row(outs[2 * h], j, num))
            new.append(_put_row(outs[2 * h + 1], j, n_new))
        return tuple(new)

    zero = jnp.zeros((STEP_ROWS, LANES), F32)
    outs = lax.fori_loop(0, STEP_ROWS, body, (zero,) * (2 * n_heads), unroll=STEP_UNROLL)
    for h in range(n_heads):
        num, n_new = outs[2 * h], outs[2 * h + 1]
        den = jnp.sum(z_ref[h, grp, :] * n_new, axis=-1, keepdims=True)
        hid = num / jnp.maximum(jnp.abs(den), em_scr[h, grp, :])
        gate = jax.nn.sigmoid(z_ref[3 * n_heads + h, grp, :])
        o_ref[h] = _head_norm_gate(hid, gn_ref[:, h * LANES:(h + 1) * LANES], gate).astype(o_ref.dtype)


def kernel(x_prompt, x_sample, state_hgrn, state_mlstm_c, state_mlstm_n, state_mlstm_m, state_ret, norm_ffn1, ffn1_w_gate, ffn1_w_up, ffn1_w_down, norm_mix, w_in, hgrn_lb_logits, mlstm_gate_bias, gn_hgrn, gn_mlstm, gn_ret, w_out, norm_ffn2, ffn2_w_gate, ffn2_w_up, ffn2_w_down, norm_final):
    bp, t_len, d = x_prompt.shape
    bs, dec_len, _ = x_sample.shape
    depth = w_in.shape[0]
    h_hgrn, h_mlstm, h_ret = state_hgrn.shape[2], state_mlstm_c.shape[2], state_ret.shape[2]
    w_hgrn, w_mlstm, w_ret = h_hgrn * HEAD_DIM, h_mlstm * HEAD_DIM, h_ret * HEAD_DIM
    d_ff = ffn1_w_gate.shape[2]
    n_prompt = bp * t_len
    m_rows = n_prompt + bs
    seq_tile = min(SEQ_TILE, t_len)
    n_seq = t_len // seq_tile
    assert dec_len == 1 and bs == LANES and n_prompt % bs == 0 and t_len % seq_tile == 0 and seq_tile % BLK == 0
    row_block = n_prompt // bs
    step_block = n_prompt // STEP_ROWS
    tm = m_rows // 8 if m_rows % 128 == 0 else m_rows
    assert m_rows % tm == 0 and tm % 16 == 0 and n_prompt % STACK_TILE == 0
    o_mlstm, o_ret, o_hgrn = 0, h_mlstm, h_mlstm + h_ret
    assert o_ret % h_ret == 0 and o_hgrn % h_hgrn == 0
    n_slabs = h_hgrn + h_mlstm + h_ret
    o_order = tuple(range(o_hgrn, n_slabs)) + tuple(range(o_mlstm, o_ret)) + tuple(range(o_ret, o_hgrn))

    gate0 = 4 * w_hgrn + 4 * w_mlstm
    ret0 = gate0 + 2 * h_mlstm
    w_in_ret = jnp.concatenate(
        [w_in[:, :, ret0:], jnp.pad(w_in[:, :, gate0:ret0], ((0, 0), (0, 0), (0, LANES - 2 * h_mlstm)))],
        axis=2).astype(BF16)
    w_in_hm = w_in[:, :, :gate0].astype(BF16)
    gate_slab = 4 * h_ret
    bias_pad = jnp.pad(mlstm_gate_bias, ((0, 0), (0, LANES - 2 * h_mlstm))).reshape(depth, 1, LANES)
    m_pad = jnp.pad(state_mlstm_m, ((0, 0), (0, 0), (0, LANES - h_mlstm)))
    gn_h = gn_hgrn.reshape(depth, 1, w_hgrn)
    gn_m = gn_mlstm.reshape(depth, 1, w_mlstm)
    gn_r = gn_ret.reshape(depth, 1, w_ret)
    log_gamma = jnp.log1p(-jnp.exp2(-5.0 - jnp.arange(h_ret, dtype=F32)))
    cos_p, sin_p = _rope_tables(t_len, min(ROPE_TILE, t_len), 0.0, 1.0)
    cos_s, sin_s = _rope_tables(8, 8, PAST_LEN, 0.0)

    smem = pl.BlockSpec(memory_space=pltpu.SMEM)
    o3_shape = (n_slabs, m_rows, LANES)
    head_sq = (HEAD_DIM, HEAD_DIM)

    def z_prompt(n_blocks):
        return pl.BlockSpec((n_blocks, seq_tile, LANES), lambda b, t: (0, b * n_seq + t, 0))

    def o_prompt(heads, slab0):
        return pl.BlockSpec((heads, seq_tile, LANES), lambda b, t: (slab0 // heads, b * n_seq + t, 0))

    def o_step(heads, slab0):
        return pl.BlockSpec((heads, STEP_ROWS, LANES), lambda g: (slab0 // heads, step_block + g, 0))

    x, n = _stack_rows(x_prompt.reshape(n_prompt, d), x_sample.reshape(bs, d), norm_ffn1[0])
    new_h_p, new_c_p, new_n_p, new_m_p, new_r_p = [], [], [], [], []
    new_n_s, new_m_s = [], []
    st_h = st_c = st_r = None

    def ffn(x, n, ssq, w_gate, w_up, w_down, layer, next_norm):
        mid = _matmul(n, [w_gate, w_up], layer, mode="swiglu", tm=2 * tm, tn=512, n_out=d_ff, row_ssq=ssq)
        down = dict(tm=tm // 2, tn=1024, n_out=d, resid=x, weight_buffers=1)
        if next_norm is None:
            return _matmul(mid, [w_down], layer, mode="resid", **down), None, None
        return _matmul(mid, [w_down], layer, mode="resid_gain", norm=next_norm, **down)

    ssq = None
    for layer in range(depth):
        x, n, ssq = ffn(x, n, ssq, ffn1_w_gate, ffn1_w_up, ffn1_w_down, layer, (norm_mix, layer))

        cols = dict(mode="cols", tm=2 * tm, row_ssq=ssq)
        z_h = _matmul(n, [w_in_hm], layer, tn=1024, n_out=4 * w_hgrn, **cols)
        z_m = _matmul(n, [w_in_hm], layer, tn=1024, n_out=4 * w_mlstm, col_off=4 * w_hgrn // 1024, **cols)
        z_r = _matmul(n, [w_in_ret], layer, tn=640, n_out=4 * w_ret + LANES, **cols)

        def gain2(width):
            return pl.BlockSpec((None, 1, width), lambda b, t: (layer, 0, 0))

        def gain1(width):
            return pl.BlockSpec((None, 1, width), lambda g: (layer, 0, 0))

        o3, s_hp = _mixer_call(
            functools.partial(_hgrn_prompt_kernel, layer=layer, n_heads=h_hgrn), "hgrn_prompt", (bp, n_seq),
            [z_prompt(4 * h_hgrn), pl.BlockSpec(hgrn_lb_logits.shape, lambda b, t: (0, 0)), gain2(w_hgrn)],
            [z_h, hgrn_lb_logits, gn_h],
            [(o3_shape, BF16, o_prompt(h_hgrn, o_hgrn), None),
             ((bp, h_hgrn) + head_sq, F32, pl.BlockSpec((None, h_hgrn) + head_sq, lambda b, t: (b, 0, 0, 0)), None)],
            [pltpu.VMEM((h_hgrn,) + head_sq, F32)] + [pltpu.VMEM((h_hgrn, BLK, LANES), F32) for _ in range(4)],
            ("arbitrary", "arbitrary"))

        vec = ((bp, h_mlstm, 1, LANES), F32, pl.BlockSpec((None, h_mlstm, 1, LANES), lambda b, t: (b, 0, 0, 0)), None)
        o3, c_p, n_p, m_p = _mixer_call(
            functools.partial(_mlstm_prompt_kernel, n_heads=h_mlstm), "mlstm_prompt", (bp, n_seq),
            [z_prompt(4 * h_mlstm),
             pl.BlockSpec((None, seq_tile, LANES), lambda b, t: (gate_slab, b * n_seq + t, 0)),
             pl.BlockSpec((None, 1, LANES), lambda b, t: (layer, 0, 0)), gain2(w_mlstm)],
            [z_m, z_r, bias_pad, gn_m],
            [(o3_shape, BF16, o_prompt(h_mlstm, o_mlstm), o3),
             ((bp, h_mlstm) + head_sq, F32, pl.BlockSpec((None, h_mlstm) + head_sq, lambda b, t: (b, 0, 0, 0)), None),
             vec, vec],
            [pltpu.VMEM((h_mlstm,) + head_sq, F32), pltpu.VMEM((h_mlstm, 1, LANES), F32),
             pltpu.VMEM((h_mlstm, 1, LANES), F32)], ("arbitrary", "arbitrary"))

        table = pl.BlockSpec((t_len, LANES), lambda b, t: (0, 0))
        o3, s_rp = _mixer_call(
            functools.partial(_ret_prompt_kernel, n_heads=h_ret), "ret_prompt", (bp, n_seq),
            [z_prompt(4 * h_ret), table, table, smem, gain2(w_ret)],
            [z_r, cos_p, sin_p, log_gamma, gn_r],
            [(o3_shape, BF16, o_prompt(h_ret, o_ret), o3),
             ((bp, h_ret) + head_sq, F32, pl.BlockSpec((None, h_ret) + head_sq, lambda b, t: (b, 0, 0, 0)), None)],
            [pltpu.VMEM((h_ret,) + head_sq, F32), pltpu.VMEM((h_ret, 3, BLK, LANES), F32)],
            ("arbitrary", "arbitrary"))

        def z_step(n_blocks):
            return pl.BlockSpec((n_blocks, bs, LANES), lambda g: (0, row_block, 0))

        def state_in(state):
            tail = state.shape[2:]
            zeros = (0,) * len(tail)
            return pl.BlockSpec((None, STEP_ROWS) + tail, lambda g: (layer, g) + zeros)

        tposed = pltpu.VMEM((h_hgrn, HEAD_DIM, bs), F32)
        o3, st_h = _mixer_call(
            functools.partial(_hgrn_step_kernel, layer=layer, n_heads=h_hgrn), "hgrn_step", (bs // STEP_ROWS,),
            [z_step(4 * h_hgrn), state_in(state_hgrn), pl.BlockSpec(hgrn_lb_logits.shape, lambda g: (0, 0)), gain1(w_hgrn)],
            [z_h, state_hgrn, hgrn_lb_logits, gn_h],
            [(o3_shape, BF16, o_step(h_hgrn, o_hgrn), o3), _state_out(state_hgrn, layer, st_h)],
            [tposed, tposed], ("arbitrary",))

        tposed = pltpu.VMEM((h_mlstm, HEAD_DIM, bs), F32)
        rowrep = pltpu.VMEM((h_mlstm, bs, LANES), F32)
        tile = pl.BlockSpec((bs, LANES), lambda g: (0, 0))
        o3, st_c, n_s, m_s = _mixer_call(
            functools.partial(_mlstm_step_kernel, n_heads=h_mlstm), "mlstm_step", (bs // STEP_ROWS,),
            [z_step(4 * h_mlstm), pl.BlockSpec((None, bs, LANES), lambda g: (gate_slab, row_block, 0)),
             pl.BlockSpec((None, 1, LANES), lambda g: (layer, 0, 0)),
             state_in(state_mlstm_c), state_in(state_mlstm_n), tile, gain1(w_mlstm)],
            [z_m, z_r, bias_pad, state_mlstm_c, state_mlstm_n, m_pad[layer], gn_m],
            [(o3_shape, BF16, o_step(h_mlstm, o_mlstm), o3), _state_out(state_mlstm_c, layer, st_c),
             (state_mlstm_n.shape[1:], F32, pl.BlockSpec((STEP_ROWS, h_mlstm, HEAD_DIM), lambda g: (g, 0, 0)), None),
             ((bs, LANES), F32, tile, None)],
            [tposed, rowrep, rowrep, rowrep], ("arbitrary",))

        tposed = pltpu.VMEM((h_ret, HEAD_DIM, bs), F32)
        table = pl.BlockSpec(cos_s.shape, lambda g: (0, 0))
        o3, st_r = _mixer_call(
            functools.partial(_ret_step_kernel, n_heads=h_ret), "ret_step", (bs // STEP_ROWS,),
            [z_step(4 * h_ret), state_in(state_ret), table, table, smem, gain1(w_ret)],
            [z_r, state_ret, cos_s, sin_s, log_gamma, gn_r],
            [(o3_shape, BF16, o_step(h_ret, o_ret), o3), _state_out(state_ret, layer, st_r)],
            [tposed], ("arbitrary",))

        new_h_p.append(s_hp)
        new_c_p.append(c_p)
        new_n_p.append(n_p[:, :, 0, :])
        new_m_p.append(m_p[:, :, 0, 0])
        new_r_p.append(s_rp)
        new_n_s.append(n_s)
        new_m_s.append(m_s[:, :h_mlstm])

        x, n = _matmul(o3, [w_out], layer, mode="resid_norm", tm=tm // 2, tn=d, n_out=d, resid=x,
                       norm=(norm_ffn2, layer), weight_buffers=1, lhs_order=o_order)

        next_norm = (norm_ffn1, layer + 1) if layer + 1 < depth else None
        x, n, ssq = ffn(x, n, None, ffn2_w_gate, ffn2_w_up, ffn2_w_down, layer, next_norm)

    y_p, y_s = _rmsnorm_split(x, norm_final, n_prompt)
    return (
        y_p.reshape(bp, t_len, d), y_s.reshape(bs, dec_len, d),
        jnp.stack(new_h_p), jnp.stack(new_c_p), jnp.stack(new_n_p), jnp.stack(new_m_p), jnp.stack(new_r_p),
        st_h, st_c, jnp.stack(new_n_s), jnp.stack(new_m_s), st_r,
    )
```

```python
import functools

import jax
import jax.numpy as jnp
from jax import lax
from jax.experimental import pallas as pl
from jax.experimental.pallas import tpu as pltpu

F32 = jnp.float32
BF16 = jnp.bfloat16

HEAD_DIM = 128
LANES = 128
EPS = 1e-6
ROPE_BASE = 10000.0
K_SCALE = HEAD_DIM ** -0.5
PAST_LEN = 16384

BLK = 128
SUB = 32
SEQ_TILE = 512
STEP_ROWS = 16
STEP_UNROLL = 4
STACK_TILE = 1024
ROPE_TILE = 256
VMEM_LIMIT_BYTES = 56 * 1024 * 1024

DECAY_SPAN_MAX = 60.0

_TRANS_B = (((1,), (1,)), ((), ()))
_TRANS_A = (((0,), (0,)), ((), ()))


def _params(*semantics):
    return pltpu.CompilerParams(dimension_semantics=semantics, vmem_limit_bytes=VMEM_LIMIT_BYTES)


def _silu(x):
    return x * jax.nn.sigmoid(x)


def _softplus_neg_abs(x):
    return jnp.log(1.0 + jnp.exp(-jnp.abs(x)))


def _log_sigmoid(x):
    return jnp.minimum(x, 0.0) - _softplus_neg_abs(x)


def _iota(shape, dim):
    return lax.broadcasted_iota(jnp.int32, shape, dim)


def _seg_cumsum(x, seg):
    pos = _iota(x.shape, 0) & (seg - 1)
    shift = 1
    while shift < seg:
        x = x + jnp.where(pos >= shift, pltpu.roll(x, shift, 0), 0.0)
        shift *= 2
    return x


def _head_norm_gate(o, gain, gate):
    return o * lax.rsqrt(jnp.mean(o * o, axis=-1, keepdims=True) + EPS) * gain * gate


def _pick_lane(x, lane_mask):
    return jnp.sum(jnp.where(lane_mask, x, 0.0), axis=-1, keepdims=True)


def _put_row(tile, j, new):
    return jnp.where(_iota(tile.shape, 0) == j, new, tile)


def _stack_rows_kernel(a_ref, b_ref, g_ref, o_ref, n_ref, *, n_a):
    i = pl.program_id(0)

    def emit(x, rows):
        o_ref[rows, :] = x
        y = x * lax.rsqrt(jnp.mean(x * x, axis=-1, keepdims=True) + EPS)
        n_ref[rows, :] = (y * g_ref[...]).astype(n_ref.dtype)

    @pl.when(i < n_a)
    def _():
        emit(a_ref[...], slice(None))

    @pl.when(i == n_a)
    def _():
        emit(b_ref[...], slice(0, b_ref.shape[0]))


def _stack_rows(a, b, gain):
    (ma, d), mb = a.shape, b.shape[0]
    n_a = ma // STACK_TILE
    tile = pl.BlockSpec((STACK_TILE, d), lambda i: (i, 0))
    return pl.pallas_call(
        functools.partial(_stack_rows_kernel, n_a=n_a),
        out_shape=(jax.ShapeDtypeStruct((ma + mb, d), a.dtype), jax.ShapeDtypeStruct((ma + mb, d), BF16)),
        grid=(n_a + 1,),
        in_specs=[
            pl.BlockSpec((STACK_TILE, d), lambda i: (jnp.minimum(i, n_a - 1), 0)),
            pl.BlockSpec((mb, d), lambda i: (0, 0)),
            pl.BlockSpec((1, d), lambda i: (0, 0)),
        ],
        out_specs=(tile, tile),
        compiler_params=_params("arbitrary"),
        name="stack_rows",
    )(a, b, gain.reshape(1, d))


def _rmsnorm_split_kernel(x_ref, g_ref, a_ref, b_ref, *, n_a):
    i = pl.program_id(0)
    x = x_ref[...]
    y = x * lax.rsqrt(jnp.mean(x * x, axis=-1, keepdims=True) + EPS) * g_ref[...]

    @pl.when(i < n_a)
    def _():
        a_ref[...] = y

    @pl.when(i == n_a)
    def _():
        b_ref[...] = y[0:b_ref.shape[0], :]


def _rmsnorm_split(x, gain, rows_a):
    m, d = x.shape
    n_a = rows_a // STACK_TILE
    rows_b = m - rows_a
    return pl.pallas_call(
        functools.partial(_rmsnorm_split_kernel, n_a=n_a),
        out_shape=(jax.ShapeDtypeStruct((rows_a, d), F32), jax.ShapeDtypeStruct((rows_b, d), F32)),
        grid=(n_a + 1,),
        in_specs=[
            pl.BlockSpec((STACK_TILE, d), lambda i: (i, 0)),
            pl.BlockSpec((1, d), lambda i: (0, 0)),
        ],
        out_specs=(
            pl.BlockSpec((STACK_TILE, d), lambda i: (jnp.minimum(i, n_a - 1), 0)),
            pl.BlockSpec((rows_b, d), lambda i: (0, 0)),
        ),
        compiler_params=_params("arbitrary"),
        name="rmsnorm_split",
    )(x, gain.reshape(1, d))


def _matmul_kernel(*refs, n_w, mode, lhs_order, scaled):
    lhs_ref = refs[0]
    w_refs = refs[1:1 + n_w]
    rest = refs[1 + n_w:]

    if lhs_order is None:
        lhs = lhs_ref[...]
    else:
        lhs = jnp.concatenate([lhs_ref[c] for c in lhs_order], axis=1)
    acc = [jnp.dot(lhs, w_ref[...].astype(BF16), preferred_element_type=F32) for w_ref in w_refs]
    if scaled:
        ssq_ref, rest = rest[0], rest[1:]
        total = ssq_ref[0]
        for part in range(1, ssq_ref.shape[0]):
            total = total + ssq_ref[part]
        r = lax.rsqrt(total[:, 0:1] * (1.0 / lhs.shape[1]) + EPS)
        acc = [a * r for a in acc]
    if mode == "swiglu":
        (o_ref,) = rest
        gate, up = acc
        o_ref[...] = (0.5 * _silu(gate) * up).astype(o_ref.dtype)
    elif mode == "resid":
        x_ref, o_ref = rest
        o_ref[...] = x_ref[...] + acc[0]
    elif mode == "resid_gain":
        x_ref, g_ref, o_ref, xg_ref, ssq_ref = rest
        x = x_ref[...] + acc[0]
        o_ref[...] = x
        xg_ref[...] = (x * g_ref[...]).astype(xg_ref.dtype)
        ssq_ref[...] = jnp.broadcast_to(jnp.sum(x * x, axis=-1, keepdims=True), ssq_ref.shape)
    elif mode == "resid_norm":
        x_ref, g_ref, o_ref, n_ref = rest
        x = x_ref[...] + acc[0]
        o_ref[...] = x
        y = x * lax.rsqrt(jnp.mean(x * x, axis=-1, keepdims=True) + EPS)
        n_ref[...] = (y * g_ref[...]).astype(n_ref.dtype)
    else:
        (o_ref,) = rest
        for c in range(o_ref.shape[0]):
            o_ref[c] = acc[0][:, c * LANES:(c + 1) * LANES]


def _matmul(lhs, weights, layer, *, mode, tm, tn, n_out, col_off=0, resid=None, norm=None, row_ssq=None,
            weight_buffers=2, lhs_order=None):
    if lhs_order is None:
        m, k = lhs.shape
        lhs_spec = pl.BlockSpec((tm, k), lambda j, i: (i, 0))
    else:
        m, k = lhs.shape[1], lhs.shape[0] * LANES
        lhs_spec = pl.BlockSpec((lhs.shape[0], tm, LANES), lambda j, i: (0, i, 0))
    n_w = len(weights)
    n_tiles = pl.cdiv(n_out, tn)
    grid = (n_tiles, m // tm)
    in_specs = [lhs_spec]
    in_specs += [pl.BlockSpec((None, k, tn), lambda j, i: (layer, 0, j + col_off),
                              pipeline_mode=pl.Buffered(weight_buffers)) for _ in weights]
    args = [lhs, *weights]
    if row_ssq is not None:
        in_specs.append(pl.BlockSpec((row_ssq.shape[0], tm, LANES), lambda j, i: (0, i, 0)))
        args.append(row_ssq)
    tile = pl.BlockSpec((tm, tn), lambda j, i: (i, j))
    if mode in ("resid", "resid_gain", "resid_norm"):
        in_specs.append(tile)
        args.append(resid)
    if mode in ("resid_gain", "resid_norm"):
        gains, norm_layer = norm
        in_specs.append(pl.BlockSpec((None, 1, tn), lambda j, i: (norm_layer, 0, j)))
        args.append(gains.reshape(-1, 1, n_out))
    if mode == "resid_gain":
        out_shape = (jax.ShapeDtypeStruct((m, n_out), F32), jax.ShapeDtypeStruct((m, n_out), BF16),
                     jax.ShapeDtypeStruct((n_tiles, m, LANES), F32))
        out_spec = (tile, tile, pl.BlockSpec((None, tm, LANES), lambda j, i: (j, i, 0)))
    elif mode == "resid_norm":
        assert tn == n_out
        out_shape = (jax.ShapeDtypeStruct((m, n_out), F32), jax.ShapeDtypeStruct((m, n_out), BF16))
        out_spec = (tile, tile)
    elif mode == "cols":
        out_shape = jax.ShapeDtypeStruct((n_out // LANES, m, LANES), F32)
        out_spec = pl.BlockSpec((tn // LANES, tm, LANES), lambda j, i: (j, i, 0))
    else:
        out_shape = jax.ShapeDtypeStruct((m, n_out), BF16 if mode == "swiglu" else F32)
        out_spec = tile
    return pl.pallas_call(
        functools.partial(_matmul_kernel, n_w=n_w, mode=mode, lhs_order=lhs_order, scaled=row_ssq is not None),
        out_shape=out_shape,
        grid=grid,
        in_specs=in_specs,
        out_specs=out_spec,
        compiler_params=_params("arbitrary", "arbitrary"),
        name="matmul_" + mode,
    )(*args)


def _rope_kernel(inv_ref, cos_ref, sin_ref, *, pos0, step):
    shape = cos_ref.shape
    row = _iota(shape, 0) + pl.program_id(0) * shape[0]
    ang = (pos0 + step * row.astype(F32)) * inv_ref[...]
    sin = jnp.sin(ang)
    cos_ref[...] = jnp.cos(ang)
    sin_ref[...] = jnp.where(_iota(shape, 1) < HEAD_DIM // 2, -sin, sin)


def _rope_tables(rows, tile, pos0, step):
    inv = ROPE_BASE ** (-jnp.arange(0, HEAD_DIM, 2, dtype=F32) / HEAD_DIM)
    inv = jnp.concatenate([inv, inv]).reshape(1, HEAD_DIM)
    out = jax.ShapeDtypeStruct((rows, HEAD_DIM), F32)
    spec = pl.BlockSpec((tile, HEAD_DIM), lambda i: (i, 0))
    return pl.pallas_call(
        functools.partial(_rope_kernel, pos0=float(pos0), step=float(step)),
        out_shape=(out, out),
        grid=(rows // tile,),
        in_specs=[pl.BlockSpec((1, HEAD_DIM), lambda i: (0, 0))],
        out_specs=(spec, spec),
        compiler_params=_params("arbitrary"),
        name="rope_tables",
    )(inv)


def _rope(x, cos, sin):
    return x * cos + pltpu.roll(x, HEAD_DIM // 2, 1) * sin


def _mixer_call(kernel_fn, name, grid, in_specs, args, outs, scratch, semantics):
    in_specs, args, aliases = list(in_specs), list(args), {}
    for k, (_, _, _, prev) in enumerate(outs):
        if prev is not None:
            aliases[len(args)] = k
            in_specs.append(pl.BlockSpec(memory_space=pl.ANY))
            args.append(prev)
    n_alias = len(aliases)

    def body(*refs):
        n_in = len(args) - n_alias
        kernel_fn(*refs[:n_in], *refs[n_in + n_alias:])

    return pl.pallas_call(
        body,
        out_shape=tuple(jax.ShapeDtypeStruct(s, dt) for s, dt, _, _ in outs),
        grid=grid,
        in_specs=in_specs,
        out_specs=tuple(spec for _, _, spec, _ in outs),
        scratch_shapes=scratch,
        input_output_aliases=aliases,
        compiler_params=_params(*semantics),
        name=name,
    )(*args)


def _state_out(state, layer, prev):
    tail = state.shape[2:]
    zeros = (0,) * len(tail)
    spec = pl.BlockSpec((None, STEP_ROWS) + tail, lambda g: (layer, g) + zeros)
    return (state.shape, F32, spec, prev)


def _hgrn_lower_bound(logits, layer):
    mx = jnp.max(logits, axis=0, keepdims=True)
    e = jnp.exp(logits - mx)
    p = e / jnp.sum(e, axis=0, keepdims=True)
    if layer == 0:
        return jnp.zeros_like(mx)
    return jnp.sum(p[1:layer + 1], axis=0, keepdims=True)


def _hgrn_gates(fz, lb):
    log_lb = jnp.log(lb)
    y = jnp.log1p(-lb) + _log_sigmoid(fz)
    log_f = jnp.maximum(log_lb, y) + _softplus_neg_abs(log_lb - y)
    key = (1.0 - lb) / (1.0 + jnp.exp(fz))
    return log_f, key


def _hgrn_prompt_kernel(z_ref, lbl_ref, gn_ref, o_ref, s_ref, st_ref, q_scr, k_scr, b_scr, a_scr, *, layer, n_heads):
    t = pl.program_id(1)
    n_sub = BLK // SUB
    lb = _hgrn_lower_bound(lbl_ref[...], layer)
    row = _iota((BLK, LANES), 0)
    col = _iota((BLK, LANES), 1)
    pos = row & (SUB - 1)
    seg0 = row - pos

    @pl.when(t == 0)
    def _():
        st_ref[...] = jnp.zeros_like(st_ref)

    def seg_row(x, s):
        x3 = x.reshape(n_sub, SUB, LANES)
        return jnp.broadcast_to(x3[:, s:s + 1, :], x3.shape).reshape(BLK, LANES)

    def block(i, carry):
        rows = pl.ds(pl.multiple_of(i * BLK, BLK), BLK)
        span = jnp.zeros((BLK, LANES), F32)
        for h in range(n_heads):
            log_f, k = _hgrn_gates(z_ref[n_heads + h, rows, :], lb[:, h * LANES:(h + 1) * LANES])
            b = _seg_cumsum(log_f, SUB)
            q_scr[h] = _silu(z_ref[h, rows, :])
            k_scr[h] = k
            b_scr[h] = b
            span = jnp.maximum(span, -seg_row(b, SUB - 1))
        mild = jnp.max(span) <= DECAY_SPAN_MAX

        @pl.when(mild)
        def _():
            for h in range(n_heads):
                b = b_scr[h]
                b_end = seg_row(b, SUB - 1)
                q_end = (q_scr[h] * jnp.exp(b - b_end)).astype(BF16)
                k_end = (k_scr[h] * jnp.exp(b_end - b)).astype(BF16)
                a = lax.dot_general(q_end, k_end, _TRANS_B, preferred_element_type=F32)
                a_scr[h] = jnp.where((col >= seg0) & (col <= row), a, 0.0)

        @pl.when(jnp.logical_not(mild))
        def _():
            for h in range(n_heads):
                q, k, b = q_scr[h], k_scr[h], b_scr[h]
                a = jnp.zeros((BLK, LANES), F32)
                for s in range(SUB):
                    e = jnp.exp(jnp.where(pos >= s, b - seg_row(b, s), -jnp.inf))
                    c = jnp.sum(q * seg_row(k, s) * e, axis=-1, keepdims=True)
                    a = jnp.where(col == seg0 + s, c, a)
                a_scr[h] = a

        for h in range(n_heads):
            cols = slice(h * LANES, (h + 1) * LANES)
            b = b_scr[h]
            b_end = seg_row(b, SUB - 1)
            vb = z_ref[2 * n_heads + h, rows, :].astype(BF16)
            q_in = (q_scr[h] * jnp.exp(b)).astype(BF16)
            k_out = (k_scr[h] * jnp.exp(b_end - b)).astype(BF16)
            lam = jnp.exp(b_end)
            o = jnp.dot(a_scr[h].astype(BF16), vb, preferred_element_type=F32)
            subs = [slice(n * SUB, (n + 1) * SUB) for n in range(n_sub)]
            ups = [lax.dot_general(vb[sl], k_out[sl], _TRANS_A, preferred_element_type=F32) for sl in subs]
            states = [st_ref[h]]
            for n in range(n_sub):
                states.append(states[n] * lam[n * SUB:n * SUB + 1, :] + ups[n])
            st_ref[h] = states[n_sub]
            parts = [lax.dot_general(q_in[sl], states[n].astype(BF16), _TRANS_B, preferred_element_type=F32)
                     for n, sl in enumerate(subs)]
            o = o + jnp.concatenate(parts, axis=0)
            gate = _silu(z_ref[3 * n_heads + h, rows, :])
            o_ref[h, rows, :] = _head_norm_gate(o, gn_ref[:, cols], gate).astype(o_ref.dtype)
        return carry

    lax.fori_loop(0, z_ref.shape[1] // BLK, block, 0)

    @pl.when(t == pl.num_programs(1) - 1)
    def _():
        for h in range(n_heads):
            s_ref[h] = st_ref[h].T


def _hgrn_step_kernel(z_ref, s_ref, lbl_ref, gn_ref, o_ref, so_ref, f_scr, k_scr, *, layer, n_heads):
    g = pl.program_id(0)
    n_rows = z_ref.shape[1]

    @pl.when(g == 0)
    def _():
        lb = _hgrn_lower_bound(lbl_ref[...], layer)
        for h in range(n_heads):
            log_f, k = _hgrn_gates(z_ref[n_heads + h], lb[:, h * LANES:(h + 1) * LANES])
            f_scr[h] = jnp.exp(log_f).T
            k_scr[h] = k.T

    lane = _iota((HEAD_DIM, n_rows), 1)
    grp = pl.ds(pl.multiple_of(g * STEP_ROWS, STEP_ROWS), STEP_ROWS)
    q = [_silu(z_ref[h, grp, :]).astype(BF16) for h in range(n_heads)]

    def body(j, outs):
        b = g * STEP_ROWS + j
        sel = lane == b
        one = pl.ds(b, 1)
        new = []
        for h in range(n_heads):
            s_new = _pick_lane(f_scr[h], sel) * s_ref[j, h] + _pick_lane(k_scr[h], sel) * z_ref[2 * n_heads + h, one, :]
            so_ref[j, h] = s_new
            new.append(_put_row(outs[h], j, jnp.dot(q[h], s_new.astype(BF16), preferred_element_type=F32)))
        return tuple(new)

    zero = jnp.zeros((STEP_ROWS, LANES), F32)
    outs = lax.fori_loop(0, STEP_ROWS, body, (zero,) * n_heads, unroll=STEP_UNROLL)
    for h in range(n_heads):
        gate = _silu(z_ref[3 * n_heads + h, grp, :])
        o_ref[h] = _head_norm_gate(outs[h], gn_ref[:, h * LANES:(h + 1) * LANES], gate).astype(o_ref.dtype)


def _ret_prompt_kernel(z_ref, cos_ref, sin_ref, lg_ref, gn_ref, o_ref, s_ref, st_ref, dec_ref, *, n_heads):
    t = pl.program_id(1)

    @pl.when((pl.program_id(0) == 0) & (t == 0))
    def _():
        row = _iota((BLK, LANES), 0).astype(F32)
        diff = row - _iota((BLK, LANES), 1).astype(F32)
        for h in range(n_heads):
            lg = lg_ref[h]
            dec_ref[h, 0] = jnp.where(diff >= 0.0, jnp.exp(diff * lg), 0.0)
            dec_ref[h, 1] = jnp.exp((row + 1.0) * lg)
            dec_ref[h, 2] = jnp.exp((BLK - 1.0 - row) * lg)

    @pl.when(t == 0)
    def _():
        st_ref[...] = jnp.zeros_like(st_ref)

    def chunk(i, carry):
        rows = pl.ds(pl.multiple_of(i * BLK, BLK), BLK)
        tab = pl.ds(pl.multiple_of(t * z_ref.shape[1] + i * BLK, BLK), BLK)
        cos = cos_ref[tab, :]
        sin = sin_ref[tab, :]
        for h in range(n_heads):
            cols = slice(h * LANES, (h + 1) * LANES)
            q = _rope(z_ref[h, rows, :], cos, sin)
            k = _rope(z_ref[n_heads + h, rows, :], cos, sin) * K_SCALE
            vb = z_ref[2 * n_heads + h, rows, :].astype(BF16)
            a = lax.dot_general(q.astype(BF16), k.astype(BF16), _TRANS_B, preferred_element_type=F32) * dec_ref[h, 0]
            st = st_ref[h]
            o = jnp.dot(a.astype(BF16), vb, preferred_element_type=F32)
            o = o + jnp.dot((q * dec_ref[h, 1]).astype(BF16), st.astype(BF16), preferred_element_type=F32)
            s_dec = jnp.exp(jnp.full((1, LANES), BLK, F32) * lg_ref[h])
            st_ref[h] = s_dec * st + jnp.dot((k * dec_ref[h, 2]).T.astype(BF16), vb, preferred_element_type=F32)
            gate = _silu(z_ref[3 * n_heads + h, rows, :])
            o_ref[h, rows, :] = _head_norm_gate(o, gn_ref[:, cols], gate).astype(o_ref.dtype)
        return carry

    lax.fori_loop(0, z_ref.shape[1] // BLK, chunk, 0)

    @pl.when(t == pl.num_programs(1) - 1)
    def _():
        s_ref[...] = st_ref[...]


def _ret_step_kernel(z_ref, s_ref, cos_ref, sin_ref, lg_ref, gn_ref, o_ref, so_ref, k_scr, *, n_heads):
    g = pl.program_id(0)
    n_rows = z_ref.shape[1]
    cos = cos_ref[0:1, :]
    sin = sin_ref[0:1, :]

    @pl.when(g == 0)
    def _():
        for h in range(n_heads):
            k_scr[h] = (_rope(z_ref[n_heads + h], cos, sin) * K_SCALE).T

    lane = _iota((HEAD_DIM, n_rows), 1)
    grp = pl.ds(pl.multiple_of(g * STEP_ROWS, STEP_ROWS), STEP_ROWS)
    q = [_rope(z_ref[h, grp, :], cos, sin).astype(BF16) for h in range(n_heads)]

    def body(j, outs):
        b = g * STEP_ROWS + j
        sel = lane == b
        one = pl.ds(b, 1)
        new = []
        for h in range(n_heads):
            gamma = jnp.exp(jnp.full((1, LANES), lg_ref[h], F32))
            s_new = gamma * s_ref[j, h] + _pick_lane(k_scr[h], sel) * z_ref[2 * n_heads + h, one, :]
            so_ref[j, h] = s_new
            new.append(_put_row(outs[h], j, jnp.dot(q[h], s_new.astype(BF16), preferred_element_type=F32)))
        return tuple(new)

    zero = jnp.zeros((STEP_ROWS, LANES), F32)
    outs = lax.fori_loop(0, STEP_ROWS, body, (zero,) * n_heads, unroll=STEP_UNROLL)
    for h in range(n_heads):
        gate = _silu(z_ref[3 * n_heads + h, grp, :])
        o_ref[h] = _head_norm_gate(outs[h], gn_ref[:, h * LANES:(h + 1) * LANES], gate).astype(o_ref.dtype)


def _mlstm_prompt_kernel(z_ref, zg_ref, bias_ref, gn_ref, o_ref, c_ref, n_ref, m_ref, cn_scr, m_scr, *, n_heads):
    t = pl.program_id(1)
    lane = _iota((BLK, LANES), 1)
    causal = lane <= _iota((BLK, LANES), 0)
    ones = jnp.ones((BLK, LANES), BF16)

    def twice(x):
        return jnp.concatenate([x, x], axis=1)

    @pl.when(t == 0)
    def _():
        cn_scr[...] = jnp.zeros_like(cn_scr)
        m_scr[...] = jnp.zeros_like(m_scr)

    def chunk(i, carry):
        rows = pl.ds(pl.multiple_of(i * BLK, BLK), BLK)
        gates = zg_ref[rows, :] + bias_ref[...]
        log_f_all = pltpu.roll(_log_sigmoid(gates), LANES - n_heads, 1)
        bt_all = _seg_cumsum(log_f_all, BLK)
        w_t = (gates - bt_all).T
        for h in range(n_heads):
            cols = slice(h * LANES, (h + 1) * LANES)
            bt = _pick_lane(bt_all, lane == h)
            w_row = w_t[h:h + 1, :]
            log_d = jnp.where(causal, bt + w_row, -jnp.inf)
            m_prev = m_scr[h]
            log_inter = bt + m_prev
            m_t = jnp.maximum(log_inter, jnp.max(log_d, axis=-1, keepdims=True))
            d_mat = jnp.exp(log_d - m_t)
            w_inter = jnp.exp(log_inter - m_t)
            k = z_ref[n_heads + h, rows, :] * K_SCALE
            qb = z_ref[h, rows, :].astype(BF16)
            v_ones = jnp.concatenate([z_ref[2 * n_heads + h, rows, :].astype(BF16), ones], axis=1)
            sc = lax.dot_general(qb, k.astype(BF16), _TRANS_B, preferred_element_type=F32) * d_mat
            cn = cn_scr[h]
            both = jnp.dot(sc.astype(BF16), v_ones, preferred_element_type=F32)
            both = both + jnp.dot(qb, cn.astype(BF16), preferred_element_type=F32) * twice(w_inter)
            num, den = both[:, :LANES], both[:, LANES:]
            hid = num / jnp.maximum(jnp.abs(den), jnp.exp(-m_t))
            m_end = m_t[BLK - 1:BLK, :]
            b_end = bt[BLK - 1:BLK, :]
            kw_t = k.T * jnp.exp(b_end + w_row - m_end)
            w_c = jnp.exp(b_end + m_prev - m_end)
            cn_scr[h] = twice(w_c) * cn + jnp.dot(kw_t.astype(BF16), v_ones, preferred_element_type=F32)
            m_scr[h] = m_end
            gate = jax.nn.sigmoid(z_ref[3 * n_heads + h, rows, :])
            o_ref[h, rows, :] = _head_norm_gate(hid, gn_ref[:, cols], gate).astype(o_ref.dtype)
        return carry

    lax.fori_loop(0, z_ref.shape[1] // BLK, chunk, 0)

    @pl.when(t == pl.num_programs(1) - 1)
    def _():
        for h in range(n_heads):
            c_ref[h] = cn_scr[h, :, :LANES]
            n_ref[h] = cn_scr[h, :, LANES:].T[0:1, :]
        m_ref[...] = m_scr[...]


def _mlstm_step_kernel(z_ref, zg_ref, bias_ref, c_ref, n_ref, m_ref, gn_ref,
                       o_ref, co_ref, no_ref, mo_ref, k_scr, wc_scr, wk_scr, em_scr, *, n_heads):
    g = pl.program_id(0)
    n_rows = z_ref.shape[1]

    @pl.when(g == 0)
    def _():
        lanes = _iota((n_rows, LANES), 1)
        gates = zg_ref[...] + bias_ref[...]
        log_f = pltpu.roll(_log_sigmoid(gates), LANES - n_heads, 1)
        m_old = m_ref[...]
        m_new = jnp.maximum(log_f + m_old, gates)
        mo_ref[...] = m_new
        w_c = jnp.exp(log_f + m_old - m_new)
        w_k = jnp.exp(gates - m_new)
        e_m = jnp.exp(-m_new)
        for h in range(n_heads):
            sel = lanes == h
            wc_scr[h] = jnp.broadcast_to(_pick_lane(w_c, sel), (n_rows, LANES))
            wk_scr[h] = jnp.broadcast_to(_pick_lane(w_k, sel), (n_rows, LANES))
            em_scr[h] = jnp.broadcast_to(_pick_lane(e_m, sel), (n_rows, LANES))
            k_scr[h] = (z_ref[n_heads + h] * K_SCALE).T

    lane = _iota((HEAD_DIM, n_rows), 1)
    grp = pl.ds(pl.multiple_of(g * STEP_ROWS, STEP_ROWS), STEP_ROWS)
    q = [z_ref[h, grp, :].astype(BF16) for h in range(n_heads)]

    def body(j, outs):
        b = g * STEP_ROWS + j
        sel = lane == b
        one = pl.ds(b, 1)
        new = []
        for h in range(n_heads):
            w_c = wc_scr[h, one, :]
            w_k = wk_scr[h, one, :]
            c_new = w_c * c_ref[j, h] + (w_k * _pick_lane(k_scr[h], sel)) * z_ref[2 * n_heads + h, one, :]
            co_ref[j, h] = c_new
            n_new = w_c * n_ref[j, h:h + 1, :] + w_k * (z_ref[n_heads + h, one, :] * K_SCALE)
            no_ref[j, h:h + 1, :] = n_new
            num = jnp.dot(q[h], c_new.astype(BF16), preferred_element_type=F32)
            new.append(_put_row(outs[2 * h], j, num))
            new.append(_put_row(outs[2 * h + 1], j, n_new))
        return tuple(new)

    zero = jnp.zeros((STEP_ROWS, LANES), F32)
    outs = lax.fori_loop(0, STEP_ROWS, body, (zero,) * (2 * n_heads), unroll=STEP_UNROLL)
    for h in range(n_heads):
        num, n_new = outs[2 * h], outs[2 * h + 1]
        den = jnp.sum(z_ref[h, grp, :] * n_new, axis=-1, keepdims=True)
        hid = num / jnp.maximum(jnp.abs(den), em_scr[h, grp, :])
        gate = jax.nn.sigmoid(z_ref[3 * n_heads + h, grp, :])
        o_ref[h] = _head_norm_gate(hid, gn_ref[:, h * LANES:(h + 1) * LANES], gate).astype(o_ref.dtype)


def kernel(x_prompt, x_sample, state_hgrn, state_mlstm_c, state_mlstm_n, state_mlstm_m, state_ret, norm_ffn1, ffn1_w_gate, ffn1_w_up, ffn1_w_down, norm_mix, w_in, hgrn_lb_logits, mlstm_gate_bias, gn_hgrn, gn_mlstm, gn_ret, w_out, norm_ffn2, ffn2_w_gate, ffn2_w_up, ffn2_w_down, norm_final):
    bp, t_len, d = x_prompt.shape
    bs, dec_len, _ = x_sample.shape
    depth = w_in.shape[0]
    h_hgrn, h_mlstm, h_ret = state_hgrn.shape[2], state_mlstm_c.shape[2], state_ret.shape[2]
    w_hgrn, w_mlstm, w_ret = h_hgrn * HEAD_DIM, h_mlstm * HEAD_DIM, h_ret * HEAD_DIM
    d_ff = ffn1_w_gate.shape[2]
    n_prompt = bp * t_len
    m_rows = n_prompt + bs
    seq_tile = min(SEQ_TILE, t_len)
    n_seq = t_len // seq_tile
    assert dec_len == 1 and bs == LANES and n_prompt % bs == 0 and t_len % seq_tile == 0 and seq_tile % BLK == 0
    row_block = n_prompt // bs
    step_block = n_prompt // STEP_ROWS
    tm = m_rows // 8 if m_rows % 128 == 0 else m_rows
    assert m_rows % tm == 0 and tm % 16 == 0 and n_prompt % STACK_TILE == 0
    o_mlstm, o_ret, o_hgrn = 0, h_mlstm, h_mlstm + h_ret
    assert o_ret % h_ret == 0 and o_hgrn % h_hgrn == 0
    n_slabs = h_hgrn + h_mlstm + h_ret
    o_order = tuple(range(o_hgrn, n_slabs)) + tuple(range(o_mlstm, o_ret)) + tuple(range(o_ret, o_hgrn))

    gate0 = 4 * w_hgrn + 4 * w_mlstm
    ret0 = gate0 + 2 * h_mlstm
    w_in_ret = jnp.concatenate(
        [w_in[:, :, ret0:], jnp.pad(w_in[:, :, gate0:ret0], ((0, 0), (0, 0), (0, LANES - 2 * h_mlstm)))],
        axis=2).astype(BF16)
    w_in_hm = w_in[:, :, :gate0].astype(BF16)
    gate_slab = 4 * h_ret
    bias_pad = jnp.pad(mlstm_gate_bias, ((0, 0), (0, LANES - 2 * h_mlstm))).reshape(depth, 1, LANES)
    m_pad = jnp.pad(state_mlstm_m, ((0, 0), (0, 0), (0, LANES - h_mlstm)))
    gn_h = gn_hgrn.reshape(depth, 1, w_hgrn)
    gn_m = gn_mlstm.reshape(depth, 1, w_mlstm)
    gn_r = gn_ret.reshape(depth, 1, w_ret)
    log_gamma = jnp.log1p(-jnp.exp2(-5.0 - jnp.arange(h_ret, dtype=F32)))
    cos_p, sin_p = _rope_tables(t_len, min(ROPE_TILE, t_len), 0.0, 1.0)
    cos_s, sin_s = _rope_tables(8, 8, PAST_LEN, 0.0)

    smem = pl.BlockSpec(memory_space=pltpu.SMEM)
    o3_shape = (n_slabs, m_rows, LANES)
    head_sq = (HEAD_DIM, HEAD_DIM)

    def z_prompt(n_blocks):
        return pl.BlockSpec((n_blocks, seq_tile, LANES), lambda b, t: (0, b * n_seq + t, 0))

    def o_prompt(heads, slab0):
        return pl.BlockSpec((heads, seq_tile, LANES), lambda b, t: (slab0 // heads, b * n_seq + t, 0))

    def o_step(heads, slab0):
        return pl.BlockSpec((heads, STEP_ROWS, LANES), lambda g: (slab0 // heads, step_block + g, 0))

    x, n = _stack_rows(x_prompt.reshape(n_prompt, d), x_sample.reshape(bs, d), norm_ffn1[0])
    new_h_p, new_c_p, new_n_p, new_m_p, new_r_p = [], [], [], [], []
    new_n_s, new_m_s = [], []
    st_h = st_c = st_r = None

    def ffn(x, n, ssq, w_gate, w_up, w_down, layer, next_norm):
        mid = _matmul(n, [w_gate, w_up], layer, mode="swiglu", tm=2 * tm, tn=512, n_out=d_ff, row_ssq=ssq)
        down = dict(tm=tm // 2, tn=1024, n_out=d, resid=x, weight_buffers=1)
        if next_norm is None:
            return _matmul(mid, [w_down], layer, mode="resid", **down), None, None
        return _matmul(mid, [w_down], layer, mode="resid_gain", norm=next_norm, **down)

    ssq = None
    for layer in range(depth):
        x, n, ssq = ffn(x, n, ssq, ffn1_w_gate, ffn1_w_up, ffn1_w_down, layer, (norm_mix, layer))

        cols = dict(mode="cols", tm=2 * tm, row_ssq=ssq)
        z_h = _matmul(n, [w_in_hm], layer, tn=1024, n_out=4 * w_hgrn, **cols)
        z_m = _matmul(n, [w_in_hm], layer, tn=1024, n_out=4 * w_mlstm, col_off=4 * w_hgrn // 1024, **cols)
        z_r = _matmul(n, [w_in_ret], layer, tn=640, n_out=4 * w_ret + LANES, **cols)

        def gain2(width):
            return pl.BlockSpec((None, 1, width), lambda b, t: (layer, 0, 0))

        def gain1(width):
            return pl.BlockSpec((None, 1, width), lambda g: (layer, 0, 0))

        o3, s_hp = _mixer_call(
            functools.partial(_hgrn_prompt_kernel, layer=layer, n_heads=h_hgrn), "hgrn_prompt", (bp, n_seq),
            [z_prompt(4 * h_hgrn), pl.BlockSpec(hgrn_lb_logits.shape, lambda b, t: (0, 0)), gain2(w_hgrn)],
            [z_h, hgrn_lb_logits, gn_h],
            [(o3_shape, BF16, o_prompt(h_hgrn, o_hgrn), None),
             ((bp, h_hgrn) + head_sq, F32, pl.BlockSpec((None, h_hgrn) + head_sq, lambda b, t: (b, 0, 0, 0)), None)],
            [pltpu.VMEM((h_hgrn,) + head_sq, F32)] + [pltpu.VMEM((h_hgrn, BLK, LANES), F32) for _ in range(4)],
            ("arbitrary", "arbitrary"))

        vec = ((bp, h_mlstm, 1, LANES), F32, pl.BlockSpec((None, h_mlstm, 1, LANES), lambda b, t: (b, 0, 0, 0)), None)
        o3, c_p, n_p, m_p = _mixer_call(
            functools.partial(_mlstm_prompt_kernel, n_heads=h_mlstm), "mlstm_prompt", (bp, n_seq),
            [z_prompt(4 * h_mlstm),
             pl.BlockSpec((None, seq_tile, LANES), lambda b, t: (gate_slab, b * n_seq + t, 0)),
             pl.BlockSpec((None, 1, LANES), lambda b, t: (layer, 0, 0)), gain2(w_mlstm)],
            [z_m, z_r, bias_pad, gn_m],
            [(o3_shape, BF16, o_prompt(h_mlstm, o_mlstm), o3),
             ((bp, h_mlstm) + head_sq, F32, pl.BlockSpec((None, h_mlstm) + head_sq, lambda b, t: (b, 0, 0, 0)), None),
             vec, vec],
            [pltpu.VMEM((h_mlstm, HEAD_DIM, 2 * HEAD_DIM), F32), pltpu.VMEM((h_mlstm, 1, LANES), F32)],
            ("arbitrary", "arbitrary"))

        table = pl.BlockSpec((t_len, LANES), lambda b, t: (0, 0))
        o3, s_rp = _mixer_call(
            functools.partial(_ret_prompt_kernel, n_heads=h_ret), "ret_prompt", (bp, n_seq),
            [z_prompt(4 * h_ret), table, table, smem, gain2(w_ret)],
            [z_r, cos_p, sin_p, log_gamma, gn_r],
            [(o3_shape, BF16, o_prompt(h_ret, o_ret), o3),
             ((bp, h_ret) + head_sq, F32, pl.BlockSpec((None, h_ret) + head_sq, lambda b, t: (b, 0, 0, 0)), None)],
            [pltpu.VMEM((h_ret,) + head_sq, F32), pltpu.VMEM((h_ret, 3, BLK, LANES), F32)],
            ("arbitrary", "arbitrary"))

        def z_step(n_blocks):
            return pl.BlockSpec((n_blocks, bs, LANES), lambda g: (0, row_block, 0))

        def state_in(state):
            tail = state.shape[2:]
            zeros = (0,) * len(tail)
            return pl.BlockSpec((None, STEP_ROWS) + tail, lambda g: (layer, g) + zeros)

        tposed = pltpu.VMEM((h_hgrn, HEAD_DIM, bs), F32)
        o3, st_h = _mixer_call(
            functools.partial(_hgrn_step_kernel, layer=layer, n_heads=h_hgrn), "hgrn_step", (bs // STEP_ROWS,),
            [z_step(4 * h_hgrn), state_in(state_hgrn), pl.BlockSpec(hgrn_lb_logits.shape, lambda g: (0, 0)), gain1(w_hgrn)],
            [z_h, state_hgrn, hgrn_lb_logits, gn_h],
            [(o3_shape, BF16, o_step(h_hgrn, o_hgrn), o3), _state_out(state_hgrn, layer, st_h)],
            [tposed, tposed], ("arbitrary",))

        tposed = pltpu.VMEM((h_mlstm, HEAD_DIM, bs), F32)
        rowrep = pltpu.VMEM((h_mlstm, bs, LANES), F32)
        tile = pl.BlockSpec((bs, LANES), lambda g: (0, 0))
        o3, st_c, n_s, m_s = _mixer_call(
            functools.partial(_mlstm_step_kernel, n_heads=h_mlstm), "mlstm_step", (bs // STEP_ROWS,),
            [z_step(4 * h_mlstm), pl.BlockSpec((None, bs, LANES), lambda g: (gate_slab, row_block, 0)),
             pl.BlockSpec((None, 1, LANES), lambda g: (layer, 0, 0)),
             state_in(state_mlstm_c), state_in(state_mlstm_n), tile, gain1(w_mlstm)],
            [z_m, z_r, bias_pad, state_mlstm_c, state_mlstm_n, m_pad[layer], gn_m],
            [(o3_shape, BF16, o_step(h_mlstm, o_mlstm), o3), _state_out(state_mlstm_c, layer, st_c),
             (state_mlstm_n.shape[1:], F32, pl.BlockSpec((STEP_ROWS, h_mlstm, HEAD_DIM), lambda g: (g, 0, 0)), None),
             ((bs, LANES), F32, tile, None)],
            [tposed, rowrep, rowrep, rowrep], ("arbitrary",))

        tposed = pltpu.VMEM((h_ret, HEAD_DIM, bs), F32)
        table = pl.BlockSpec(cos_s.shape, lambda g: (0, 0))
        o3, st_r = _mixer_call(
            functools.partial(_ret_step_kernel, n_heads=h_ret), "ret_step", (bs // STEP_ROWS,),
            [z_step(4 * h_ret), state_in(state_ret), table, table, smem, gain1(w_ret)],
            [z_r, state_ret, cos_s, sin_s, log_gamma, gn_r],
            [(o3_shape, BF16, o_step(h_ret, o_ret), o3), _state_out(state_ret, layer, st_r)],
            [tposed], ("arbitrary",))

        new_h_p.append(s_hp)
        new_c_p.append(c_p)
        new_n_p.append(n_p[:, :, 0, :])
        new_m_p.append(m_p[:, :, 0, 0])
        new_r_p.append(s_rp)
        new_n_s.append(n_s)
        new_m_s.append(m_s[:, :h_mlstm])

        x, n = _matmul(o3, [w_out], layer, mode="resid_norm", tm=tm // 2, tn=d, n_out=d, resid=x,
                       norm=(norm_ffn2, layer), weight_buffers=1, lhs_order=o_order)

        next_norm = (norm_ffn1, layer + 1) if layer + 1 < depth else None
        x, n, ssq = ffn(x, n, None, ffn2_w_gate, ffn2_w_up, ffn2_w_down, layer, next_norm)

    y_p, y_s = _rmsnorm_split(x, norm_final, n_prompt)
    return (
        y_p.reshape(bp, t_len, d), y_s.reshape(bs, dec_len, d),
        jnp.stack(new_h_p), jnp.stack(new_c_p), jnp.stack(new_n_p), jnp.stack(new_m_p), jnp.stack(new_r_p),
        st_h, st_c, jnp.stack(new_n_s), jnp.stack(new_m_s), st_r,
    )
```

```python
import functools

import jax
import jax.numpy as jnp
from jax import lax
from jax.experimental import pallas as pl
from jax.experimental.pallas import tpu as pltpu

F32 = jnp.float32
BF16 = jnp.bfloat16

HEAD_DIM = 128
LANES = 128
EPS = 1e-6
ROPE_BASE = 10000.0
K_SCALE = HEAD_DIM ** -0.5
PAST_LEN = 16384

BLK = 128
SUB = 32
SEQ_TILE = 512
STEP_ROWS = 16
STEP_UNROLL = 4
STACK_TILE = 1024
ROPE_TILE = 256
VMEM_LIMIT_BYTES = 56 * 1024 * 1024

DECAY_SPAN_MAX = 60.0

_TRANS_B = (((1,), (1,)), ((), ()))
_TRANS_A = (((0,), (0,)), ((), ()))


def _params(*semantics):
    return pltpu.CompilerParams(dimension_semantics=semantics, vmem_limit_bytes=VMEM_LIMIT_BYTES)


def _silu(x):
    return x * jax.nn.sigmoid(x)


def _softplus_neg_abs(x):
    return jnp.log(1.0 + jnp.exp(-jnp.abs(x)))


def _log_sigmoid(x):
    return jnp.minimum(x, 0.0) - _softplus_neg_abs(x)


def _iota(shape, dim):
    return lax.broadcasted_iota(jnp.int32, shape, dim)


def _seg_cumsum(x, seg):
    pos = _iota(x.shape, 0) & (seg - 1)
    shift = 1
    while shift < seg:
        x = x + jnp.where(pos >= shift, pltpu.roll(x, shift, 0), 0.0)
        shift *= 2
    return x


def _head_norm_gate(o, gain, gate):
    return o * lax.rsqrt(jnp.mean(o * o, axis=-1, keepdims=True) + EPS) * gain * gate


def _pick_lane(x, lane_mask):
    return jnp.sum(jnp.where(lane_mask, x, 0.0), axis=-1, keepdims=True)


def _put_row(tile, j, new):
    return jnp.where(_iota(tile.shape, 0) == j, new, tile)


def _stack_rows_kernel(a_ref, b_ref, g_ref, o_ref, n_ref, *, n_a):
    i = pl.program_id(0)

    def emit(x, rows):
        o_ref[rows, :] = x
        y = x * lax.rsqrt(jnp.mean(x * x, axis=-1, keepdims=True) + EPS)
        n_ref[rows, :] = (y * g_ref[...]).astype(n_ref.dtype)

    @pl.when(i < n_a)
    def _():
        emit(a_ref[...], slice(None))

    @pl.when(i == n_a)
    def _():
        emit(b_ref[...], slice(0, b_ref.shape[0]))


def _stack_rows(a, b, gain):
    (ma, d), mb = a.shape, b.shape[0]
    n_a = ma // STACK_TILE
    tile = pl.BlockSpec((STACK_TILE, d), lambda i: (i, 0))
    return pl.pallas_call(
        functools.partial(_stack_rows_kernel, n_a=n_a),
        out_shape=(jax.ShapeDtypeStruct((ma + mb, d), a.dtype), jax.ShapeDtypeStruct((ma + mb, d), BF16)),
        grid=(n_a + 1,),
        in_specs=[
            pl.BlockSpec((STACK_TILE, d), lambda i: (jnp.minimum(i, n_a - 1), 0)),
            pl.BlockSpec((mb, d), lambda i: (0, 0)),
            pl.BlockSpec((1, d), lambda i: (0, 0)),
        ],
        out_specs=(tile, tile),
        compiler_params=_params("arbitrary"),
        name="stack_rows",
    )(a, b, gain.reshape(1, d))


def _rmsnorm_split_kernel(x_ref, g_ref, a_ref, b_ref, *, n_a):
    i = pl.program_id(0)
    x = x_ref[...]
    y = x * lax.rsqrt(jnp.mean(x * x, axis=-1, keepdims=True) + EPS) * g_ref[...]

    @pl.when(i < n_a)
    def _():
        a_ref[...] = y

    @pl.when(i == n_a)
    def _():
        b_ref[...] = y[0:b_ref.shape[0], :]


def _rmsnorm_split(x, gain, rows_a):
    m, d = x.shape
    n_a = rows_a // STACK_TILE
    rows_b = m - rows_a
    return pl.pallas_call(
        functools.partial(_rmsnorm_split_kernel, n_a=n_a),
        out_shape=(jax.ShapeDtypeStruct((rows_a, d), F32), jax.ShapeDtypeStruct((rows_b, d), F32)),
        grid=(n_a + 1,),
        in_specs=[
            pl.BlockSpec((STACK_TILE, d), lambda i: (i, 0)),
            pl.BlockSpec((1, d), lambda i: (0, 0)),
        ],
        out_specs=(
            pl.BlockSpec((STACK_TILE, d), lambda i: (jnp.minimum(i, n_a - 1), 0)),
            pl.BlockSpec((rows_b, d), lambda i: (0, 0)),
        ),
        compiler_params=_params("arbitrary"),
        name="rmsnorm_split",
    )(x, gain.reshape(1, d))


def _matmul_kernel(*refs, n_w, mode, lhs_order, scaled):
    lhs_ref = refs[0]
    w_refs = refs[1:1 + n_w]
    rest = refs[1 + n_w:]

    if lhs_order is None:
        lhs = lhs_ref[...]
    else:
        lhs = jnp.concatenate([lhs_ref[c] for c in lhs_order], axis=1)
    acc = [jnp.dot(lhs, w_ref[...].astype(BF16), preferred_element_type=F32) for w_ref in w_refs]
    if scaled:
        ssq_ref, rest = rest[0], rest[1:]
        total = ssq_ref[0]
        for part in range(1, ssq_ref.shape[0]):
            total = total + ssq_ref[part]
        r = lax.rsqrt(total[:, 0:1] * (1.0 / lhs.shape[1]) + EPS)
        acc = [a * r for a in acc]
    if mode == "swiglu":
        (o_ref,) = rest
        gate, up = acc
        o_ref[...] = (0.5 * _silu(gate) * up).astype(o_ref.dtype)
    elif mode == "resid":
        x_ref, o_ref = rest
        o_ref[...] = x_ref[...] + acc[0]
    elif mode == "resid_gain":
        x_ref, g_ref, o_ref, xg_ref, ssq_ref = rest
        x = x_ref[...] + acc[0]
        o_ref[...] = x
        xg_ref[...] = (x * g_ref[...]).astype(xg_ref.dtype)
        ssq_ref[...] = jnp.broadcast_to(jnp.sum(x * x, axis=-1, keepdims=True), ssq_ref.shape)
    elif mode == "resid_norm":
        x_ref, g_ref, o_ref, n_ref = rest
        x = x_ref[...] + acc[0]
        o_ref[...] = x
        y = x * lax.rsqrt(jnp.mean(x * x, axis=-1, keepdims=True) + EPS)
        n_ref[...] = (y * g_ref[...]).astype(n_ref.dtype)
    else:
        (o_ref,) = rest
        for c in range(o_ref.shape[0]):
            o_ref[c] = acc[0][:, c * LANES:(c + 1) * LANES]


def _matmul(lhs, weights, layer, *, mode, tm, tn, n_out, col_off=0, resid=None, norm=None, row_ssq=None,
            weight_buffers=2, lhs_order=None):
    if lhs_order is None:
        m, k = lhs.shape
        lhs_spec = pl.BlockSpec((tm, k), lambda j, i: (i, 0))
    else:
        m, k = lhs.shape[1], lhs.shape[0] * LANES
        lhs_spec = pl.BlockSpec((lhs.shape[0], tm, LANES), lambda j, i: (0, i, 0))
    n_w = len(weights)
    n_tiles = pl.cdiv(n_out, tn)
    grid = (n_tiles, m // tm)
    in_specs = [lhs_spec]
    in_specs += [pl.BlockSpec((None, k, tn), lambda j, i: (layer, 0, j + col_off),
                              pipeline_mode=pl.Buffered(weight_buffers)) for _ in weights]
    args = [lhs, *weights]
    if row_ssq is not None:
        in_specs.append(pl.BlockSpec((row_ssq.shape[0], tm, LANES), lambda j, i: (0, i, 0)))
        args.append(row_ssq)
    tile = pl.BlockSpec((tm, tn), lambda j, i: (i, j))
    if mode in ("resid", "resid_gain", "resid_norm"):
        in_specs.append(tile)
        args.append(resid)
    if mode in ("resid_gain", "resid_norm"):
        gains, norm_layer = norm
        in_specs.append(pl.BlockSpec((None, 1, tn), lambda j, i: (norm_layer, 0, j)))
        args.append(gains.reshape(-1, 1, n_out))
    if mode == "resid_gain":
        out_shape = (jax.ShapeDtypeStruct((m, n_out), F32), jax.ShapeDtypeStruct((m, n_out), BF16),
                     jax.ShapeDtypeStruct((n_tiles, m, LANES), F32))
        out_spec = (tile, tile, pl.BlockSpec((None, tm, LANES), lambda j, i: (j, i, 0)))
    elif mode == "resid_norm":
        assert tn == n_out
        out_shape = (jax.ShapeDtypeStruct((m, n_out), F32), jax.ShapeDtypeStruct((m, n_out), BF16))
        out_spec = (tile, tile)
    elif mode == "cols":
        out_shape = jax.ShapeDtypeStruct((n_out // LANES, m, LANES), F32)
        out_spec = pl.BlockSpec((tn // LANES, tm, LANES), lambda j, i: (j, i, 0))
    else:
        out_shape = jax.ShapeDtypeStruct((m, n_out), BF16 if mode == "swiglu" else F32)
        out_spec = tile
    return pl.pallas_call(
        functools.partial(_matmul_kernel, n_w=n_w, mode=mode, lhs_order=lhs_order, scaled=row_ssq is not None),
        out_shape=out_shape,
        grid=grid,
        in_specs=in_specs,
        out_specs=out_spec,
        compiler_params=_params("arbitrary", "arbitrary"),
        name="matmul_" + mode,
    )(*args)


def _rope_kernel(inv_ref, cos_ref, sin_ref, *, pos0, step):
    shape = cos_ref.shape
    row = _iota(shape, 0) + pl.program_id(0) * shape[0]
    ang = (pos0 + step * row.astype(F32)) * inv_ref[...]
    sin = jnp.sin(ang)
    cos_ref[...] = jnp.cos(ang)
    sin_ref[...] = jnp.where(_iota(shape, 1) < HEAD_DIM // 2, -sin, sin)


def _rope_tables(rows, tile, pos0, step):
    inv = ROPE_BASE ** (-jnp.arange(0, HEAD_DIM, 2, dtype=F32) / HEAD_DIM)
    inv = jnp.concatenate([inv, inv]).reshape(1, HEAD_DIM)
    out = jax.ShapeDtypeStruct((rows, HEAD_DIM), F32)
    spec = pl.BlockSpec((tile, HEAD_DIM), lambda i: (i, 0))
    return pl.pallas_call(
        functools.partial(_rope_kernel, pos0=float(pos0), step=float(step)),
        out_shape=(out, out),
        grid=(rows // tile,),
        in_specs=[pl.BlockSpec((1, HEAD_DIM), lambda i: (0, 0))],
        out_specs=(spec, spec),
        compiler_params=_params("arbitrary"),
        name="rope_tables",
    )(inv)


def _rope(x, cos, sin):
    return x * cos + pltpu.roll(x, HEAD_DIM // 2, 1) * sin


def _mixer_call(kernel_fn, name, grid, in_specs, args, outs, scratch, semantics):
    in_specs, args, aliases = list(in_specs), list(args), {}
    for k, (_, _, _, prev) in enumerate(outs):
        if prev is not None:
            aliases[len(args)] = k
            in_specs.append(pl.BlockSpec(memory_space=pl.ANY))
            args.append(prev)
    n_alias = len(aliases)

    def body(*refs):
        n_in = len(args) - n_alias
        kernel_fn(*refs[:n_in], *refs[n_in + n_alias:])

    return pl.pallas_call(
        body,
        out_shape=tuple(jax.ShapeDtypeStruct(s, dt) for s, dt, _, _ in outs),
        grid=grid,
        in_specs=in_specs,
        out_specs=tuple(spec for _, _, spec, _ in outs),
        scratch_shapes=scratch,
        input_output_aliases=aliases,
        compiler_params=_params(*semantics),
        name=name,
    )(*args)


def _state_out(state, layer, prev):
    tail = state.shape[2:]
    zeros = (0,) * len(tail)
    spec = pl.BlockSpec((None, STEP_ROWS) + tail, lambda g: (layer, g) + zeros)
    return (state.shape, F32, spec, prev)


def _hgrn_lower_bound(logits, layer):
    mx = jnp.max(logits, axis=0, keepdims=True)
    e = jnp.exp(logits - mx)
    p = e / jnp.sum(e, axis=0, keepdims=True)
    if layer == 0:
        return jnp.zeros_like(mx)
    return jnp.sum(p[1:layer + 1], axis=0, keepdims=True)


def _hgrn_gates(fz, lb):
    log_lb = jnp.log(lb)
    y = jnp.log1p(-lb) + _log_sigmoid(fz)
    log_f = jnp.maximum(log_lb, y) + _softplus_neg_abs(log_lb - y)
    key = (1.0 - lb) / (1.0 + jnp.exp(fz))
    return log_f, key


def _hgrn_prompt_kernel(z_ref, lbl_ref, gn_ref, o_ref, s_ref, st_ref, q_scr, k_scr, b_scr, a_scr, *, layer, n_heads):
    t = pl.program_id(1)
    n_sub = BLK // SUB
    lb = _hgrn_lower_bound(lbl_ref[...], layer)
    row = _iota((BLK, LANES), 0)
    col = _iota((BLK, LANES), 1)
    pos = row & (SUB - 1)
    seg0 = row - pos

    @pl.when(t == 0)
    def _():
        st_ref[...] = jnp.zeros_like(st_ref)

    def seg_row(x, s):
        x3 = x.reshape(n_sub, SUB, LANES)
        return jnp.broadcast_to(x3[:, s:s + 1, :], x3.shape).reshape(BLK, LANES)

    def block(i, carry):
        rows = pl.ds(pl.multiple_of(i * BLK, BLK), BLK)
        span = jnp.zeros((BLK, LANES), F32)
        for h in range(n_heads):
            log_f, k = _hgrn_gates(z_ref[n_heads + h, rows, :], lb[:, h * LANES:(h + 1) * LANES])
            b = _seg_cumsum(log_f, SUB)
            q_scr[h] = _silu(z_ref[h, rows, :])
            k_scr[h] = k
            b_scr[h] = b
            span = jnp.maximum(span, -seg_row(b, SUB - 1))
        mild = jnp.max(span) <= DECAY_SPAN_MAX

        @pl.when(mild)
        def _():
            q_end, k_end = [], []
            for h in range(n_heads):
                b = b_scr[h]
                b_end = seg_row(b, SUB - 1)
                q_end.append((q_scr[h] * jnp.exp(b - b_end)).astype(BF16))
                k_end.append((k_scr[h] * jnp.exp(b_end - b)).astype(BF16))
            a = [lax.dot_general(q_end[h], k_end[h], _TRANS_B, preferred_element_type=F32) for h in range(n_heads)]
            for h in range(n_heads):
                a_scr[h] = jnp.where((col >= seg0) & (col <= row), a[h], 0.0)

        @pl.when(jnp.logical_not(mild))
        def _():
            for h in range(n_heads):
                q, k, b = q_scr[h], k_scr[h], b_scr[h]
                a = jnp.zeros((BLK, LANES), F32)
                for s in range(SUB):
                    e = jnp.exp(jnp.where(pos >= s, b - seg_row(b, s), -jnp.inf))
                    c = jnp.sum(q * seg_row(k, s) * e, axis=-1, keepdims=True)
                    a = jnp.where(col == seg0 + s, c, a)
                a_scr[h] = a

        heads = range(n_heads)
        subs = [slice(n * SUB, (n + 1) * SUB) for n in range(n_sub)]
        vb, q_in, k_out, lam = [], [], [], []
        for h in heads:
            b = b_scr[h]
            b_end = seg_row(b, SUB - 1)
            vb.append(z_ref[2 * n_heads + h, rows, :].astype(BF16))
            q_in.append((q_scr[h] * jnp.exp(b)).astype(BF16))
            k_out.append((k_scr[h] * jnp.exp(b_end - b)).astype(BF16))
            lam.append(jnp.exp(b_end))
        ups = [[lax.dot_general(vb[h][sl], k_out[h][sl], _TRANS_A, preferred_element_type=F32) for sl in subs]
               for h in heads]
        o = [jnp.dot(a_scr[h].astype(BF16), vb[h], preferred_element_type=F32) for h in heads]
        states = []
        for h in heads:
            chain = [st_ref[h]]
            for n in range(n_sub):
                chain.append(chain[n] * lam[h][n * SUB:n * SUB + 1, :] + ups[h][n])
            st_ref[h] = chain[n_sub]
            states.append([s.astype(BF16) for s in chain[:n_sub]])
        parts = [[lax.dot_general(q_in[h][sl], states[h][n], _TRANS_B, preferred_element_type=F32)
                  for n, sl in enumerate(subs)] for h in heads]
        for h in heads:
            gate = _silu(z_ref[3 * n_heads + h, rows, :])
            out = _head_norm_gate(o[h] + jnp.concatenate(parts[h], axis=0), gn_ref[:, h * LANES:(h + 1) * LANES], gate)
            o_ref[h, rows, :] = out.astype(o_ref.dtype)
        return carry

    lax.fori_loop(0, z_ref.shape[1] // BLK, block, 0)

    @pl.when(t == pl.num_programs(1) - 1)
    def _():
        for h in range(n_heads):
            s_ref[h] = st_ref[h].T


def _hgrn_step_kernel(z_ref, s_ref, lbl_ref, gn_ref, o_ref, so_ref, f_scr, k_scr, *, layer, n_heads):
    g = pl.program_id(0)
    n_rows = z_ref.shape[1]

    @pl.when(g == 0)
    def _():
        lb = _hgrn_lower_bound(lbl_ref[...], layer)
        for h in range(n_heads):
            log_f, k = _hgrn_gates(z_ref[n_heads + h], lb[:, h * LANES:(h + 1) * LANES])
            f_scr[h] = jnp.exp(log_f).T
            k_scr[h] = k.T

    lane = _iota((HEAD_DIM, n_rows), 1)
    grp = pl.ds(pl.multiple_of(g * STEP_ROWS, STEP_ROWS), STEP_ROWS)
    q = [_silu(z_ref[h, grp, :]).astype(BF16) for h in range(n_heads)]

    def body(j, outs):
        b = g * STEP_ROWS + j
        sel = lane == b
        one = pl.ds(b, 1)
        new = []
        for h in range(n_heads):
            s_new = _pick_lane(f_scr[h], sel) * s_ref[j, h] + _pick_lane(k_scr[h], sel) * z_ref[2 * n_heads + h, one, :]
            so_ref[j, h] = s_new
            new.append(_put_row(outs[h], j, jnp.dot(q[h], s_new.astype(BF16), preferred_element_type=F32)))
        return tuple(new)

    zero = jnp.zeros((STEP_ROWS, LANES), F32)
    outs = lax.fori_loop(0, STEP_ROWS, body, (zero,) * n_heads, unroll=STEP_UNROLL)
    for h in range(n_heads):
        gate = _silu(z_ref[3 * n_heads + h, grp, :])
        o_ref[h] = _head_norm_gate(outs[h], gn_ref[:, h * LANES:(h + 1) * LANES], gate).astype(o_ref.dtype)


def _ret_prompt_kernel(z_ref, cos_ref, sin_ref, lg_ref, gn_ref, o_ref, s_ref, st_ref, dec_ref, *, n_heads):
    t = pl.program_id(1)

    @pl.when((pl.program_id(0) == 0) & (t == 0))
    def _():
        row = _iota((BLK, LANES), 0).astype(F32)
        diff = row - _iota((BLK, LANES), 1).astype(F32)
        for h in range(n_heads):
            lg = lg_ref[h]
            dec_ref[h, 0] = jnp.where(diff >= 0.0, jnp.exp(diff * lg), 0.0)
            dec_ref[h, 1] = jnp.exp((row + 1.0) * lg)
            dec_ref[h, 2] = jnp.exp((BLK - 1.0 - row) * lg)

    @pl.when(t == 0)
    def _():
        st_ref[...] = jnp.zeros_like(st_ref)

    def chunk(i, carry):
        rows = pl.ds(pl.multiple_of(i * BLK, BLK), BLK)
        tab = pl.ds(pl.multiple_of(t * z_ref.shape[1] + i * BLK, BLK), BLK)
        cos = cos_ref[tab, :]
        sin = sin_ref[tab, :]
        heads = range(n_heads)
        q = [_rope(z_ref[h, rows, :], cos, sin) for h in heads]
        k = [_rope(z_ref[n_heads + h, rows, :], cos, sin) * K_SCALE for h in heads]
        vb = [z_ref[2 * n_heads + h, rows, :].astype(BF16) for h in heads]
        a = [lax.dot_general(q[h].astype(BF16), k[h].astype(BF16), _TRANS_B, preferred_element_type=F32)
             for h in heads]
        carried = [jnp.dot((q[h] * dec_ref[h, 1]).astype(BF16), st_ref[h].astype(BF16), preferred_element_type=F32)
                   for h in heads]
        update = [jnp.dot((k[h] * dec_ref[h, 2]).T.astype(BF16), vb[h], preferred_element_type=F32) for h in heads]
        o = [jnp.dot((a[h] * dec_ref[h, 0]).astype(BF16), vb[h], preferred_element_type=F32) for h in heads]
        for h in heads:
            s_dec = jnp.exp(jnp.full((1, LANES), BLK, F32) * lg_ref[h])
            st_ref[h] = s_dec * st_ref[h] + update[h]
            gate = _silu(z_ref[3 * n_heads + h, rows, :])
            out = _head_norm_gate(o[h] + carried[h], gn_ref[:, h * LANES:(h + 1) * LANES], gate)
            o_ref[h, rows, :] = out.astype(o_ref.dtype)
        return carry

    lax.fori_loop(0, z_ref.shape[1] // BLK, chunk, 0)

    @pl.when(t == pl.num_programs(1) - 1)
    def _():
        s_ref[...] = st_ref[...]


def _ret_step_kernel(z_ref, s_ref, cos_ref, sin_ref, lg_ref, gn_ref, o_ref, so_ref, k_scr, *, n_heads):
    g = pl.program_id(0)
    n_rows = z_ref.shape[1]
    cos = cos_ref[0:1, :]
    sin = sin_ref[0:1, :]

    @pl.when(g == 0)
    def _():
        for h in range(n_heads):
            k_scr[h] = (_rope(z_ref[n_heads + h], cos, sin) * K_SCALE).T

    lane = _iota((HEAD_DIM, n_rows), 1)
    grp = pl.ds(pl.multiple_of(g * STEP_ROWS, STEP_ROWS), STEP_ROWS)
    q = [_rope(z_ref[h, grp, :], cos, sin).astype(BF16) for h in range(n_heads)]

    def body(j, outs):
        b = g * STEP_ROWS + j
        sel = lane == b
        one = pl.ds(b, 1)
        new = []
        for h in range(n_heads):
            gamma = jnp.exp(jnp.full((1, LANES), lg_ref[h], F32))
            s_new = gamma * s_ref[j, h] + _pick_lane(k_scr[h], sel) * z_ref[2 * n_heads + h, one, :]
            so_ref[j, h] = s_new
            new.append(_put_row(outs[h], j, jnp.dot(q[h], s_new.astype(BF16), preferred_element_type=F32)))
        return tuple(new)

    zero = jnp.zeros((STEP_ROWS, LANES), F32)
    outs = lax.fori_loop(0, STEP_ROWS, body, (zero,) * n_heads, unroll=STEP_UNROLL)
    for h in range(n_heads):
        gate = _silu(z_ref[3 * n_heads + h, grp, :])
        o_ref[h] = _head_norm_gate(outs[h], gn_ref[:, h * LANES:(h + 1) * LANES], gate).astype(o_ref.dtype)


def _mlstm_prompt_kernel(z_ref, zg_ref, bias_ref, gn_ref, o_ref, c_ref, n_ref, m_ref, cn_scr, m_scr, *, n_heads):
    t = pl.program_id(1)
    lane = _iota((BLK, LANES), 1)
    causal = lane <= _iota((BLK, LANES), 0)
    ones = jnp.ones((BLK, LANES), BF16)

    def twice(x):
        return jnp.concatenate([x, x], axis=1)

    @pl.when(t == 0)
    def _():
        cn_scr[...] = jnp.zeros_like(cn_scr)
        m_scr[...] = jnp.zeros_like(m_scr)

    def chunk(i, carry):
        rows = pl.ds(pl.multiple_of(i * BLK, BLK), BLK)
        gates = zg_ref[rows, :] + bias_ref[...]
        log_f_all = pltpu.roll(_log_sigmoid(gates), LANES - n_heads, 1)
        bt_all = _seg_cumsum(log_f_all, BLK)
        w_t = (gates - bt_all).T
        heads = range(n_heads)
        d_mat, w_inter, m_t, w_c, kw_t, qb, kb, v_ones = [], [], [], [], [], [], [], []
        for h in heads:
            bt = _pick_lane(bt_all, lane == h)
            w_row = w_t[h:h + 1, :]
            log_d = jnp.where(causal, bt + w_row, -jnp.inf)
            m_prev = m_scr[h]
            log_inter = bt + m_prev
            m_t.append(jnp.maximum(log_inter, jnp.max(log_d, axis=-1, keepdims=True)))
            d_mat.append(jnp.exp(log_d - m_t[h]))
            w_inter.append(jnp.exp(log_inter - m_t[h]))
            m_end = m_t[h][BLK - 1:BLK, :]
            b_end = bt[BLK - 1:BLK, :]
            k = z_ref[n_heads + h, rows, :] * K_SCALE
            kb.append(k.astype(BF16))
            kw_t.append((k.T * jnp.exp(b_end + w_row - m_end)).astype(BF16))
            w_c.append(jnp.exp(b_end + m_prev - m_end))
            qb.append(z_ref[h, rows, :].astype(BF16))
            v_ones.append(jnp.concatenate([z_ref[2 * n_heads + h, rows, :].astype(BF16), ones], axis=1))
        sc = [lax.dot_general(qb[h], kb[h], _TRANS_B, preferred_element_type=F32) for h in heads]
        carried = [jnp.dot(qb[h], cn_scr[h].astype(BF16), preferred_element_type=F32) for h in heads]
        update = [jnp.dot(kw_t[h], v_ones[h], preferred_element_type=F32) for h in heads]
        both = [jnp.dot((sc[h] * d_mat[h]).astype(BF16), v_ones[h], preferred_element_type=F32) for h in heads]
        for h in heads:
            tot = both[h] + carried[h] * twice(w_inter[h])
            num, den = tot[:, :LANES], tot[:, LANES:]
            hid = num / jnp.maximum(jnp.abs(den), jnp.exp(-m_t[h]))
            cn_scr[h] = twice(w_c[h]) * cn_scr[h] + update[h]
            m_scr[h] = m_t[h][BLK - 1:BLK, :]
            gate = jax.nn.sigmoid(z_ref[3 * n_heads + h, rows, :])
            out = _head_norm_gate(hid, gn_ref[:, h * LANES:(h + 1) * LANES], gate)
            o_ref[h, rows, :] = out.astype(o_ref.dtype)
        return carry

    lax.fori_loop(0, z_ref.shape[1] // BLK, chunk, 0)

    @pl.when(t == pl.num_programs(1) - 1)
    def _():
        for h in range(n_heads):
            c_ref[h] = cn_scr[h, :, :LANES]
            n_ref[h] = cn_scr[h, :, LANES:].T[0:1, :]
        m_ref[...] = m_scr[...]


def _mlstm_step_kernel(z_ref, zg_ref, bias_ref, c_ref, n_ref, m_ref, gn_ref,
                       o_ref, co_ref, no_ref, mo_ref, k_scr, wc_scr, wk_scr, em_scr, *, n_heads):
    g = pl.program_id(0)
    n_rows = z_ref.shape[1]

    @pl.when(g == 0)
    def _():
        lanes = _iota((n_rows, LANES), 1)
        gates = zg_ref[...] + bias_ref[...]
        log_f = pltpu.roll(_log_sigmoid(gates), LANES - n_heads, 1)
        m_old = m_ref[...]
        m_new = jnp.maximum(log_f + m_old, gates)
        mo_ref[...] = m_new
        w_c = jnp.exp(log_f + m_old - m_new)
        w_k = jnp.exp(gates - m_new)
        e_m = jnp.exp(-m_new)
        for h in range(n_heads):
            sel = lanes == h
            wc_scr[h] = jnp.broadcast_to(_pick_lane(w_c, sel), (n_rows, LANES))
            wk_scr[h] = jnp.broadcast_to(_pick_lane(w_k, sel), (n_rows, LANES))
            em_scr[h] = jnp.broadcast_to(_pick_lane(e_m, sel), (n_rows, LANES))
            k_scr[h] = (z_ref[n_heads + h] * K_SCALE).T

    lane = _iota((HEAD_DIM, n_rows), 1)
    grp = pl.ds(pl.multiple_of(g * STEP_ROWS, STEP_ROWS), STEP_ROWS)
    q = [z_ref[h, grp, :].astype(BF16) for h in range(n_heads)]

    def body(j, outs):
        b = g * STEP_ROWS + j
        sel = lane == b
        one = pl.ds(b, 1)
        new = []
        for h in range(n_heads):
            w_c = wc_scr[h, one, :]
            w_k = wk_scr[h, one, :]
            c_new = w_c * c_ref[j, h] + (w_k * _pick_lane(k_scr[h], sel)) * z_ref[2 * n_heads + h, one, :]
            co_ref[j, h] = c_new
            n_new = w_c * n_ref[j, h:h + 1, :] + w_k * (z_ref[n_heads + h, one, :] * K_SCALE)
            no_ref[j, h:h + 1, :] = n_new
            num = jnp.dot(q[h], c_new.astype(BF16), preferred_element_type=F32)
            new.append(_put_row(outs[2 * h], j, num))
            new.append(_put_row(outs[2 * h + 1], j, n_new))
        return tuple(new)

    zero = jnp.zeros((STEP_ROWS, LANES), F32)
    outs = lax.fori_loop(0, STEP_ROWS, body, (zero,) * (2 * n_heads), unroll=STEP_UNROLL)
    for h in range(n_heads):
        num, n_new = outs[2 * h], outs[2 * h + 1]
        den = jnp.sum(z_ref[h, grp, :] * n_new, axis=-1, keepdims=True)
        hid = num / jnp.maximum(jnp.abs(den), em_scr[h, grp, :])
        gate = jax.nn.sigmoid(z_ref[3 * n_heads + h, grp, :])
        o_ref[h] = _head_norm_gate(hid, gn_ref[:, h * LANES:(h + 1) * LANES], gate).astype(o_ref.dtype)


def kernel(x_prompt, x_sample, state_hgrn, state_mlstm_c, state_mlstm_n, state_mlstm_m, state_ret, norm_ffn1, ffn1_w_gate, ffn1_w_up, ffn1_w_down, norm_mix, w_in, hgrn_lb_logits, mlstm_gate_bias, gn_hgrn, gn_mlstm, gn_ret, w_out, norm_ffn2, ffn2_w_gate, ffn2_w_up, ffn2_w_down, norm_final):
    bp, t_len, d = x_prompt.shape
    bs, dec_len, _ = x_sample.shape
    depth = w_in.shape[0]
    h_hgrn, h_mlstm, h_ret = state_hgrn.shape[2], state_mlstm_c.shape[2], state_ret.shape[2]
    w_hgrn, w_mlstm, w_ret = h_hgrn * HEAD_DIM, h_mlstm * HEAD_DIM, h_ret * HEAD_DIM
    d_ff = ffn1_w_gate.shape[2]
    n_prompt = bp * t_len
    m_rows = n_prompt + bs
    seq_tile = min(SEQ_TILE, t_len)
    n_seq = t_len // seq_tile
    assert dec_len == 1 and bs == LANES and n_prompt % bs == 0 and t_len % seq_tile == 0 and seq_tile % BLK == 0
    row_block = n_prompt // bs
    step_block = n_prompt // STEP_ROWS
    tm = m_rows // 8 if m_rows % 128 == 0 else m_rows
    assert m_rows % tm == 0 and tm % 16 == 0 and n_prompt % STACK_TILE == 0
    o_mlstm, o_ret, o_hgrn = 0, h_mlstm, h_mlstm + h_ret
    assert o_ret % h_ret == 0 and o_hgrn % h_hgrn == 0
    n_slabs = h_hgrn + h_mlstm + h_ret
    o_order = tuple(range(o_hgrn, n_slabs)) + tuple(range(o_mlstm, o_ret)) + tuple(range(o_ret, o_hgrn))

    gate0 = 4 * w_hgrn + 4 * w_mlstm
    ret0 = gate0 + 2 * h_mlstm
    w_in_ret = jnp.concatenate(
        [w_in[:, :, ret0:], jnp.pad(w_in[:, :, gate0:ret0], ((0, 0), (0, 0), (0, LANES - 2 * h_mlstm)))],
        axis=2).astype(BF16)
    w_in_hm = w_in[:, :, :gate0].astype(BF16)
    gate_slab = 4 * h_ret
    bias_pad = jnp.pad(mlstm_gate_bias, ((0, 0), (0, LANES - 2 * h_mlstm))).reshape(depth, 1, LANES)
    m_pad = jnp.pad(state_mlstm_m, ((0, 0), (0, 0), (0, LANES - h_mlstm)))
    gn_h = gn_hgrn.reshape(depth, 1, w_hgrn)
    gn_m = gn_mlstm.reshape(depth, 1, w_mlstm)
    gn_r = gn_ret.reshape(depth, 1, w_ret)
    log_gamma = jnp.log1p(-jnp.exp2(-5.0 - jnp.arange(h_ret, dtype=F32)))
    cos_p, sin_p = _rope_tables(t_len, min(ROPE_TILE, t_len), 0.0, 1.0)
    cos_s, sin_s = _rope_tables(8, 8, PAST_LEN, 0.0)

    smem = pl.BlockSpec(memory_space=pltpu.SMEM)
    o3_shape = (n_slabs, m_rows, LANES)
    head_sq = (HEAD_DIM, HEAD_DIM)

    def z_prompt(n_blocks):
        return pl.BlockSpec((n_blocks, seq_tile, LANES), lambda b, t: (0, b * n_seq + t, 0))

    def o_prompt(heads, slab0):
        return pl.BlockSpec((heads, seq_tile, LANES), lambda b, t: (slab0 // heads, b * n_seq + t, 0))

    def o_step(heads, slab0):
        return pl.BlockSpec((heads, STEP_ROWS, LANES), lambda g: (slab0 // heads, step_block + g, 0))

    x, n = _stack_rows(x_prompt.reshape(n_prompt, d), x_sample.reshape(bs, d), norm_ffn1[0])
    new_h_p, new_c_p, new_n_p, new_m_p, new_r_p = [], [], [], [], []
    new_n_s, new_m_s = [], []
    st_h = st_c = st_r = None

    def ffn(x, n, ssq, w_gate, w_up, w_down, layer, next_norm):
        mid = _matmul(n, [w_gate, w_up], layer, mode="swiglu", tm=2 * tm, tn=512, n_out=d_ff, row_ssq=ssq)
        down = dict(tm=tm // 2, tn=1024, n_out=d, resid=x, weight_buffers=1)
        if next_norm is None:
            return _matmul(mid, [w_down], layer, mode="resid", **down), None, None
        return _matmul(mid, [w_down], layer, mode="resid_gain", norm=next_norm, **down)

    ssq = None
    for layer in range(depth):
        x, n, ssq = ffn(x, n, ssq, ffn1_w_gate, ffn1_w_up, ffn1_w_down, layer, (norm_mix, layer))

        cols = dict(mode="cols", tm=2 * tm, row_ssq=ssq)
        z_h = _matmul(n, [w_in_hm], layer, tn=1024, n_out=4 * w_hgrn, **cols)
        z_m = _matmul(n, [w_in_hm], layer, tn=1024, n_out=4 * w_mlstm, col_off=4 * w_hgrn // 1024, **cols)
        z_r = _matmul(n, [w_in_ret], layer, tn=640, n_out=4 * w_ret + LANES, **cols)

        def gain2(width):
            return pl.BlockSpec((None, 1, width), lambda b, t: (layer, 0, 0))

        def gain1(width):
            return pl.BlockSpec((None, 1, width), lambda g: (layer, 0, 0))

        o3, s_hp = _mixer_call(
            functools.partial(_hgrn_prompt_kernel, layer=layer, n_heads=h_hgrn), "hgrn_prompt", (bp, n_seq),
            [z_prompt(4 * h_hgrn), pl.BlockSpec(hgrn_lb_logits.shape, lambda b, t: (0, 0)), gain2(w_hgrn)],
            [z_h, hgrn_lb_logits, gn_h],
            [(o3_shape, BF16, o_prompt(h_hgrn, o_hgrn), None),
             ((bp, h_hgrn) + head_sq, F32, pl.BlockSpec((None, h_hgrn) + head_sq, lambda b, t: (b, 0, 0, 0)), None)],
            [pltpu.VMEM((h_hgrn,) + head_sq, F32)] + [pltpu.VMEM((h_hgrn, BLK, LANES), F32) for _ in range(4)],
            ("arbitrary", "arbitrary"))

        vec = ((bp, h_mlstm, 1, LANES), F32, pl.BlockSpec((None, h_mlstm, 1, LANES), lambda b, t: (b, 0, 0, 0)), None)
        o3, c_p, n_p, m_p = _mixer_call(
            functools.partial(_mlstm_prompt_kernel, n_heads=h_mlstm), "mlstm_prompt", (bp, n_seq),
            [z_prompt(4 * h_mlstm),
             pl.BlockSpec((None, seq_tile, LANES), lambda b, t: (gate_slab, b * n_seq + t, 0)),
             pl.BlockSpec((None, 1, LANES), lambda b, t: (layer, 0, 0)), gain2(w_mlstm)],
            [z_m, z_r, bias_pad, gn_m],
            [(o3_shape, BF16, o_prompt(h_mlstm, o_mlstm), o3),
             ((bp, h_mlstm) + head_sq, F32, pl.BlockSpec((None, h_mlstm) + head_sq, lambda b, t: (b, 0, 0, 0)), None),
             vec, vec],
            [pltpu.VMEM((h_mlstm, HEAD_DIM, 2 * HEAD_DIM), F32), pltpu.VMEM((h_mlstm, 1, LANES), F32)],
            ("arbitrary", "arbitrary"))

        table = pl.BlockSpec((t_len, LANES), lambda b, t: (0, 0))
        o3, s_rp = _mixer_call(
            functools.partial(_ret_prompt_kernel, n_heads=h_ret), "ret_prompt", (bp, n_seq),
            [z_prompt(4 * h_ret), table, table, smem, gain2(w_ret)],
            [z_r, cos_p, sin_p, log_gamma, gn_r],
            [(o3_shape, BF16, o_prompt(h_ret, o_ret), o3),
             ((bp, h_ret) + head_sq, F32, pl.BlockSpec((None, h_ret) + head_sq, lambda b, t: (b, 0, 0, 0)), None)],
            [pltpu.VMEM((h_ret,) + head_sq, F32), pltpu.VMEM((h_ret, 3, BLK, LANES), F32)],
            ("arbitrary", "arbitrary"))

        def z_step(n_blocks):
            return pl.BlockSpec((n_blocks, bs, LANES), lambda g: (0, row_block, 0))

        def state_in(state):
            tail = state.shape[2:]
            zeros = (0,) * len(tail)
            return pl.BlockSpec((None, STEP_ROWS) + tail, lambda g: (layer, g) + zeros)

        tposed = pltpu.VMEM((h_hgrn, HEAD_DIM, bs), F32)
        o3, st_h = _mixer_call(
            functools.partial(_hgrn_step_kernel, layer=layer, n_heads=h_hgrn), "hgrn_step", (bs // STEP_ROWS,),
            [z_step(4 * h_hgrn), state_in(state_hgrn), pl.BlockSpec(hgrn_lb_logits.shape, lambda g: (0, 0)), gain1(w_hgrn)],
            [z_h, state_hgrn, hgrn_lb_logits, gn_h],
            [(o3_shape, BF16, o_step(h_hgrn, o_hgrn), o3), _state_out(state_hgrn, layer, st_h)],
            [tposed, tposed], ("arbitrary",))

        tposed = pltpu.VMEM((h_mlstm, HEAD_DIM, bs), F32)
        rowrep = pltpu.VMEM((h_mlstm, bs, LANES), F32)
        tile = pl.BlockSpec((bs, LANES), lambda g: (0, 0))
        o3, st_c, n_s, m_s = _mixer_call(
            functools.partial(_mlstm_step_kernel, n_heads=h_mlstm), "mlstm_step", (bs // STEP_ROWS,),
            [z_step(4 * h_mlstm), pl.BlockSpec((None, bs, LANES), lambda g: (gate_slab, row_block, 0)),
             pl.BlockSpec((None, 1, LANES), lambda g: (layer, 0, 0)),
             state_in(state_mlstm_c), state_in(state_mlstm_n), tile, gain1(w_mlstm)],
            [z_m, z_r, bias_pad, state_mlstm_c, state_mlstm_n, m_pad[layer], gn_m],
            [(o3_shape, BF16, o_step(h_mlstm, o_mlstm), o3), _state_out(state_mlstm_c, layer, st_c),
             (state_mlstm_n.shape[1:], F32, pl.BlockSpec((STEP_ROWS, h_mlstm, HEAD_DIM), lambda g: (g, 0, 0)), None),
             ((bs, LANES), F32, tile, None)],
            [tposed, rowrep, rowrep, rowrep], ("arbitrary",))

        tposed = pltpu.VMEM((h_ret, HEAD_DIM, bs), F32)
        table = pl.BlockSpec(cos_s.shape, lambda g: (0, 0))
        o3, st_r = _mixer_call(
            functools.partial(_ret_step_kernel, n_heads=h_ret), "ret_step", (bs // STEP_ROWS,),
            [z_step(4 * h_ret), state_in(state_ret), table, table, smem, gain1(w_ret)],
            [z_r, state_ret, cos_s, sin_s, log_gamma, gn_r],
            [(o3_shape, BF16, o_step(h_ret, o_ret), o3), _state_out(state_ret, layer, st_r)],
            [tposed], ("arbitrary",))

        new_h_p.append(s_hp)
        new_c_p.append(c_p)
        new_n_p.append(n_p[:, :, 0, :])
        new_m_p.append(m_p[:, :, 0, 0])
        new_r_p.append(s_rp)
        new_n_s.append(n_s)
        new_m_s.append(m_s[:, :h_mlstm])

        x, n = _matmul(o3, [w_out], layer, mode="resid_norm", tm=tm // 2, tn=d, n_out=d, resid=x,
                       norm=(norm_ffn2, layer), weight_buffers=1, lhs_order=o_order)

        next_norm = (norm_ffn1, layer + 1) if layer + 1 < depth else None
        x, n, ssq = ffn(x, n, None, ffn2_w_gate, ffn2_w_up, ffn2_w_down, layer, next_norm)

    y_p, y_s = _rmsnorm_split(x, norm_final, n_prompt)
    return (
        y_p.reshape(bp, t_len, d), y_s.reshape(bs, dec_len, d),
        jnp.stack(new_h_p), jnp.stack(new_c_p), jnp.stack(new_n_p), jnp.stack(new_m_p), jnp.stack(new_r_p),
        st_h, st_c, jnp.stack(new_n_s), jnp.stack(new_m_s), st_r,
    )
```

```python
import functools

import jax
import jax.numpy as jnp
from jax import lax
from jax.experimental import pallas as pl
from jax.experimental.pallas import tpu as pltpu

F32 = jnp.float32
BF16 = jnp.bfloat16

HEAD_DIM = 128
LANES = 128
EPS = 1e-6
ROPE_BASE = 10000.0
K_SCALE = HEAD_DIM ** -0.5
PAST_LEN = 16384

BLK = 128
SUB = 32
SEQ_TILE = 512
STEP_ROWS = 16
STEP_UNROLL = 4
STACK_TILE = 1024
ROW_PARTS = (5, 2, 1)
WIDE_ROW_TILE = 2080
DEEP_ROW_TILE = 640
ROPE_TILE = 256
VMEM_LIMIT_BYTES = 60 * 1024 * 1024

DECAY_SPAN_MAX = 60.0

_TRANS_B = (((1,), (1,)), ((), ()))
_TRANS_A = (((0,), (0,)), ((), ()))


def _row_tile(rows, limit):
    return max(t for t in range(16, min(rows, limit) + 1, 16) if rows % t == 0)


def _params(*semantics):
    return pltpu.CompilerParams(dimension_semantics=semantics, vmem_limit_bytes=VMEM_LIMIT_BYTES)


def _silu(x):
    return x * jax.nn.sigmoid(x)


def _softplus_neg_abs(x):
    return jnp.log(1.0 + jnp.exp(-jnp.abs(x)))


def _log_sigmoid(x):
    return jnp.minimum(x, 0.0) - _softplus_neg_abs(x)


def _iota(shape, dim):
    return lax.broadcasted_iota(jnp.int32, shape, dim)


def _seg_cumsum(x, seg):
    pos = _iota(x.shape, 0) & (seg - 1)
    shift = 1
    while shift < seg:
        x = x + jnp.where(pos >= shift, pltpu.roll(x, shift, 0), 0.0)
        shift *= 2
    return x


def _head_norm_gate(o, gain, gate):
    return o * lax.rsqrt(jnp.mean(o * o, axis=-1, keepdims=True) + EPS) * gain * gate


def _pick_lane(x, lane_mask):
    return jnp.sum(jnp.where(lane_mask, x, 0.0), axis=-1, keepdims=True)


def _put_row(tile, j, new):
    return jnp.where(_iota(tile.shape, 0) == j, new, tile)


def _stack_rows_kernel(a_ref, b_ref, g_ref, o_ref, n_ref, *, n_a):
    i = pl.program_id(0)

    def emit(x, rows):
        o_ref[rows, :] = x
        y = x * lax.rsqrt(jnp.mean(x * x, axis=-1, keepdims=True) + EPS)
        n_ref[rows, :] = (y * g_ref[...]).astype(n_ref.dtype)

    @pl.when(i < n_a)
    def _():
        emit(a_ref[...], slice(None))

    @pl.when(i == n_a)
    def _():
        emit(b_ref[...], slice(0, b_ref.shape[0]))


def _stack_rows(a, b, gain):
    (ma, d), mb = a.shape, b.shape[0]
    n_a = ma // STACK_TILE
    tile = pl.BlockSpec((STACK_TILE, d), lambda i: (i, 0))
    return pl.pallas_call(
        functools.partial(_stack_rows_kernel, n_a=n_a),
        out_shape=(jax.ShapeDtypeStruct((ma + mb, d), a.dtype), jax.ShapeDtypeStruct((ma + mb, d), BF16)),
        grid=(n_a + 1,),
        in_specs=[
            pl.BlockSpec((STACK_TILE, d), lambda i: (jnp.minimum(i, n_a - 1), 0)),
            pl.BlockSpec((mb, d), lambda i: (0, 0)),
            pl.BlockSpec((1, d), lambda i: (0, 0)),
        ],
        out_specs=(tile, tile),
        compiler_params=_params("arbitrary"),
        name="stack_rows",
    )(a, b, gain.reshape(1, d))


def _rmsnorm_split_kernel(x_ref, g_ref, a_ref, b_ref, *, n_a):
    i = pl.program_id(0)
    x = x_ref[...]
    y = x * lax.rsqrt(jnp.mean(x * x, axis=-1, keepdims=True) + EPS) * g_ref[...]

    @pl.when(i < n_a)
    def _():
        a_ref[...] = y

    @pl.when(i == n_a)
    def _():
        b_ref[...] = y[0:b_ref.shape[0], :]


def _rmsnorm_split(x, gain, rows_a):
    m, d = x.shape
    n_a = rows_a // STACK_TILE
    rows_b = m - rows_a
    return pl.pallas_call(
        functools.partial(_rmsnorm_split_kernel, n_a=n_a),
        out_shape=(jax.ShapeDtypeStruct((rows_a, d), F32), jax.ShapeDtypeStruct((rows_b, d), F32)),
        grid=(n_a + 1,),
        in_specs=[
            pl.BlockSpec((STACK_TILE, d), lambda i: (i, 0)),
            pl.BlockSpec((1, d), lambda i: (0, 0)),
        ],
        out_specs=(
            pl.BlockSpec((STACK_TILE, d), lambda i: (jnp.minimum(i, n_a - 1), 0)),
            pl.BlockSpec((rows_b, d), lambda i: (0, 0)),
        ),
        compiler_params=_params("arbitrary"),
        name="rmsnorm_split",
    )(x, gain.reshape(1, d))


def _matmul_kernel(*refs, n_w, mode, lhs_order, scaled, parts):
    lhs_ref = refs[0]
    w_refs = refs[1:1 + n_w]
    rest = refs[1 + n_w:]
    if scaled:
        ssq_ref, rest = rest[0], rest[1:]
    tm = lhs_ref.shape[0] if lhs_order is None else lhs_ref.shape[1]
    row_parts = [slice(p * (tm // parts), (p + 1) * (tm // parts)) for p in range(parts)]

    def lhs_rows(rs):
        if lhs_order is None:
            return lhs_ref[rs, :]
        return jnp.concatenate([lhs_ref[c, rs, :] for c in lhs_order], axis=1)

    ws = [w_ref[...].astype(BF16) for w_ref in w_refs]
    accs = [[jnp.dot(lhs_rows(rs), w, preferred_element_type=F32) for w in ws] for rs in row_parts]
    for rs, acc in zip(row_parts, accs):
        if scaled:
            total = ssq_ref[0, rs, :]
            for part in range(1, ssq_ref.shape[0]):
                total = total + ssq_ref[part, rs, :]
            r = lax.rsqrt(total[:, 0:1] * (1.0 / ws[0].shape[0]) + EPS)
            acc = [a * r for a in acc]
        if mode == "swiglu":
            (o_ref,) = rest
            gate, up = acc
            o_ref[rs, :] = (0.5 * _silu(gate) * up).astype(o_ref.dtype)
        elif mode == "resid":
            x_ref, o_ref = rest
            o_ref[rs, :] = x_ref[rs, :] + acc[0]
        elif mode == "resid_gain":
            x_ref, g_ref, o_ref, xg_ref, sq_ref = rest
            x = x_ref[rs, :] + acc[0]
            o_ref[rs, :] = x
            xg_ref[rs, :] = (x * g_ref[...]).astype(xg_ref.dtype)
            sq_ref[rs, :] = jnp.broadcast_to(jnp.sum(x * x, axis=-1, keepdims=True), (x.shape[0], LANES))
        elif mode == "resid_norm":
            x_ref, g_ref, o_ref, n_ref = rest
            x = x_ref[rs, :] + acc[0]
            o_ref[rs, :] = x
            y = x * lax.rsqrt(jnp.mean(x * x, axis=-1, keepdims=True) + EPS)
            n_ref[rs, :] = (y * g_ref[...]).astype(n_ref.dtype)
        else:
            (o_ref,) = rest
            for c in range(o_ref.shape[0]):
                o_ref[c, rs, :] = acc[0][:, c * LANES:(c + 1) * LANES]


def _matmul(lhs, weights, layer, *, mode, tm, tn, n_out, col_off=0, resid=None, norm=None, row_ssq=None,
            weight_buffers=2, lhs_order=None):
    if lhs_order is None:
        m, k = lhs.shape
        lhs_spec = pl.BlockSpec((tm, k), lambda j, i: (i, 0))
    else:
        m, k = lhs.shape[1], lhs.shape[0] * LANES
        lhs_spec = pl.BlockSpec((lhs.shape[0], tm, LANES), lambda j, i: (0, i, 0))
    n_w = len(weights)
    n_tiles = pl.cdiv(n_out, tn)
    grid = (n_tiles, m // tm)
    in_specs = [lhs_spec]
    in_specs += [pl.BlockSpec((None, k, tn), lambda j, i: (layer, 0, j + col_off),
                              pipeline_mode=pl.Buffered(weight_buffers)) for _ in weights]
    args = [lhs, *weights]
    if row_ssq is not None:
        in_specs.append(pl.BlockSpec((row_ssq.shape[0], tm, LANES), lambda j, i: (0, i, 0)))
        args.append(row_ssq)
    tile = pl.BlockSpec((tm, tn), lambda j, i: (i, j))
    if mode in ("resid", "resid_gain", "resid_norm"):
        in_specs.append(tile)
        args.append(resid)
    if mode in ("resid_gain", "resid_norm"):
        gains, norm_layer = norm
        in_specs.append(pl.BlockSpec((None, 1, tn), lambda j, i: (norm_layer, 0, j)))
        args.append(gains.reshape(-1, 1, n_out))
    if mode == "resid_gain":
        out_shape = (jax.ShapeDtypeStruct((m, n_out), F32), jax.ShapeDtypeStruct((m, n_out), BF16),
                     jax.ShapeDtypeStruct((n_tiles, m, LANES), F32))
        out_spec = (tile, tile, pl.BlockSpec((None, tm, LANES), lambda j, i: (j, i, 0)))
    elif mode == "resid_norm":
        assert tn == n_out
        out_shape = (jax.ShapeDtypeStruct((m, n_out), F32), jax.ShapeDtypeStruct((m, n_out), BF16))
        out_spec = (tile, tile)
    elif mode == "cols":
        out_shape = jax.ShapeDtypeStruct((n_out // LANES, m, LANES), F32)
        out_spec = pl.BlockSpec((tn // LANES, tm, LANES), lambda j, i: (j, i, 0))
    else:
        out_shape = jax.ShapeDtypeStruct((m, n_out), BF16 if mode == "swiglu" else F32)
        out_spec = tile
    return pl.pallas_call(
        functools.partial(_matmul_kernel, n_w=n_w, mode=mode, lhs_order=lhs_order, scaled=row_ssq is not None,
                          parts=next(p for p in ROW_PARTS if p == 1 or tm % (16 * p) == 0)),
        out_shape=out_shape,
        grid=grid,
        in_specs=in_specs,
        out_specs=out_spec,
        compiler_params=_params("arbitrary", "arbitrary"),
        name="matmul_" + mode,
    )(*args)


def _rope_kernel(inv_ref, cos_ref, sin_ref, *, pos0, step):
    shape = cos_ref.shape
    row = _iota(shape, 0) + pl.program_id(0) * shape[0]
    ang = (pos0 + step * row.astype(F32)) * inv_ref[...]
    sin = jnp.sin(ang)
    cos_ref[...] = jnp.cos(ang)
    sin_ref[...] = jnp.where(_iota(shape, 1) < HEAD_DIM // 2, -sin, sin)


def _rope_tables(rows, tile, pos0, step):
    inv = ROPE_BASE ** (-jnp.arange(0, HEAD_DIM, 2, dtype=F32) / HEAD_DIM)
    inv = jnp.concatenate([inv, inv]).reshape(1, HEAD_DIM)
    out = jax.ShapeDtypeStruct((rows, HEAD_DIM), F32)
    spec = pl.BlockSpec((tile, HEAD_DIM), lambda i: (i, 0))
    return pl.pallas_call(
        functools.partial(_rope_kernel, pos0=float(pos0), step=float(step)),
        out_shape=(out, out),
        grid=(rows // tile,),
        in_specs=[pl.BlockSpec((1, HEAD_DIM), lambda i: (0, 0))],
        out_specs=(spec, spec),
        compiler_params=_params("arbitrary"),
        name="rope_tables",
    )(inv)


def _rope(x, cos, sin):
    return x * cos + pltpu.roll(x, HEAD_DIM // 2, 1) * sin


def _mixer_call(kernel_fn, name, grid, in_specs, args, outs, scratch, semantics):
    in_specs, args, aliases = list(in_specs), list(args), {}
    for k, (_, _, _, prev) in enumerate(outs):
        if prev is not None:
            aliases[len(args)] = k
            in_specs.append(pl.BlockSpec(memory_space=pl.ANY))
            args.append(prev)
    n_alias = len(aliases)

    def body(*refs):
        n_in = len(args) - n_alias
        kernel_fn(*refs[:n_in], *refs[n_in + n_alias:])

    return pl.pallas_call(
        body,
        out_shape=tuple(jax.ShapeDtypeStruct(s, dt) for s, dt, _, _ in outs),
        grid=grid,
        in_specs=in_specs,
        out_specs=tuple(spec for _, _, spec, _ in outs),
        scratch_shapes=scratch,
        input_output_aliases=aliases,
        compiler_params=_params(*semantics),
        name=name,
    )(*args)


def _state_out(state, layer, prev):
    tail = state.shape[2:]
    zeros = (0,) * len(tail)
    spec = pl.BlockSpec((None, STEP_ROWS) + tail, lambda g: (layer, g) + zeros)
    return (state.shape, F32, spec, prev)


def _hgrn_lower_bound(logits, layer):
    mx = jnp.max(logits, axis=0, keepdims=True)
    e = jnp.exp(logits - mx)
    p = e / jnp.sum(e, axis=0, keepdims=True)
    if layer == 0:
        return jnp.zeros_like(mx)
    return jnp.sum(p[1:layer + 1], axis=0, keepdims=True)


def _hgrn_gates(fz, lb):
    log_lb = jnp.log(lb)
    y = jnp.log1p(-lb) + _log_sigmoid(fz)
    log_f = jnp.maximum(log_lb, y) + _softplus_neg_abs(log_lb - y)
    key = (1.0 - lb) / (1.0 + jnp.exp(fz))
    return log_f, key


def _hgrn_prompt_kernel(z_ref, lbl_ref, gn_ref, o_ref, s_ref, st_ref, q_scr, k_scr, b_scr, a_scr, *, layer, n_heads):
    t = pl.program_id(1)
    n_sub = BLK // SUB
    lb = _hgrn_lower_bound(lbl_ref[...], layer)
    row = _iota((BLK, LANES), 0)
    col = _iota((BLK, LANES), 1)
    pos = row & (SUB - 1)
    seg0 = row - pos

    @pl.when(t == 0)
    def _():
        st_ref[...] = jnp.zeros_like(st_ref)

    def seg_row(x, s):
        x3 = x.reshape(n_sub, SUB, LANES)
        return jnp.broadcast_to(x3[:, s:s + 1, :], x3.shape).reshape(BLK, LANES)

    def block(i, carry):
        rows = pl.ds(pl.multiple_of(i * BLK, BLK), BLK)
        span = jnp.zeros((BLK, LANES), F32)
        for h in range(n_heads):
            log_f, k = _hgrn_gates(z_ref[n_heads + h, rows, :], lb[:, h * LANES:(h + 1) * LANES])
            b = _seg_cumsum(log_f, SUB)
            q_scr[h] = _silu(z_ref[h, rows, :])
            k_scr[h] = k
            b_scr[h] = b
            span = jnp.maximum(span, -seg_row(b, SUB - 1))
        mild = jnp.max(span) <= DECAY_SPAN_MAX

        @pl.when(mild)
        def _():
            q_end, k_end = [], []
            for h in range(n_heads):
                b = b_scr[h]
                b_end = seg_row(b, SUB - 1)
                q_end.append((q_scr[h] * jnp.exp(b - b_end)).astype(BF16))
                k_end.append((k_scr[h] * jnp.exp(b_end - b)).astype(BF16))
            a = [lax.dot_general(q_end[h], k_end[h], _TRANS_B, preferred_element_type=F32) for h in range(n_heads)]
            for h in range(n_heads):
                a_scr[h] = jnp.where((col >= seg0) & (col <= row), a[h], 0.0)

        @pl.when(jnp.logical_not(mild))
        def _():
            for h in range(n_heads):
                q, k, b = q_scr[h], k_scr[h], b_scr[h]
                a = jnp.zeros((BLK, LANES), F32)
                for s in range(SUB):
                    e = jnp.exp(jnp.where(pos >= s, b - seg_row(b, s), -jnp.inf))
                    c = jnp.sum(q * seg_row(k, s) * e, axis=-1, keepdims=True)
                    a = jnp.where(col == seg0 + s, c, a)
                a_scr[h] = a

        heads = range(n_heads)
        subs = [slice(n * SUB, (n + 1) * SUB) for n in range(n_sub)]
        vb, q_in, k_out, lam = [], [], [], []
        for h in heads:
            b = b_scr[h]
            b_end = seg_row(b, SUB - 1)
            vb.append(z_ref[2 * n_heads + h, rows, :].astype(BF16))
            q_in.append((q_scr[h] * jnp.exp(b)).astype(BF16))
            k_out.append((k_scr[h] * jnp.exp(b_end - b)).astype(BF16))
            lam.append(jnp.exp(b_end))
        ups = [[lax.dot_general(vb[h][sl], k_out[h][sl], _TRANS_A, preferred_element_type=F32) for sl in subs]
               for h in heads]
        o = [jnp.dot(a_scr[h].astype(BF16), vb[h], preferred_element_type=F32) for h in heads]
        states = []
        for h in heads:
            chain = [st_ref[h]]
            for n in range(n_sub):
                chain.append(chain[n] * lam[h][n * SUB:n * SUB + 1, :] + ups[h][n])
            st_ref[h] = chain[n_sub]
            states.append([s.astype(BF16) for s in chain[:n_sub]])
        parts = [[lax.dot_general(q_in[h][sl], states[h][n], _TRANS_B, preferred_element_type=F32)
                  for n, sl in enumerate(subs)] for h in heads]
        for h in heads:
            gate = _silu(z_ref[3 * n_heads + h, rows, :])
            out = _head_norm_gate(o[h] + jnp.concatenate(parts[h], axis=0), gn_ref[:, h * LANES:(h + 1) * LANES], gate)
            o_ref[h, rows, :] = out.astype(o_ref.dtype)
        return carry

    lax.fori_loop(0, z_ref.shape[1] // BLK, block, 0)

    @pl.when(t == pl.num_programs(1) - 1)
    def _():
        for h in range(n_heads):
            s_ref[h] = st_ref[h].T


def _hgrn_step_kernel(z_ref, s_ref, lbl_ref, gn_ref, o_ref, so_ref, f_scr, k_scr, *, layer, n_heads):
    g = pl.program_id(0)
    n_rows = z_ref.shape[1]

    @pl.when(g == 0)
    def _():
        lb = _hgrn_lower_bound(lbl_ref[...], layer)
        for h in range(n_heads):
            log_f, k = _hgrn_gates(z_ref[n_heads + h], lb[:, h * LANES:(h + 1) * LANES])
            f_scr[h] = jnp.exp(log_f).T
            k_scr[h] = k.T

    lane = _iota((HEAD_DIM, n_rows), 1)
    grp = pl.ds(pl.multiple_of(g * STEP_ROWS, STEP_ROWS), STEP_ROWS)
    q = [_silu(z_ref[h, grp, :]).astype(BF16) for h in range(n_heads)]

    def body(j, outs):
        b = g * STEP_ROWS + j
        sel = lane == b
        one = pl.ds(b, 1)
        new = []
        for h in range(n_heads):
            s_new = _pick_lane(f_scr[h], sel) * s_ref[j, h] + _pick_lane(k_scr[h], sel) * z_ref[2 * n_heads + h, one, :]
            so_ref[j, h] = s_new
            new.append(_put_row(outs[h], j, jnp.dot(q[h], s_new.astype(BF16), preferred_element_type=F32)))
        return tuple(new)

    zero = jnp.zeros((STEP_ROWS, LANES), F32)
    outs = lax.fori_loop(0, STEP_ROWS, body, (zero,) * n_heads, unroll=STEP_UNROLL)
    for h in range(n_heads):
        gate = _silu(z_ref[3 * n_heads + h, grp, :])
        o_ref[h] = _head_norm_gate(outs[h], gn_ref[:, h * LANES:(h + 1) * LANES], gate).astype(o_ref.dtype)


def _ret_prompt_kernel(z_ref, cos_ref, sin_ref, lg_ref, gn_ref, o_ref, s_ref, st_ref, dec_ref, *, n_heads):
    t = pl.program_id(1)

    @pl.when((pl.program_id(0) == 0) & (t == 0))
    def _():
        row = _iota((BLK, LANES), 0).astype(F32)
        diff = row - _iota((BLK, LANES), 1).astype(F32)
        for h in range(n_heads):
            lg = lg_ref[h]
            dec_ref[h, 0] = jnp.where(diff >= 0.0, jnp.exp(diff * lg), 0.0)
            dec_ref[h, 1] = jnp.exp((row + 1.0) * lg)
            dec_ref[h, 2] = jnp.exp((BLK - 1.0 - row) * lg)

    @pl.when(t == 0)
    def _():
        st_ref[...] = jnp.zeros_like(st_ref)

    def chunk(i, carry):
        rows = pl.ds(pl.multiple_of(i * BLK, BLK), BLK)
        tab = pl.ds(pl.multiple_of(t * z_ref.shape[1] + i * BLK, BLK), BLK)
        cos = cos_ref[tab, :]
        sin = sin_ref[tab, :]
        heads = range(n_heads)
        q = [_rope(z_ref[h, rows, :], cos, sin) for h in heads]
        k = [_rope(z_ref[n_heads + h, rows, :], cos, sin) * K_SCALE for h in heads]
        vb = [z_ref[2 * n_heads + h, rows, :].astype(BF16) for h in heads]
        a = [lax.dot_general(q[h].astype(BF16), k[h].astype(BF16), _TRANS_B, preferred_element_type=F32)
             for h in heads]
        carried = [jnp.dot((q[h] * dec_ref[h, 1]).astype(BF16), st_ref[h].astype(BF16), preferred_element_type=F32)
                   for h in heads]
        update = [jnp.dot((k[h] * dec_ref[h, 2]).T.astype(BF16), vb[h], preferred_element_type=F32) for h in heads]
        o = [jnp.dot((a[h] * dec_ref[h, 0]).astype(BF16), vb[h], preferred_element_type=F32) for h in heads]
        for h in heads:
            s_dec = jnp.exp(jnp.full((1, LANES), BLK, F32) * lg_ref[h])
            st_ref[h] = s_dec * st_ref[h] + update[h]
            gate = _silu(z_ref[3 * n_heads + h, rows, :])
            out = _head_norm_gate(o[h] + carried[h], gn_ref[:, h * LANES:(h + 1) * LANES], gate)
            o_ref[h, rows, :] = out.astype(o_ref.dtype)
        return carry

    lax.fori_loop(0, z_ref.shape[1] // BLK, chunk, 0)

    @pl.when(t == pl.num_programs(1) - 1)
    def _():
        s_ref[...] = st_ref[...]


def _ret_step_kernel(z_ref, s_ref, cos_ref, sin_ref, lg_ref, gn_ref, o_ref, so_ref, k_scr, *, n_heads):
    g = pl.program_id(0)
    n_rows = z_ref.shape[1]
    cos = cos_ref[0:1, :]
    sin = sin_ref[0:1, :]

    @pl.when(g == 0)
    def _():
        for h in range(n_heads):
            k_scr[h] = (_rope(z_ref[n_heads + h], cos, sin) * K_SCALE).T

    lane = _iota((HEAD_DIM, n_rows), 1)
    grp = pl.ds(pl.multiple_of(g * STEP_ROWS, STEP_ROWS), STEP_ROWS)
    q = [_rope(z_ref[h, grp, :], cos, sin).astype(BF16) for h in range(n_heads)]

    def body(j, outs):
        b = g * STEP_ROWS + j
        sel = lane == b
        one = pl.ds(b, 1)
        new = []
        for h in range(n_heads):
            gamma = jnp.exp(jnp.full((1, LANES), lg_ref[h], F32))
            s_new = gamma * s_ref[j, h] + _pick_lane(k_scr[h], sel) * z_ref[2 * n_heads + h, one, :]
            so_ref[j, h] = s_new
            new.append(_put_row(outs[h], j, jnp.dot(q[h], s_new.astype(BF16), preferred_element_type=F32)))
        return tuple(new)

    zero = jnp.zeros((STEP_ROWS, LANES), F32)
    outs = lax.fori_loop(0, STEP_ROWS, body, (zero,) * n_heads, unroll=STEP_UNROLL)
    for h in range(n_heads):
        gate = _silu(z_ref[3 * n_heads + h, grp, :])
        o_ref[h] = _head_norm_gate(outs[h], gn_ref[:, h * LANES:(h + 1) * LANES], gate).astype(o_ref.dtype)


def _mlstm_prompt_kernel(z_ref, zg_ref, bias_ref, gn_ref, o_ref, c_ref, n_ref, m_ref, cn_scr, m_scr, *, n_heads):
    t = pl.program_id(1)
    lane = _iota((BLK, LANES), 1)
    causal = lane <= _iota((BLK, LANES), 0)
    ones = jnp.ones((BLK, LANES), BF16)

    def twice(x):
        return jnp.concatenate([x, x], axis=1)

    @pl.when(t == 0)
    def _():
        cn_scr[...] = jnp.zeros_like(cn_scr)
        m_scr[...] = jnp.zeros_like(m_scr)

    def chunk(i, carry):
        rows = pl.ds(pl.multiple_of(i * BLK, BLK), BLK)
        gates = zg_ref[rows, :] + bias_ref[...]
        log_f_all = pltpu.roll(_log_sigmoid(gates), LANES - n_heads, 1)
        bt_all = _seg_cumsum(log_f_all, BLK)
        w_t = (gates - bt_all).T
        heads = range(n_heads)
        d_mat, w_inter, m_t, w_c, kw_t, qb, kb, v_ones = [], [], [], [], [], [], [], []
        for h in heads:
            bt = _pick_lane(bt_all, lane == h)
            w_row = w_t[h:h + 1, :]
            log_d = jnp.where(causal, bt + w_row, -jnp.inf)
            m_prev = m_scr[h]
            log_inter = bt + m_prev
            m_t.append(jnp.maximum(log_inter, jnp.max(log_d, axis=-1, keepdims=True)))
            d_mat.append(jnp.exp(log_d - m_t[h]))
            w_inter.append(jnp.exp(log_inter - m_t[h]))
            m_end = m_t[h][BLK - 1:BLK, :]
            b_end = bt[BLK - 1:BLK, :]
            k = z_ref[n_heads + h, rows, :] * K_SCALE
            kb.append(k.astype(BF16))
            kw_t.append((k.T * jnp.exp(b_end + w_row - m_end)).astype(BF16))
            w_c.append(jnp.exp(b_end + m_prev - m_end))
            qb.append(z_ref[h, rows, :].astype(BF16))
            v_ones.append(jnp.concatenate([z_ref[2 * n_heads + h, rows, :].astype(BF16), ones], axis=1))
        sc = [lax.dot_general(qb[h], kb[h], _TRANS_B, preferred_element_type=F32) for h in heads]
        carried = [jnp.dot(qb[h], cn_scr[h].astype(BF16), preferred_element_type=F32) for h in heads]
        update = [jnp.dot(kw_t[h], v_ones[h], preferred_element_type=F32) for h in heads]
        both = [jnp.dot((sc[h] * d_mat[h]).astype(BF16), v_ones[h], preferred_element_type=F32) for h in heads]
        for h in heads:
            tot = both[h] + carried[h] * twice(w_inter[h])
            num, den = tot[:, :LANES], tot[:, LANES:]
            hid = num / jnp.maximum(jnp.abs(den), jnp.exp(-m_t[h]))
            cn_scr[h] = twice(w_c[h]) * cn_scr[h] + update[h]
            m_scr[h] = m_t[h][BLK - 1:BLK, :]
            gate = jax.nn.sigmoid(z_ref[3 * n_heads + h, rows, :])
            out = _head_norm_gate(hid, gn_ref[:, h * LANES:(h + 1) * LANES], gate)
            o_ref[h, rows, :] = out.astype(o_ref.dtype)
        return carry

    lax.fori_loop(0, z_ref.shape[1] // BLK, chunk, 0)

    @pl.when(t == pl.num_programs(1) - 1)
    def _():
        for h in range(n_heads):
            c_ref[h] = cn_scr[h, :, :LANES]
            n_ref[h] = cn_scr[h, :, LANES:].T[0:1, :]
        m_ref[...] = m_scr[...]


def _mlstm_step_kernel(z_ref, zg_ref, bias_ref, c_ref, n_ref, m_ref, gn_ref,
                       o_ref, co_ref, no_ref, mo_ref, k_scr, wc_scr, wk_scr, em_scr, *, n_heads):
    g = pl.program_id(0)
    n_rows = z_ref.shape[1]

    @pl.when(g == 0)
    def _():
        lanes = _iota((n_rows, LANES), 1)
        gates = zg_ref[...] + bias_ref[...]
        log_f = pltpu.roll(_log_sigmoid(gates), LANES - n_heads, 1)
        m_old = m_ref[...]
        m_new = jnp.maximum(log_f + m_old, gates)
        mo_ref[...] = m_new
        w_c = jnp.exp(log_f + m_old - m_new)
        w_k = jnp.exp(gates - m_new)
        e_m = jnp.exp(-m_new)
        for h in range(n_heads):
            sel = lanes == h
            wc_scr[h] = jnp.broadcast_to(_pick_lane(w_c, sel), (n_rows, LANES))
            wk_scr[h] = jnp.broadcast_to(_pick_lane(w_k, sel), (n_rows, LANES))
            em_scr[h] = jnp.broadcast_to(_pick_lane(e_m, sel), (n_rows, LANES))
            k_scr[h] = (z_ref[n_heads + h] * K_SCALE).T

    lane = _iota((HEAD_DIM, n_rows), 1)
    grp = pl.ds(pl.multiple_of(g * STEP_ROWS, STEP_ROWS), STEP_ROWS)
    q = [z_ref[h, grp, :].astype(BF16) for h in range(n_heads)]

    def body(j, outs):
        b = g * STEP_ROWS + j
        sel = lane == b
        one = pl.ds(b, 1)
        new = []
        for h in range(n_heads):
            w_c = wc_scr[h, one, :]
            w_k = wk_scr[h, one, :]
            c_new = w_c * c_ref[j, h] + (w_k * _pick_lane(k_scr[h], sel)) * z_ref[2 * n_heads + h, one, :]
            co_ref[j, h] = c_new
            n_new = w_c * n_ref[j, h:h + 1, :] + w_k * (z_ref[n_heads + h, one, :] * K_SCALE)
            no_ref[j, h:h + 1, :] = n_new
            num = jnp.dot(q[h], c_new.astype(BF16), preferred_element_type=F32)
            new.append(_put_row(outs[2 * h], j, num))
            new.append(_put_row(outs[2 * h + 1], j, n_new))
        return tuple(new)

    zero = jnp.zeros((STEP_ROWS, LANES), F32)
    outs = lax.fori_loop(0, STEP_ROWS, body, (zero,) * (2 * n_heads), unroll=STEP_UNROLL)
    for h in range(n_heads):
        num, n_new = outs[2 * h], outs[2 * h + 1]
        den = jnp.sum(z_ref[h, grp, :] * n_new, axis=-1, keepdims=True)
        hid = num / jnp.maximum(jnp.abs(den), em_scr[h, grp, :])
        gate = jax.nn.sigmoid(z_ref[3 * n_heads + h, grp, :])
        o_ref[h] = _head_norm_gate(hid, gn_ref[:, h * LANES:(h + 1) * LANES], gate).astype(o_ref.dtype)


def kernel(x_prompt, x_sample, state_hgrn, state_mlstm_c, state_mlstm_n, state_mlstm_m, state_ret, norm_ffn1, ffn1_w_gate, ffn1_w_up, ffn1_w_down, norm_mix, w_in, hgrn_lb_logits, mlstm_gate_bias, gn_hgrn, gn_mlstm, gn_ret, w_out, norm_ffn2, ffn2_w_gate, ffn2_w_up, ffn2_w_down, norm_final):
    bp, t_len, d = x_prompt.shape
    bs, dec_len, _ = x_sample.shape
    depth = w_in.shape[0]
    h_hgrn, h_mlstm, h_ret = state_hgrn.shape[2], state_mlstm_c.shape[2], state_ret.shape[2]
    w_hgrn, w_mlstm, w_ret = h_hgrn * HEAD_DIM, h_mlstm * HEAD_DIM, h_ret * HEAD_DIM
    d_ff = ffn1_w_gate.shape[2]
    n_prompt = bp * t_len
    m_rows = n_prompt + bs
    seq_tile = min(SEQ_TILE, t_len)
    n_seq = t_len // seq_tile
    assert dec_len == 1 and bs == LANES and n_prompt % bs == 0 and t_len % seq_tile == 0 and seq_tile % BLK == 0
    row_block = n_prompt // bs
    step_block = n_prompt // STEP_ROWS
    tm_wide = _row_tile(m_rows, WIDE_ROW_TILE)
    tm_deep = _row_tile(m_rows, DEEP_ROW_TILE)
    assert n_prompt % STACK_TILE == 0
    o_mlstm, o_ret, o_hgrn = 0, h_mlstm, h_mlstm + h_ret
    assert o_ret % h_ret == 0 and o_hgrn % h_hgrn == 0
    n_slabs = h_hgrn + h_mlstm + h_ret
    o_order = tuple(range(o_hgrn, n_slabs)) + tuple(range(o_mlstm, o_ret)) + tuple(range(o_ret, o_hgrn))

    gate0 = 4 * w_hgrn + 4 * w_mlstm
    ret0 = gate0 + 2 * h_mlstm
    w_in_ret = jnp.concatenate(
        [w_in[:, :, ret0:], jnp.pad(w_in[:, :, gate0:ret0], ((0, 0), (0, 0), (0, LANES - 2 * h_mlstm)))],
        axis=2).astype(BF16)
    w_in_hm = w_in[:, :, :gate0].astype(BF16)
    gate_slab = 4 * h_ret
    bias_pad = jnp.pad(mlstm_gate_bias, ((0, 0), (0, LANES - 2 * h_mlstm))).reshape(depth, 1, LANES)
    m_pad = jnp.pad(state_mlstm_m, ((0, 0), (0, 0), (0, LANES - h_mlstm)))
    gn_h = gn_hgrn.reshape(depth, 1, w_hgrn)
    gn_m = gn_mlstm.reshape(depth, 1, w_mlstm)
    gn_r = gn_ret.reshape(depth, 1, w_ret)
    log_gamma = jnp.log1p(-jnp.exp2(-5.0 - jnp.arange(h_ret, dtype=F32)))
    cos_p, sin_p = _rope_tables(t_len, min(ROPE_TILE, t_len), 0.0, 1.0)
    cos_s, sin_s = _rope_tables(8, 8, PAST_LEN, 0.0)

    smem = pl.BlockSpec(memory_space=pltpu.SMEM)
    o3_shape = (n_slabs, m_rows, LANES)
    head_sq = (HEAD_DIM, HEAD_DIM)

    def z_prompt(n_blocks):
        return pl.BlockSpec((n_blocks, seq_tile, LANES), lambda b, t: (0, b * n_seq + t, 0))

    def o_prompt(heads, slab0):
        return pl.BlockSpec((heads, seq_tile, LANES), lambda b, t: (slab0 // heads, b * n_seq + t, 0))

    def o_step(heads, slab0):
        return pl.BlockSpec((heads, STEP_ROWS, LANES), lambda g: (slab0 // heads, step_block + g, 0))

    x, n = _stack_rows(x_prompt.reshape(n_prompt, d), x_sample.reshape(bs, d), norm_ffn1[0])
    new_h_p, new_c_p, new_n_p, new_m_p, new_r_p = [], [], [], [], []
    new_n_s, new_m_s = [], []
    st_h = st_c = st_r = None

    def ffn(x, n, ssq, w_gate, w_up, w_down, layer, next_norm):
        mid = _matmul(n, [w_gate, w_up], layer, mode="swiglu", tm=tm_wide, tn=512, n_out=d_ff, row_ssq=ssq)
        down = dict(tm=tm_deep, tn=1024, n_out=d, resid=x, weight_buffers=1)
        if next_norm is None:
            return _matmul(mid, [w_down], layer, mode="resid", **down), None, None
        return _matmul(mid, [w_down], layer, mode="resid_gain", norm=next_norm, **down)

    ssq = None
    for layer in range(depth):
        x, n, ssq = ffn(x, n, ssq, ffn1_w_gate, ffn1_w_up, ffn1_w_down, layer, (norm_mix, layer))

        cols = dict(mode="cols", tm=tm_wide, row_ssq=ssq)
        z_h = _matmul(n, [w_in_hm], layer, tn=1024, n_out=4 * w_hgrn, **cols)
        z_m = _matmul(n, [w_in_hm], layer, tn=1024, n_out=4 * w_mlstm, col_off=4 * w_hgrn // 1024, **cols)
        z_r = _matmul(n, [w_in_ret], layer, tn=640, n_out=4 * w_ret + LANES, **cols)

        def gain2(width):
            return pl.BlockSpec((None, 1, width), lambda b, t: (layer, 0, 0))

        def gain1(width):
            return pl.BlockSpec((None, 1, width), lambda g: (layer, 0, 0))

        o3, s_hp = _mixer_call(
            functools.partial(_hgrn_prompt_kernel, layer=layer, n_heads=h_hgrn), "hgrn_prompt", (bp, n_seq),
            [z_prompt(4 * h_hgrn), pl.BlockSpec(hgrn_lb_logits.shape, lambda b, t: (0, 0)), gain2(w_hgrn)],
            [z_h, hgrn_lb_logits, gn_h],
            [(o3_shape, BF16, o_prompt(h_hgrn, o_hgrn), None),
             ((bp, h_hgrn) + head_sq, F32, pl.BlockSpec((None, h_hgrn) + head_sq, lambda b, t: (b, 0, 0, 0)), None)],
            [pltpu.VMEM((h_hgrn,) + head_sq, F32)] + [pltpu.VMEM((h_hgrn, BLK, LANES), F32) for _ in range(4)],
            ("arbitrary", "arbitrary"))

        vec = ((bp, h_mlstm, 1, LANES), F32, pl.BlockSpec((None, h_mlstm, 1, LANES), lambda b, t: (b, 0, 0, 0)), None)
        o3, c_p, n_p, m_p = _mixer_call(
            functools.partial(_mlstm_prompt_kernel, n_heads=h_mlstm), "mlstm_prompt", (bp, n_seq),
            [z_prompt(4 * h_mlstm),
             pl.BlockSpec((None, seq_tile, LANES), lambda b, t: (gate_slab, b * n_seq + t, 0)),
             pl.BlockSpec((None, 1, LANES), lambda b, t: (layer, 0, 0)), gain2(w_mlstm)],
            [z_m, z_r, bias_pad, gn_m],
            [(o3_shape, BF16, o_prompt(h_mlstm, o_mlstm), o3),
             ((bp, h_mlstm) + head_sq, F32, pl.BlockSpec((None, h_mlstm) + head_sq, lambda b, t: (b, 0, 0, 0)), None),
             vec, vec],
            [pltpu.VMEM((h_mlstm, HEAD_DIM, 2 * HEAD_DIM), F32), pltpu.VMEM((h_mlstm, 1, LANES), F32)],
            ("arbitrary", "arbitrary"))

        table = pl.BlockSpec((t_len, LANES), lambda b, t: (0, 0))
        o3, s_rp = _mixer_call(
            functools.partial(_ret_prompt_kernel, n_heads=h_ret), "ret_prompt", (bp, n_seq),
            [z_prompt(4 * h_ret), table, table, smem, gain2(w_ret)],
            [z_r, cos_p, sin_p, log_gamma, gn_r],
            [(o3_shape, BF16, o_prompt(h_ret, o_ret), o3),
             ((bp, h_ret) + head_sq, F32, pl.BlockSpec((None, h_ret) + head_sq, lambda b, t: (b, 0, 0, 0)), None)],
            [pltpu.VMEM((h_ret,) + head_sq, F32), pltpu.VMEM((h_ret, 3, BLK, LANES), F32)],
            ("arbitrary", "arbitrary"))

        def z_step(n_blocks):
            return pl.BlockSpec((n_blocks, bs, LANES), lambda g: (0, row_block, 0))

        def state_in(state):
            tail = state.shape[2:]
            zeros = (0,) * len(tail)
            return pl.BlockSpec((None, STEP_ROWS) + tail, lambda g: (layer, g) + zeros)

        tposed = pltpu.VMEM((h_hgrn, HEAD_DIM, bs), F32)
        o3, st_h = _mixer_call(
            functools.partial(_hgrn_step_kernel, layer=layer, n_heads=h_hgrn), "hgrn_step", (bs // STEP_ROWS,),
            [z_step(4 * h_hgrn), state_in(state_hgrn), pl.BlockSpec(hgrn_lb_logits.shape, lambda g: (0, 0)), gain1(w_hgrn)],
            [z_h, state_hgrn, hgrn_lb_logits, gn_h],
            [(o3_shape, BF16, o_step(h_hgrn, o_hgrn), o3), _state_out(state_hgrn, layer, st_h)],
            [tposed, tposed], ("arbitrary",))

        tposed = pltpu.VMEM((h_mlstm, HEAD_DIM, bs), F32)
        rowrep = pltpu.VMEM((h_mlstm, bs, LANES), F32)
        tile = pl.BlockSpec((bs, LANES), lambda g: (0, 0))
        o3, st_c, n_s, m_s = _mixer_call(
            functools.partial(_mlstm_step_kernel, n_heads=h_mlstm), "mlstm_step", (bs // STEP_ROWS,),
            [z_step(4 * h_mlstm), pl.BlockSpec((None, bs, LANES), lambda g: (gate_slab, row_block, 0)),
             pl.BlockSpec((None, 1, LANES), lambda g: (layer, 0, 0)),
             state_in(state_mlstm_c), state_in(state_mlstm_n), tile, gain1(w_mlstm)],
            [z_m, z_r, bias_pad, state_mlstm_c, state_mlstm_n, m_pad[layer], gn_m],
            [(o3_shape, BF16, o_step(h_mlstm, o_mlstm), o3), _state_out(state_mlstm_c, layer, st_c),
             (state_mlstm_n.shape[1:], F32, pl.BlockSpec((STEP_ROWS, h_mlstm, HEAD_DIM), lambda g: (g, 0, 0)), None),
             ((bs, LANES), F32, tile, None)],
            [tposed, rowrep, rowrep, rowrep], ("arbitrary",))

        tposed = pltpu.VMEM((h_ret, HEAD_DIM, bs), F32)
        table = pl.BlockSpec(cos_s.shape, lambda g: (0, 0))
        o3, st_r = _mixer_call(
            functools.partial(_ret_step_kernel, n_heads=h_ret), "ret_step", (bs // STEP_ROWS,),
            [z_step(4 * h_ret), state_in(state_ret), table, table, smem, gain1(w_ret)],
            [z_r, state_ret, cos_s, sin_s, log_gamma, gn_r],
            [(o3_shape, BF16, o_step(h_ret, o_ret), o3), _state_out(state_ret, layer, st_r)],
            [tposed], ("arbitrary",))

        new_h_p.append(s_hp)
        new_c_p.append(c_p)
        new_n_p.append(n_p[:, :, 0, :])
        new_m_p.append(m_p[:, :, 0, 0])
        new_r_p.append(s_rp)
        new_n_s.append(n_s)
        new_m_s.append(m_s[:, :h_mlstm])

        x, n = _matmul(o3, [w_out], layer, mode="resid_norm", tm=tm_deep, tn=d, n_out=d, resid=x,
                       norm=(norm_ffn2, layer), weight_buffers=1, lhs_order=o_order)

        next_norm = (norm_ffn1, layer + 1) if layer + 1 < depth else None
        x, n, ssq = ffn(x, n, None, ffn2_w_gate, ffn2_w_up, ffn2_w_down, layer, next_norm)

    y_p, y_s = _rmsnorm_split(x, norm_final, n_prompt)
    return (
        y_p.reshape(bp, t_len, d), y_s.reshape(bs, dec_len, d),
        jnp.stack(new_h_p), jnp.stack(new_c_p), jnp.stack(new_n_p), jnp.stack(new_m_p), jnp.stack(new_r_p),
        st_h, st_c, jnp.stack(new_n_s), jnp.stack(new_m_s), st_r,
    )
```

```python
import functools

import jax
import jax.numpy as jnp
from jax import lax
from jax.experimental import pallas as pl
from jax.experimental.pallas import tpu as pltpu

F32 = jnp.float32
BF16 = jnp.bfloat16

HEAD_DIM = 128
LANES = 128
EPS = 1e-6
ROPE_BASE = 10000.0
K_SCALE = HEAD_DIM ** -0.5
PAST_LEN = 16384

BLK = 128
SUB = 32
SEQ_TILE = 1024
STEP_ROWS = 16
STEP_UNROLL = 4
STACK_TILE = 1024
ROW_PARTS = (5, 2, 1)
WIDE_ROW_TILE = 2080
DEEP_ROW_TILE = 640
ROPE_TILE = 256
VMEM_LIMIT_BYTES = 60 * 1024 * 1024

DECAY_SPAN_MAX = 60.0

_TRANS_B = (((1,), (1,)), ((), ()))
_TRANS_A = (((0,), (0,)), ((), ()))


def _row_tile(rows, limit):
    return max(t for t in range(16, min(rows, limit) + 1, 16) if rows % t == 0)


def _params(*semantics):
    return pltpu.CompilerParams(dimension_semantics=semantics, vmem_limit_bytes=VMEM_LIMIT_BYTES)


def _silu(x):
    return x * jax.nn.sigmoid(x)


def _softplus_neg_abs(x):
    return jnp.log(1.0 + jnp.exp(-jnp.abs(x)))


def _log_sigmoid(x):
    return jnp.minimum(x, 0.0) - _softplus_neg_abs(x)


def _iota(shape, dim):
    return lax.broadcasted_iota(jnp.int32, shape, dim)


def _seg_cumsum(x, seg):
    pos = _iota(x.shape, 0) & (seg - 1)
    shift = 1
    while shift < seg:
        x = x + jnp.where(pos >= shift, pltpu.roll(x, shift, 0), 0.0)
        shift *= 2
    return x


def _head_norm_gate(o, gain, gate):
    return o * lax.rsqrt(jnp.mean(o * o, axis=-1, keepdims=True) + EPS) * gain * gate


def _pick_lane(x, lane_mask):
    return jnp.sum(jnp.where(lane_mask, x, 0.0), axis=-1, keepdims=True)


def _put_row(tile, j, new):
    return jnp.where(_iota(tile.shape, 0) == j, new, tile)


def _stack_rows_kernel(a_ref, b_ref, g_ref, o_ref, n_ref, *, n_a):
    i = pl.program_id(0)

    def emit(x, rows):
        o_ref[rows, :] = x
        y = x * lax.rsqrt(jnp.mean(x * x, axis=-1, keepdims=True) + EPS)
        n_ref[rows, :] = (y * g_ref[...]).astype(n_ref.dtype)

    @pl.when(i < n_a)
    def _():
        emit(a_ref[...], slice(None))

    @pl.when(i == n_a)
    def _():
        emit(b_ref[...], slice(0, b_ref.shape[0]))


def _stack_rows(a, b, gain):
    (ma, d), mb = a.shape, b.shape[0]
    n_a = ma // STACK_TILE
    tile = pl.BlockSpec((STACK_TILE, d), lambda i: (i, 0))
    return pl.pallas_call(
        functools.partial(_stack_rows_kernel, n_a=n_a),
        out_shape=(jax.ShapeDtypeStruct((ma + mb, d), a.dtype), jax.ShapeDtypeStruct((ma + mb, d), BF16)),
        grid=(n_a + 1,),
        in_specs=[
            pl.BlockSpec((STACK_TILE, d), lambda i: (jnp.minimum(i, n_a - 1), 0)),
            pl.BlockSpec((mb, d), lambda i: (0, 0)),
            pl.BlockSpec((1, d), lambda i: (0, 0)),
        ],
        out_specs=(tile, tile),
        compiler_params=_params("arbitrary"),
        name="stack_rows",
    )(a, b, gain.reshape(1, d))


def _rmsnorm_split_kernel(x_ref, g_ref, a_ref, b_ref, *, n_a):
    i = pl.program_id(0)
    x = x_ref[...]
    y = x * lax.rsqrt(jnp.mean(x * x, axis=-1, keepdims=True) + EPS) * g_ref[...]

    @pl.when(i < n_a)
    def _():
        a_ref[...] = y

    @pl.when(i == n_a)
    def _():
        b_ref[...] = y[0:b_ref.shape[0], :]


def _rmsnorm_split(x, gain, rows_a):
    m, d = x.shape
    n_a = rows_a // STACK_TILE
    rows_b = m - rows_a
    return pl.pallas_call(
        functools.partial(_rmsnorm_split_kernel, n_a=n_a),
        out_shape=(jax.ShapeDtypeStruct((rows_a, d), F32), jax.ShapeDtypeStruct((rows_b, d), F32)),
        grid=(n_a + 1,),
        in_specs=[
            pl.BlockSpec((STACK_TILE, d), lambda i: (i, 0)),
            pl.BlockSpec((1, d), lambda i: (0, 0)),
        ],
        out_specs=(
            pl.BlockSpec((STACK_TILE, d), lambda i: (jnp.minimum(i, n_a - 1), 0)),
            pl.BlockSpec((rows_b, d), lambda i: (0, 0)),
        ),
        compiler_params=_params("arbitrary"),
        name="rmsnorm_split",
    )(x, gain.reshape(1, d))


def _matmul_kernel(*refs, n_w, mode, lhs_order, scaled, parts):
    lhs_ref = refs[0]
    w_refs = refs[1:1 + n_w]
    rest = refs[1 + n_w:]
    if scaled:
        ssq_ref, rest = rest[0], rest[1:]
    tm = lhs_ref.shape[0] if lhs_order is None else lhs_ref.shape[1]
    row_parts = [slice(p * (tm // parts), (p + 1) * (tm // parts)) for p in range(parts)]

    def lhs_rows(rs):
        if lhs_order is None:
            return lhs_ref[rs, :]
        return jnp.concatenate([lhs_ref[c, rs, :] for c in lhs_order], axis=1)

    ws = [w_ref[...].astype(BF16) for w_ref in w_refs]
    accs = [[jnp.dot(lhs_rows(rs), w, preferred_element_type=F32) for w in ws] for rs in row_parts]
    for rs, acc in zip(row_parts, accs):
        if scaled:
            total = ssq_ref[0, rs, :]
            for part in range(1, ssq_ref.shape[0]):
                total = total + ssq_ref[part, rs, :]
            r = lax.rsqrt(total[:, 0:1] * (1.0 / ws[0].shape[0]) + EPS)
            acc = [a * r for a in acc]
        if mode == "swiglu":
            (o_ref,) = rest
            gate, up = acc
            o_ref[rs, :] = (0.5 * _silu(gate) * up).astype(o_ref.dtype)
        elif mode == "resid":
            x_ref, o_ref = rest
            o_ref[rs, :] = x_ref[rs, :] + acc[0]
        elif mode == "resid_gain":
            x_ref, g_ref, o_ref, xg_ref, sq_ref = rest
            x = x_ref[rs, :] + acc[0]
            o_ref[rs, :] = x
            xg_ref[rs, :] = (x * g_ref[...]).astype(xg_ref.dtype)
            sq_ref[rs, :] = jnp.broadcast_to(jnp.sum(x * x, axis=-1, keepdims=True), (x.shape[0], LANES))
        elif mode == "resid_norm":
            x_ref, g_ref, o_ref, n_ref = rest
            x = x_ref[rs, :] + acc[0]
            o_ref[rs, :] = x
            y = x * lax.rsqrt(jnp.mean(x * x, axis=-1, keepdims=True) + EPS)
            n_ref[rs, :] = (y * g_ref[...]).astype(n_ref.dtype)
        else:
            (o_ref,) = rest
            for c in range(o_ref.shape[0]):
                o_ref[c, rs, :] = acc[0][:, c * LANES:(c + 1) * LANES]


def _matmul(lhs, weights, layer, *, mode, tm, tn, n_out, col_off=0, resid=None, norm=None, row_ssq=None,
            weight_buffers=2, lhs_order=None):
    if lhs_order is None:
        m, k = lhs.shape
        lhs_spec = pl.BlockSpec((tm, k), lambda j, i: (i, 0))
    else:
        m, k = lhs.shape[1], lhs.shape[0] * LANES
        lhs_spec = pl.BlockSpec((lhs.shape[0], tm, LANES), lambda j, i: (0, i, 0))
    n_w = len(weights)
    n_tiles = pl.cdiv(n_out, tn)
    grid = (n_tiles, m // tm)
    in_specs = [lhs_spec]
    in_specs += [pl.BlockSpec((None, k, tn), lambda j, i: (layer, 0, j + col_off),
                              pipeline_mode=pl.Buffered(weight_buffers)) for _ in weights]
    args = [lhs, *weights]
    if row_ssq is not None:
        in_specs.append(pl.BlockSpec((row_ssq.shape[0], tm, LANES), lambda j, i: (0, i, 0)))
        args.append(row_ssq)
    tile = pl.BlockSpec((tm, tn), lambda j, i: (i, j))
    if mode in ("resid", "resid_gain", "resid_norm"):
        in_specs.append(tile)
        args.append(resid)
    if mode in ("resid_gain", "resid_norm"):
        gains, norm_layer = norm
        in_specs.append(pl.BlockSpec((None, 1, tn), lambda j, i: (norm_layer, 0, j)))
        args.append(gains.reshape(-1, 1, n_out))
    if mode == "resid_gain":
        out_shape = (jax.ShapeDtypeStruct((m, n_out), F32), jax.ShapeDtypeStruct((m, n_out), BF16),
                     jax.ShapeDtypeStruct((n_tiles, m, LANES), F32))
        out_spec = (tile, tile, pl.BlockSpec((None, tm, LANES), lambda j, i: (j, i, 0)))
    elif mode == "resid_norm":
        assert tn == n_out
        out_shape = (jax.ShapeDtypeStruct((m, n_out), F32), jax.ShapeDtypeStruct((m, n_out), BF16))
        out_spec = (tile, tile)
    elif mode == "cols":
        out_shape = jax.ShapeDtypeStruct((n_out // LANES, m, LANES), F32)
        out_spec = pl.BlockSpec((tn // LANES, tm, LANES), lambda j, i: (j, i, 0))
    else:
        out_shape = jax.ShapeDtypeStruct((m, n_out), BF16 if mode == "swiglu" else F32)
        out_spec = tile
    return pl.pallas_call(
        functools.partial(_matmul_kernel, n_w=n_w, mode=mode, lhs_order=lhs_order, scaled=row_ssq is not None,
                          parts=next(p for p in ROW_PARTS if p == 1 or tm % (16 * p) == 0)),
        out_shape=out_shape,
        grid=grid,
        in_specs=in_specs,
        out_specs=out_spec,
        compiler_params=_params("arbitrary", "arbitrary"),
        name="matmul_" + mode,
    )(*args)


def _rope_kernel(inv_ref, cos_ref, sin_ref, *, pos0, step):
    shape = cos_ref.shape
    row = _iota(shape, 0) + pl.program_id(0) * shape[0]
    ang = (pos0 + step * row.astype(F32)) * inv_ref[...]
    sin = jnp.sin(ang)
    cos_ref[...] = jnp.cos(ang)
    sin_ref[...] = jnp.where(_iota(shape, 1) < HEAD_DIM // 2, -sin, sin)


def _rope_tables(rows, tile, pos0, step):
    inv = ROPE_BASE ** (-jnp.arange(0, HEAD_DIM, 2, dtype=F32) / HEAD_DIM)
    inv = jnp.concatenate([inv, inv]).reshape(1, HEAD_DIM)
    out = jax.ShapeDtypeStruct((rows, HEAD_DIM), F32)
    spec = pl.BlockSpec((tile, HEAD_DIM), lambda i: (i, 0))
    return pl.pallas_call(
        functools.partial(_rope_kernel, pos0=float(pos0), step=float(step)),
        out_shape=(out, out),
        grid=(rows // tile,),
        in_specs=[pl.BlockSpec((1, HEAD_DIM), lambda i: (0, 0))],
        out_specs=(spec, spec),
        compiler_params=_params("arbitrary"),
        name="rope_tables",
    )(inv)


def _rope(x, cos, sin):
    return x * cos + pltpu.roll(x, HEAD_DIM // 2, 1) * sin


def _mixer_call(kernel_fn, name, grid, in_specs, args, outs, scratch, semantics):
    in_specs, args, aliases = list(in_specs), list(args), {}
    for k, (_, _, _, prev) in enumerate(outs):
        if prev is not None:
            aliases[len(args)] = k
            in_specs.append(pl.BlockSpec(memory_space=pl.ANY))
            args.append(prev)
    n_alias = len(aliases)

    def body(*refs):
        n_in = len(args) - n_alias
        kernel_fn(*refs[:n_in], *refs[n_in + n_alias:])

    return pl.pallas_call(
        body,
        out_shape=tuple(jax.ShapeDtypeStruct(s, dt) for s, dt, _, _ in outs),
        grid=grid,
        in_specs=in_specs,
        out_specs=tuple(spec for _, _, spec, _ in outs),
        scratch_shapes=scratch,
        input_output_aliases=aliases,
        compiler_params=_params(*semantics),
        name=name,
    )(*args)


def _state_out(state, layer, prev):
    tail = state.shape[2:]
    zeros = (0,) * len(tail)
    spec = pl.BlockSpec((None, STEP_ROWS) + tail, lambda g: (layer, g) + zeros)
    return (state.shape, F32, spec, prev)


def _hgrn_lower_bound(logits, layer):
    mx = jnp.max(logits, axis=0, keepdims=True)
    e = jnp.exp(logits - mx)
    p = e / jnp.sum(e, axis=0, keepdims=True)
    if layer == 0:
        return jnp.zeros_like(mx)
    return jnp.sum(p[1:layer + 1], axis=0, keepdims=True)


def _hgrn_gates(fz, lb):
    log_lb = jnp.log(lb)
    y = jnp.log1p(-lb) + _log_sigmoid(fz)
    log_f = jnp.maximum(log_lb, y) + _softplus_neg_abs(log_lb - y)
    key = (1.0 - lb) / (1.0 + jnp.exp(fz))
    return log_f, key


def _hgrn_prompt_kernel(z_ref, lbl_ref, gn_ref, o_ref, s_ref, st_ref, q_scr, k_scr, b_scr, a_scr, *, layer, n_heads):
    t = pl.program_id(1)
    n_sub = BLK // SUB
    lb = _hgrn_lower_bound(lbl_ref[...], layer)
    row = _iota((BLK, LANES), 0)
    col = _iota((BLK, LANES), 1)
    pos = row & (SUB - 1)
    seg0 = row - pos

    @pl.when(t == 0)
    def _():
        st_ref[...] = jnp.zeros_like(st_ref)

    def seg_row(x, s):
        x3 = x.reshape(n_sub, SUB, LANES)
        return jnp.broadcast_to(x3[:, s:s + 1, :], x3.shape).reshape(BLK, LANES)

    def block(i, carry):
        rows = pl.ds(pl.multiple_of(i * BLK, BLK), BLK)
        span = jnp.zeros((BLK, LANES), F32)
        for h in range(n_heads):
            log_f, k = _hgrn_gates(z_ref[n_heads + h, rows, :], lb[:, h * LANES:(h + 1) * LANES])
            b = _seg_cumsum(log_f, SUB)
            q_scr[h] = _silu(z_ref[h, rows, :])
            k_scr[h] = k
            b_scr[h] = b
            span = jnp.maximum(span, -seg_row(b, SUB - 1))
        mild = jnp.max(span) <= DECAY_SPAN_MAX

        @pl.when(mild)
        def _():
            q_end, k_end = [], []
            for h in range(n_heads):
                b = b_scr[h]
                b_end = seg_row(b, SUB - 1)
                q_end.append((q_scr[h] * jnp.exp(b - b_end)).astype(BF16))
                k_end.append((k_scr[h] * jnp.exp(b_end - b)).astype(BF16))
            a = [lax.dot_general(q_end[h], k_end[h], _TRANS_B, preferred_element_type=F32) for h in range(n_heads)]
            for h in range(n_heads):
                a_scr[h] = jnp.where((col >= seg0) & (col <= row), a[h], 0.0)

        @pl.when(jnp.logical_not(mild))
        def _():
            for h in range(n_heads):
                q, k, b = q_scr[h], k_scr[h], b_scr[h]
                a = jnp.zeros((BLK, LANES), F32)
                for s in range(SUB):
                    e = jnp.exp(jnp.where(pos >= s, b - seg_row(b, s), -jnp.inf))
                    c = jnp.sum(q * seg_row(k, s) * e, axis=-1, keepdims=True)
                    a = jnp.where(col == seg0 + s, c, a)
                a_scr[h] = a

        heads = range(n_heads)
        subs = [slice(n * SUB, (n + 1) * SUB) for n in range(n_sub)]
        vb, q_in, k_out, lam = [], [], [], []
        for h in heads:
            b = b_scr[h]
            b_end = seg_row(b, SUB - 1)
            vb.append(z_ref[2 * n_heads + h, rows, :].astype(BF16))
            q_in.append((q_scr[h] * jnp.exp(b)).astype(BF16))
            k_out.append((k_scr[h] * jnp.exp(b_end - b)).astype(BF16))
            lam.append(jnp.exp(b_end))
        ups = [[lax.dot_general(vb[h][sl], k_out[h][sl], _TRANS_A, preferred_element_type=F32) for sl in subs]
               for h in heads]
        o = [jnp.dot(a_scr[h].astype(BF16), vb[h], preferred_element_type=F32) for h in heads]
        states = []
        for h in heads:
            chain = [st_ref[h]]
            for n in range(n_sub):
                chain.append(chain[n] * lam[h][n * SUB:n * SUB + 1, :] + ups[h][n])
            st_ref[h] = chain[n_sub]
            states.append([s.astype(BF16) for s in chain[:n_sub]])
        parts = [[lax.dot_general(q_in[h][sl], states[h][n], _TRANS_B, preferred_element_type=F32)
                  for n, sl in enumerate(subs)] for h in heads]
        for h in heads:
            gate = _silu(z_ref[3 * n_heads + h, rows, :])
            out = _head_norm_gate(o[h] + jnp.concatenate(parts[h], axis=0), gn_ref[:, h * LANES:(h + 1) * LANES], gate)
            o_ref[h, rows, :] = out.astype(o_ref.dtype)
        return carry

    lax.fori_loop(0, z_ref.shape[1] // BLK, block, 0)

    @pl.when(t == pl.num_programs(1) - 1)
    def _():
        for h in range(n_heads):
            s_ref[h] = st_ref[h].T


def _hgrn_step_kernel(z_ref, s_ref, lbl_ref, gn_ref, o_ref, so_ref, f_scr, k_scr, *, layer, n_heads):
    g = pl.program_id(0)
    n_rows = z_ref.shape[1]

    @pl.when(g == 0)
    def _():
        lb = _hgrn_lower_bound(lbl_ref[...], layer)
        for h in range(n_heads):
            log_f, k = _hgrn_gates(z_ref[n_heads + h], lb[:, h * LANES:(h + 1) * LANES])
            f_scr[h] = jnp.exp(log_f).T
            k_scr[h] = k.T

    lane = _iota((HEAD_DIM, n_rows), 1)
    grp = pl.ds(pl.multiple_of(g * STEP_ROWS, STEP_ROWS), STEP_ROWS)
    q = [_silu(z_ref[h, grp, :]).astype(BF16) for h in range(n_heads)]

    def body(j, outs):
        b = g * STEP_ROWS + j
        sel = lane == b
        one = pl.ds(b, 1)
        new = []
        for h in range(n_heads):
            s_new = _pick_lane(f_scr[h], sel) * s_ref[j, h] + _pick_lane(k_scr[h], sel) * z_ref[2 * n_heads + h, one, :]
            so_ref[j, h] = s_new
            new.append(_put_row(outs[h], j, jnp.dot(q[h], s_new.astype(BF16), preferred_element_type=F32)))
        return tuple(new)

    zero = jnp.zeros((STEP_ROWS, LANES), F32)
    outs = lax.fori_loop(0, STEP_ROWS, body, (zero,) * n_heads, unroll=STEP_UNROLL)
    for h in range(n_heads):
        gate = _silu(z_ref[3 * n_heads + h, grp, :])
        o_ref[h] = _head_norm_gate(outs[h], gn_ref[:, h * LANES:(h + 1) * LANES], gate).astype(o_ref.dtype)


def _ret_prompt_kernel(z_ref, cos_ref, sin_ref, lg_ref, gn_ref, o_ref, s_ref, st_ref, dec_ref, *, n_heads):
    t = pl.program_id(1)

    @pl.when((pl.program_id(0) == 0) & (t == 0))
    def _():
        row = _iota((BLK, LANES), 0).astype(F32)
        diff = row - _iota((BLK, LANES), 1).astype(F32)
        for h in range(n_heads):
            lg = lg_ref[h]
            dec_ref[h, 0] = jnp.where(diff >= 0.0, jnp.exp(diff * lg), 0.0)
            dec_ref[h, 1] = jnp.exp((row + 1.0) * lg)
            dec_ref[h, 2] = jnp.exp((BLK - 1.0 - row) * lg)

    @pl.when(t == 0)
    def _():
        st_ref[...] = jnp.zeros_like(st_ref)

    def chunk(i, carry):
        rows = pl.ds(pl.multiple_of(i * BLK, BLK), BLK)
        tab = pl.ds(pl.multiple_of(t * z_ref.shape[1] + i * BLK, BLK), BLK)
        cos = cos_ref[tab, :]
        sin = sin_ref[tab, :]
        heads = range(n_heads)
        q = [_rope(z_ref[h, rows, :], cos, sin) for h in heads]
        k = [_rope(z_ref[n_heads + h, rows, :], cos, sin) * K_SCALE for h in heads]
        vb = [z_ref[2 * n_heads + h, rows, :].astype(BF16) for h in heads]
        a = [lax.dot_general(q[h].astype(BF16), k[h].astype(BF16), _TRANS_B, preferred_element_type=F32)
             for h in heads]
        carried = [jnp.dot((q[h] * dec_ref[h, 1]).astype(BF16), st_ref[h].astype(BF16), preferred_element_type=F32)
                   for h in heads]
        update = [jnp.dot((k[h] * dec_ref[h, 2]).T.astype(BF16), vb[h], preferred_element_type=F32) for h in heads]
        o = [jnp.dot((a[h] * dec_ref[h, 0]).astype(BF16), vb[h], preferred_element_type=F32) for h in heads]
        for h in heads:
            s_dec = jnp.exp(jnp.full((1, LANES), BLK, F32) * lg_ref[h])
            st_ref[h] = s_dec * st_ref[h] + update[h]
            gate = _silu(z_ref[3 * n_heads + h, rows, :])
            out = _head_norm_gate(o[h] + carried[h], gn_ref[:, h * LANES:(h + 1) * LANES], gate)
            o_ref[h, rows, :] = out.astype(o_ref.dtype)
        return carry

    lax.fori_loop(0, z_ref.shape[1] // BLK, chunk, 0)

    @pl.when(t == pl.num_programs(1) - 1)
    def _():
        s_ref[...] = st_ref[...]


def _ret_step_kernel(z_ref, s_ref, cos_ref, sin_ref, lg_ref, gn_ref, o_ref, so_ref, k_scr, *, n_heads):
    g = pl.program_id(0)
    n_rows = z_ref.shape[1]
    cos = cos_ref[0:1, :]
    sin = sin_ref[0:1, :]

    @pl.when(g == 0)
    def _():
        for h in range(n_heads):
            k_scr[h] = (_rope(z_ref[n_heads + h], cos, sin) * K_SCALE).T

    lane = _iota((HEAD_DIM, n_rows), 1)
    grp = pl.ds(pl.multiple_of(g * STEP_ROWS, STEP_ROWS), STEP_ROWS)
    q = [_rope(z_ref[h, grp, :], cos, sin).astype(BF16) for h in range(n_heads)]

    def body(j, outs):
        b = g * STEP_ROWS + j
        sel = lane == b
        one = pl.ds(b, 1)
        new = []
        for h in range(n_heads):
            gamma = jnp.exp(jnp.full((1, LANES), lg_ref[h], F32))
            s_new = gamma * s_ref[j, h] + _pick_lane(k_scr[h], sel) * z_ref[2 * n_heads + h, one, :]
            so_ref[j, h] = s_new
            new.append(_put_row(outs[h], j, jnp.dot(q[h], s_new.astype(BF16), preferred_element_type=F32)))
        return tuple(new)

    zero = jnp.zeros((STEP_ROWS, LANES), F32)
    outs = lax.fori_loop(0, STEP_ROWS, body, (zero,) * n_heads, unroll=STEP_UNROLL)
    for h in range(n_heads):
        gate = _silu(z_ref[3 * n_heads + h, grp, :])
        o_ref[h] = _head_norm_gate(outs[h], gn_ref[:, h * LANES:(h + 1) * LANES], gate).astype(o_ref.dtype)


def _mlstm_prompt_kernel(z_ref, zg_ref, bias_ref, gn_ref, o_ref, c_ref, n_ref, m_ref, cn_scr, m_scr, *, n_heads):
    t = pl.program_id(1)
    lane = _iota((BLK, LANES), 1)
    causal = lane <= _iota((BLK, LANES), 0)
    ones = jnp.ones((BLK, LANES), BF16)

    def twice(x):
        return jnp.concatenate([x, x], axis=1)

    @pl.when(t == 0)
    def _():
        cn_scr[...] = jnp.zeros_like(cn_scr)
        m_scr[...] = jnp.zeros_like(m_scr)

    def chunk(i, carry):
        rows = pl.ds(pl.multiple_of(i * BLK, BLK), BLK)
        gates = zg_ref[rows, :] + bias_ref[...]
        log_f_all = pltpu.roll(_log_sigmoid(gates), LANES - n_heads, 1)
        bt_all = _seg_cumsum(log_f_all, BLK)
        w_t = (gates - bt_all).T
        heads = range(n_heads)
        d_mat, w_inter, m_t, w_c, kw_t, qb, kb, v_ones = [], [], [], [], [], [], [], []
        for h in heads:
            bt = _pick_lane(bt_all, lane == h)
            w_row = w_t[h:h + 1, :]
            log_d = jnp.where(causal, bt + w_row, -jnp.inf)
            m_prev = m_scr[h]
            log_inter = bt + m_prev
            m_t.append(jnp.maximum(log_inter, jnp.max(log_d, axis=-1, keepdims=True)))
            d_mat.append(jnp.exp(log_d - m_t[h]))
            w_inter.append(jnp.exp(log_inter - m_t[h]))
            m_end = m_t[h][BLK - 1:BLK, :]
            b_end = bt[BLK - 1:BLK, :]
            k = z_ref[n_heads + h, rows, :] * K_SCALE
            kb.append(k.astype(BF16))
            kw_t.append((k.T * jnp.exp(b_end + w_row - m_end)).astype(BF16))
            w_c.append(jnp.exp(b_end + m_prev - m_end))
            qb.append(z_ref[h, rows, :].astype(BF16))
            v_ones.append(jnp.concatenate([z_ref[2 * n_heads + h, rows, :].astype(BF16), ones], axis=1))
        sc = [lax.dot_general(qb[h], kb[h], _TRANS_B, preferred_element_type=F32) for h in heads]
        carried = [jnp.dot(qb[h], cn_scr[h].astype(BF16), preferred_element_type=F32) for h in heads]
        update = [jnp.dot(kw_t[h], v_ones[h], preferred_element_type=F32) for h in heads]
        both = [jnp.dot((sc[h] * d_mat[h]).astype(BF16), v_ones[h], preferred_element_type=F32) for h in heads]
        for h in heads:
            tot = both[h] + carried[h] * twice(w_inter[h])
            num, den = tot[:, :LANES], tot[:, LANES:]
            hid = num / jnp.maximum(jnp.abs(den), jnp.exp(-m_t[h]))
            cn_scr[h] = twice(w_c[h]) * cn_scr[h] + update[h]
            m_scr[h] = m_t[h][BLK - 1:BLK, :]
            gate = jax.nn.sigmoid(z_ref[3 * n_heads + h, rows, :])
            out = _head_norm_gate(hid, gn_ref[:, h * LANES:(h + 1) * LANES], gate)
            o_ref[h, rows, :] = out.astype(o_ref.dtype)
        return carry

    lax.fori_loop(0, z_ref.shape[1] // BLK, chunk, 0)

    @pl.when(t == pl.num_programs(1) - 1)
    def _():
        for h in range(n_heads):
            c_ref[h] = cn_scr[h, :, :LANES]
            n_ref[h] = cn_scr[h, :, LANES:].T[0:1, :]
        m_ref[...] = m_scr[...]


def _mlstm_step_kernel(z_ref, zg_ref, bias_ref, c_ref, n_ref, m_ref, gn_ref,
                       o_ref, co_ref, no_ref, mo_ref, k_scr, wc_scr, wk_scr, em_scr, *, n_heads):
    g = pl.program_id(0)
    n_rows = z_ref.shape[1]

    @pl.when(g == 0)
    def _():
        lanes = _iota((n_rows, LANES), 1)
        gates = zg_ref[...] + bias_ref[...]
        log_f = pltpu.roll(_log_sigmoid(gates), LANES - n_heads, 1)
        m_old = m_ref[...]
        m_new = jnp.maximum(log_f + m_old, gates)
        mo_ref[...] = m_new
        w_c = jnp.exp(log_f + m_old - m_new)
        w_k = jnp.exp(gates - m_new)
        e_m = jnp.exp(-m_new)
        for h in range(n_heads):
            sel = lanes == h
            wc_scr[h] = jnp.broadcast_to(_pick_lane(w_c, sel), (n_rows, LANES))
            wk_scr[h] = jnp.broadcast_to(_pick_lane(w_k, sel), (n_rows, LANES))
            em_scr[h] = jnp.broadcast_to(_pick_lane(e_m, sel), (n_rows, LANES))
            k_scr[h] = (z_ref[n_heads + h] * K_SCALE).T

    lane = _iota((HEAD_DIM, n_rows), 1)
    grp = pl.ds(pl.multiple_of(g * STEP_ROWS, STEP_ROWS), STEP_ROWS)
    q = [z_ref[h, grp, :].astype(BF16) for h in range(n_heads)]

    def body(j, outs):
        b = g * STEP_ROWS + j
        sel = lane == b
        one = pl.ds(b, 1)
        new = []
        for h in range(n_heads):
            w_c = wc_scr[h, one, :]
            w_k = wk_scr[h, one, :]
            c_new = w_c * c_ref[j, h] + (w_k * _pick_lane(k_scr[h], sel)) * z_ref[2 * n_heads + h, one, :]
            co_ref[j, h] = c_new
            n_new = w_c * n_ref[j, h:h + 1, :] + w_k * (z_ref[n_heads + h, one, :] * K_SCALE)
            no_ref[j, h:h + 1, :] = n_new
            num = jnp.dot(q[h], c_new.astype(BF16), preferred_element_type=F32)
            new.append(_put_row(outs[2 * h], j, num))
            new.append(_put_row(outs[2 * h + 1], j, n_new))
        return tuple(new)

    zero = jnp.zeros((STEP_ROWS, LANES), F32)
    outs = lax.fori_loop(0, STEP_ROWS, body, (zero,) * (2 * n_heads), unroll=STEP_UNROLL)
    for h in range(n_heads):
        num, n_new = outs[2 * h], outs[2 * h + 1]
        den = jnp.sum(z_ref[h, grp, :] * n_new, axis=-1, keepdims=True)
        hid = num / jnp.maximum(jnp.abs(den), em_scr[h, grp, :])
        gate = jax.nn.sigmoid(z_ref[3 * n_heads + h, grp, :])
        o_ref[h] = _head_norm_gate(hid, gn_ref[:, h * LANES:(h + 1) * LANES], gate).astype(o_ref.dtype)


def kernel(x_prompt, x_sample, state_hgrn, state_mlstm_c, state_mlstm_n, state_mlstm_m, state_ret, norm_ffn1, ffn1_w_gate, ffn1_w_up, ffn1_w_down, norm_mix, w_in, hgrn_lb_logits, mlstm_gate_bias, gn_hgrn, gn_mlstm, gn_ret, w_out, norm_ffn2, ffn2_w_gate, ffn2_w_up, ffn2_w_down, norm_final):
    bp, t_len, d = x_prompt.shape
    bs, dec_len, _ = x_sample.shape
    depth = w_in.shape[0]
    h_hgrn, h_mlstm, h_ret = state_hgrn.shape[2], state_mlstm_c.shape[2], state_ret.shape[2]
    w_hgrn, w_mlstm, w_ret = h_hgrn * HEAD_DIM, h_mlstm * HEAD_DIM, h_ret * HEAD_DIM
    d_ff = ffn1_w_gate.shape[2]
    n_prompt = bp * t_len
    m_rows = n_prompt + bs
    seq_tile = min(SEQ_TILE, t_len)
    n_seq = t_len // seq_tile
    assert dec_len == 1 and bs == LANES and n_prompt % bs == 0 and t_len % seq_tile == 0 and seq_tile % BLK == 0
    row_block = n_prompt // bs
    step_block = n_prompt // STEP_ROWS
    tm_wide = _row_tile(m_rows, WIDE_ROW_TILE)
    tm_deep = _row_tile(m_rows, DEEP_ROW_TILE)
    assert n_prompt % STACK_TILE == 0
    o_mlstm, o_ret, o_hgrn = 0, h_mlstm, h_mlstm + h_ret
    assert o_ret % h_ret == 0 and o_hgrn % h_hgrn == 0
    n_slabs = h_hgrn + h_mlstm + h_ret
    o_order = tuple(range(o_hgrn, n_slabs)) + tuple(range(o_mlstm, o_ret)) + tuple(range(o_ret, o_hgrn))

    gate0 = 4 * w_hgrn + 4 * w_mlstm
    ret0 = gate0 + 2 * h_mlstm
    w_in_hg = jnp.concatenate(
        [w_in[:, :, :4 * w_hgrn], jnp.pad(w_in[:, :, gate0:ret0], ((0, 0), (0, 0), (0, 2 * LANES - 2 * h_mlstm)))],
        axis=2).astype(BF16)
    w_in_m = w_in[:, :, 4 * w_hgrn:gate0].astype(BF16)
    w_in_r = w_in[:, :, ret0:].astype(BF16)
    gate_slab = 4 * h_hgrn
    bias_pad = jnp.pad(mlstm_gate_bias, ((0, 0), (0, LANES - 2 * h_mlstm))).reshape(depth, 1, LANES)
    m_pad = jnp.pad(state_mlstm_m, ((0, 0), (0, 0), (0, LANES - h_mlstm)))
    gn_h = gn_hgrn.reshape(depth, 1, w_hgrn)
    gn_m = gn_mlstm.reshape(depth, 1, w_mlstm)
    gn_r = gn_ret.reshape(depth, 1, w_ret)
    log_gamma = jnp.log1p(-jnp.exp2(-5.0 - jnp.arange(h_ret, dtype=F32)))
    cos_p, sin_p = _rope_tables(t_len, min(ROPE_TILE, t_len), 0.0, 1.0)
    cos_s, sin_s = _rope_tables(8, 8, PAST_LEN, 0.0)

    smem = pl.BlockSpec(memory_space=pltpu.SMEM)
    o3_shape = (n_slabs, m_rows, LANES)
    head_sq = (HEAD_DIM, HEAD_DIM)

    def z_prompt(n_blocks):
        return pl.BlockSpec((n_blocks, seq_tile, LANES), lambda b, t: (0, b * n_seq + t, 0))

    def o_prompt(heads, slab0):
        return pl.BlockSpec((heads, seq_tile, LANES), lambda b, t: (slab0 // heads, b * n_seq + t, 0))

    def o_step(heads, slab0):
        return pl.BlockSpec((heads, STEP_ROWS, LANES), lambda g: (slab0 // heads, step_block + g, 0))

    x, n = _stack_rows(x_prompt.reshape(n_prompt, d), x_sample.reshape(bs, d), norm_ffn1[0])
    new_h_p, new_c_p, new_n_p, new_m_p, new_r_p = [], [], [], [], []
    new_n_s, new_m_s = [], []
    st_h = st_c = st_r = None

    def ffn(x, n, ssq, w_gate, w_up, w_down, layer, next_norm):
        mid = _matmul(n, [w_gate, w_up], layer, mode="swiglu", tm=tm_wide, tn=512, n_out=d_ff, row_ssq=ssq)
        down = dict(tm=tm_deep, tn=1024, n_out=d, resid=x, weight_buffers=1)
        if next_norm is None:
            return _matmul(mid, [w_down], layer, mode="resid", **down), None, None
        return _matmul(mid, [w_down], layer, mode="resid_gain", norm=next_norm, **down)

    ssq = None
    for layer in range(depth):
        x, n, ssq = ffn(x, n, ssq, ffn1_w_gate, ffn1_w_up, ffn1_w_down, layer, (norm_mix, layer))

        cols = dict(mode="cols", tm=tm_wide, row_ssq=ssq)
        z_h = _matmul(n, [w_in_hg], layer, mode="cols", tm=tm_deep, tn=w_in_hg.shape[2], n_out=w_in_hg.shape[2],
                      row_ssq=ssq)
        z_m = _matmul(n, [w_in_m], layer, tn=1024, n_out=4 * w_mlstm, **cols)
        z_r = _matmul(n, [w_in_r], layer, tn=1024, n_out=4 * w_ret, **cols)

        def gain2(width):
            return pl.BlockSpec((None, 1, width), lambda b, t: (layer, 0, 0))

        def gain1(width):
            return pl.BlockSpec((None, 1, width), lambda g: (layer, 0, 0))

        o3, s_hp = _mixer_call(
            functools.partial(_hgrn_prompt_kernel, layer=layer, n_heads=h_hgrn), "hgrn_prompt", (bp, n_seq),
            [z_prompt(4 * h_hgrn), pl.BlockSpec(hgrn_lb_logits.shape, lambda b, t: (0, 0)), gain2(w_hgrn)],
            [z_h, hgrn_lb_logits, gn_h],
            [(o3_shape, BF16, o_prompt(h_hgrn, o_hgrn), None),
             ((bp, h_hgrn) + head_sq, F32, pl.BlockSpec((None, h_hgrn) + head_sq, lambda b, t: (b, 0, 0, 0)), None)],
            [pltpu.VMEM((h_hgrn,) + head_sq, F32)] + [pltpu.VMEM((h_hgrn, BLK, LANES), F32) for _ in range(4)],
            ("arbitrary", "arbitrary"))

        vec = ((bp, h_mlstm, 1, LANES), F32, pl.BlockSpec((None, h_mlstm, 1, LANES), lambda b, t: (b, 0, 0, 0)), None)
        o3, c_p, n_p, m_p = _mixer_call(
            functools.partial(_mlstm_prompt_kernel, n_heads=h_mlstm), "mlstm_prompt", (bp, n_seq),
            [z_prompt(4 * h_mlstm),
             pl.BlockSpec((None, seq_tile, LANES), lambda b, t: (gate_slab, b * n_seq + t, 0)),
             pl.BlockSpec((None, 1, LANES), lambda b, t: (layer, 0, 0)), gain2(w_mlstm)],
            [z_m, z_h, bias_pad, gn_m],
            [(o3_shape, BF16, o_prompt(h_mlstm, o_mlstm), o3),
             ((bp, h_mlstm) + head_sq, F32, pl.BlockSpec((None, h_mlstm) + head_sq, lambda b, t: (b, 0, 0, 0)), None),
             vec, vec],
            [pltpu.VMEM((h_mlstm, HEAD_DIM, 2 * HEAD_DIM), F32), pltpu.VMEM((h_mlstm, 1, LANES), F32)],
            ("arbitrary", "arbitrary"))

        table = pl.BlockSpec((t_len, LANES), lambda b, t: (0, 0))
        o3, s_rp = _mixer_call(
            functools.partial(_ret_prompt_kernel, n_heads=h_ret), "ret_prompt", (bp, n_seq),
            [z_prompt(4 * h_ret), table, table, smem, gain2(w_ret)],
            [z_r, cos_p, sin_p, log_gamma, gn_r],
            [(o3_shape, BF16, o_prompt(h_ret, o_ret), o3),
             ((bp, h_ret) + head_sq, F32, pl.BlockSpec((None, h_ret) + head_sq, lambda b, t: (b, 0, 0, 0)), None)],
            [pltpu.VMEM((h_ret,) + head_sq, F32), pltpu.VMEM((h_ret, 3, BLK, LANES), F32)],
            ("arbitrary", "arbitrary"))

        def z_step(n_blocks):
            return pl.BlockSpec((n_blocks, bs, LANES), lambda g: (0, row_block, 0))

        def state_in(state):
            tail = state.shape[2:]
            zeros = (0,) * len(tail)
            return pl.BlockSpec((None, STEP_ROWS) + tail, lambda g: (layer, g) + zeros)

        tposed = pltpu.VMEM((h_hgrn, HEAD_DIM, bs), F32)
        o3, st_h = _mixer_call(
            functools.partial(_hgrn_step_kernel, layer=layer, n_heads=h_hgrn), "hgrn_step", (bs // STEP_ROWS,),
            [z_step(4 * h_hgrn), state_in(state_hgrn), pl.BlockSpec(hgrn_lb_logits.shape, lambda g: (0, 0)), gain1(w_hgrn)],
            [z_h, state_hgrn, hgrn_lb_logits, gn_h],
            [(o3_shape, BF16, o_step(h_hgrn, o_hgrn), o3), _state_out(state_hgrn, layer, st_h)],
            [tposed, tposed], ("arbitrary",))

        tposed = pltpu.VMEM((h_mlstm, HEAD_DIM, bs), F32)
        rowrep = pltpu.VMEM((h_mlstm, bs, LANES), F32)
        tile = pl.BlockSpec((bs, LANES), lambda g: (0, 0))
        o3, st_c, n_s, m_s = _mixer_call(
            functools.partial(_mlstm_step_kernel, n_heads=h_mlstm), "mlstm_step", (bs // STEP_ROWS,),
            [z_step(4 * h_mlstm), pl.BlockSpec((None, bs, LANES), lambda g: (gate_slab, row_block, 0)),
             pl.BlockSpec((None, 1, LANES), lambda g: (layer, 0, 0)),
             state_in(state_mlstm_c), state_in(state_mlstm_n), tile, gain1(w_mlstm)],
            [z_m, z_h, bias_pad, state_mlstm_c, state_mlstm_n, m_pad[layer], gn_m],
            [(o3_shape, BF16, o_step(h_mlstm, o_mlstm), o3), _state_out(state_mlstm_c, layer, st_c),
             (state_mlstm_n.shape[1:], F32, pl.BlockSpec((STEP_ROWS, h_mlstm, HEAD_DIM), lambda g: (g, 0, 0)), None),
             ((bs, LANES), F32, tile, None)],
            [tposed, rowrep, rowrep, rowrep], ("arbitrary",))

        tposed = pltpu.VMEM((h_ret, HEAD_DIM, bs), F32)
        table = pl.BlockSpec(cos_s.shape, lambda g: (0, 0))
        o3, st_r = _mixer_call(
            functools.partial(_ret_step_kernel, n_heads=h_ret), "ret_step", (bs // STEP_ROWS,),
            [z_step(4 * h_ret), state_in(state_ret), table, table, smem, gain1(w_ret)],
            [z_r, state_ret, cos_s, sin_s, log_gamma, gn_r],
            [(o3_shape, BF16, o_step(h_ret, o_ret), o3), _state_out(state_ret, layer, st_r)],
            [tposed], ("arbitrary",))

        new_h_p.append(s_hp)
        new_c_p.append(c_p)
        new_n_p.append(n_p[:, :, 0, :])
        new_m_p.append(m_p[:, :, 0, 0])
        new_r_p.append(s_rp)
        new_n_s.append(n_s)
        new_m_s.append(m_s[:, :h_mlstm])

        x, n = _matmul(o3, [w_out], layer, mode="resid_norm", tm=tm_deep, tn=d, n_out=d, resid=x,
                       norm=(norm_ffn2, layer), weight_buffers=1, lhs_order=o_order)

        next_norm = (norm_ffn1, layer + 1) if layer + 1 < depth else None
        x, n, ssq = ffn(x, n, None, ffn2_w_gate, ffn2_w_up, ffn2_w_down, layer, next_norm)

    y_p, y_s = _rmsnorm_split(x, norm_final, n_prompt)
    return (
        y_p.reshape(bp, t_len, d), y_s.reshape(bs, dec_len, d),
        jnp.stack(new_h_p), jnp.stack(new_c_p), jnp.stack(new_n_p), jnp.stack(new_m_p), jnp.stack(new_r_p),
        st_h, st_c, jnp.stack(new_n_s), jnp.stack(new_m_s), st_r,
    )
```

```python
import functools

import jax
import jax.numpy as jnp
from jax import lax
from jax.experimental import pallas as pl
from jax.experimental.pallas import tpu as pltpu

F32 = jnp.float32
BF16 = jnp.bfloat16

HEAD_DIM = 128
LANES = 128
EPS = 1e-6
ROPE_BASE = 10000.0
K_SCALE = HEAD_DIM ** -0.5
PAST_LEN = 16384

BLK = 128
SUB = 32
SEQ_TILE = 512
STEP_ROWS = 16
STEP_UNROLL = 4
STACK_TILE = 1024
ROW_PARTS = (5, 2, 1)
WIDE_ROW_TILE = 2080
DEEP_ROW_TILE = 640
ROPE_TILE = 256
VMEM_LIMIT_BYTES = 60 * 1024 * 1024

DECAY_SPAN_MAX = 60.0

_TRANS_B = (((1,), (1,)), ((), ()))
_TRANS_A = (((0,), (0,)), ((), ()))


def _row_tile(rows, limit):
    return max(t for t in range(16, min(rows, limit) + 1, 16) if rows % t == 0)


def _params(*semantics):
    return pltpu.CompilerParams(dimension_semantics=semantics, vmem_limit_bytes=VMEM_LIMIT_BYTES)


def _silu(x):
    return x * jax.nn.sigmoid(x)


def _softplus_neg_abs(x):
    return jnp.log(1.0 + jnp.exp(-jnp.abs(x)))


def _log_sigmoid(x):
    return jnp.minimum(x, 0.0) - _softplus_neg_abs(x)


def _iota(shape, dim):
    return lax.broadcasted_iota(jnp.int32, shape, dim)


def _seg_cumsum(x, seg):
    pos = _iota(x.shape, 0) & (seg - 1)
    shift = 1
    while shift < seg:
        x = x + jnp.where(pos >= shift, pltpu.roll(x, shift, 0), 0.0)
        shift *= 2
    return x


def _head_norm_gate(o, gain, gate):
    return o * lax.rsqrt(jnp.mean(o * o, axis=-1, keepdims=True) + EPS) * gain * gate


def _pick_lane(x, lane_mask):
    return jnp.sum(jnp.where(lane_mask, x, 0.0), axis=-1, keepdims=True)


def _put_row(tile, j, new):
    return jnp.where(_iota(tile.shape, 0) == j, new, tile)


def _stack_rows_kernel(a_ref, b_ref, g_ref, o_ref, n_ref, *, n_a):
    i = pl.program_id(0)

    def emit(x, rows):
        o_ref[rows, :] = x
        y = x * lax.rsqrt(jnp.mean(x * x, axis=-1, keepdims=True) + EPS)
        n_ref[rows, :] = (y * g_ref[...]).astype(n_ref.dtype)

    @pl.when(i < n_a)
    def _():
        emit(a_ref[...], slice(None))

    @pl.when(i == n_a)
    def _():
        emit(b_ref[...], slice(0, b_ref.shape[0]))


def _stack_rows(a, b, gain):
    (ma, d), mb = a.shape, b.shape[0]
    n_a = ma // STACK_TILE
    tile = pl.BlockSpec((STACK_TILE, d), lambda i: (i, 0))
    return pl.pallas_call(
        functools.partial(_stack_rows_kernel, n_a=n_a),
        out_shape=(jax.ShapeDtypeStruct((ma + mb, d), a.dtype), jax.ShapeDtypeStruct((ma + mb, d), BF16)),
        grid=(n_a + 1,),
        in_specs=[
            pl.BlockSpec((STACK_TILE, d), lambda i: (jnp.minimum(i, n_a - 1), 0)),
            pl.BlockSpec((mb, d), lambda i: (0, 0)),
            pl.BlockSpec((1, d), lambda i: (0, 0)),
        ],
        out_specs=(tile, tile),
        compiler_params=_params("arbitrary"),
        name="stack_rows",
    )(a, b, gain.reshape(1, d))


def _rmsnorm_split_kernel(x_ref, g_ref, a_ref, b_ref, *, n_a):
    i = pl.program_id(0)
    x = x_ref[...]
    y = x * lax.rsqrt(jnp.mean(x * x, axis=-1, keepdims=True) + EPS) * g_ref[...]

    @pl.when(i < n_a)
    def _():
        a_ref[...] = y

    @pl.when(i == n_a)
    def _():
        b_ref[...] = y[0:b_ref.shape[0], :]


def _rmsnorm_split(x, gain, rows_a):
    m, d = x.shape
    n_a = rows_a // STACK_TILE
    rows_b = m - rows_a
    return pl.pallas_call(
        functools.partial(_rmsnorm_split_kernel, n_a=n_a),
        out_shape=(jax.ShapeDtypeStruct((rows_a, d), F32), jax.ShapeDtypeStruct((rows_b, d), F32)),
        grid=(n_a + 1,),
        in_specs=[
            pl.BlockSpec((STACK_TILE, d), lambda i: (i, 0)),
            pl.BlockSpec((1, d), lambda i: (0, 0)),
        ],
        out_specs=(
            pl.BlockSpec((STACK_TILE, d), lambda i: (jnp.minimum(i, n_a - 1), 0)),
            pl.BlockSpec((rows_b, d), lambda i: (0, 0)),
        ),
        compiler_params=_params("arbitrary"),
        name="rmsnorm_split",
    )(x, gain.reshape(1, d))


def _matmul_kernel(*refs, n_w, mode, lhs_order, scaled, parts):
    lhs_ref = refs[0]
    w_refs = refs[1:1 + n_w]
    rest = refs[1 + n_w:]
    if scaled:
        ssq_ref, rest = rest[0], rest[1:]
    tm = lhs_ref.shape[0] if lhs_order is None else lhs_ref.shape[1]
    row_parts = [slice(p * (tm // parts), (p + 1) * (tm // parts)) for p in range(parts)]

    def lhs_rows(rs):
        if lhs_order is None:
            return lhs_ref[rs, :]
        return jnp.concatenate([lhs_ref[c, rs, :] for c in lhs_order], axis=1)

    ws = [w_ref[...].astype(BF16) for w_ref in w_refs]
    accs = [[jnp.dot(lhs_rows(rs), w, preferred_element_type=F32) for w in ws] for rs in row_parts]
    for rs, acc in zip(row_parts, accs):
        if scaled:
            total = ssq_ref[0, rs, :]
            for part in range(1, ssq_ref.shape[0]):
                total = total + ssq_ref[part, rs, :]
            r = lax.rsqrt(total[:, 0:1] * (1.0 / ws[0].shape[0]) + EPS)
            acc = [a * r for a in acc]
        if mode == "swiglu":
            (o_ref,) = rest
            gate, up = acc
            o_ref[rs, :] = (0.5 * _silu(gate) * up).astype(o_ref.dtype)
        elif mode == "resid":
            x_ref, o_ref = rest
            o_ref[rs, :] = x_ref[rs, :] + acc[0]
        elif mode == "resid_gain":
            x_ref, g_ref, o_ref, xg_ref, sq_ref = rest
            x = x_ref[rs, :] + acc[0]
            o_ref[rs, :] = x
            xg_ref[rs, :] = (x * g_ref[...]).astype(xg_ref.dtype)
            sq_ref[rs, :] = jnp.broadcast_to(jnp.sum(x * x, axis=-1, keepdims=True), (x.shape[0], LANES))
        elif mode == "resid_norm":
            x_ref, g_ref, o_ref, n_ref = rest
            x = x_ref[rs, :] + acc[0]
            o_ref[rs, :] = x
            y = x * lax.rsqrt(jnp.mean(x * x, axis=-1, keepdims=True) + EPS)
            n_ref[rs, :] = (y * g_ref[...]).astype(n_ref.dtype)
        else:
            (o_ref,) = rest
            for c in range(o_ref.shape[0]):
                o_ref[c, rs, :] = acc[0][:, c * LANES:(c + 1) * LANES]


def _matmul(lhs, weights, layer, *, mode, tm, tn, n_out, col_off=0, resid=None, norm=None, row_ssq=None,
            weight_buffers=2, lhs_order=None):
    if lhs_order is None:
        m, k = lhs.shape
        lhs_spec = pl.BlockSpec((tm, k), lambda j, i: (i, 0))
    else:
        m, k = lhs.shape[1], lhs.shape[0] * LANES
        lhs_spec = pl.BlockSpec((lhs.shape[0], tm, LANES), lambda j, i: (0, i, 0))
    n_w = len(weights)
    n_tiles = pl.cdiv(n_out, tn)
    grid = (n_tiles, m // tm)
    in_specs = [lhs_spec]
    in_specs += [pl.BlockSpec((None, k, tn), lambda j, i: (layer, 0, j + col_off),
                              pipeline_mode=pl.Buffered(weight_buffers)) for _ in weights]
    args = [lhs, *weights]
    if row_ssq is not None:
        in_specs.append(pl.BlockSpec((row_ssq.shape[0], tm, LANES), lambda j, i: (0, i, 0)))
        args.append(row_ssq)
    tile = pl.BlockSpec((tm, tn), lambda j, i: (i, j))
    if mode in ("resid", "resid_gain", "resid_norm"):
        in_specs.append(tile)
        args.append(resid)
    if mode in ("resid_gain", "resid_norm"):
        gains, norm_layer = norm
        in_specs.append(pl.BlockSpec((None, 1, tn), lambda j, i: (norm_layer, 0, j)))
        args.append(gains.reshape(-1, 1, n_out))
    if mode == "resid_gain":
        out_shape = (jax.ShapeDtypeStruct((m, n_out), F32), jax.ShapeDtypeStruct((m, n_out), BF16),
                     jax.ShapeDtypeStruct((n_tiles, m, LANES), F32))
        out_spec = (tile, tile, pl.BlockSpec((None, tm, LANES), lambda j, i: (j, i, 0)))
    elif mode == "resid_norm":
        assert tn == n_out
        out_shape = (jax.ShapeDtypeStruct((m, n_out), F32), jax.ShapeDtypeStruct((m, n_out), BF16))
        out_spec = (tile, tile)
    elif mode == "cols":
        out_shape = jax.ShapeDtypeStruct((n_out // LANES, m, LANES), F32)
        out_spec = pl.BlockSpec((tn // LANES, tm, LANES), lambda j, i: (j, i, 0))
    else:
        out_shape = jax.ShapeDtypeStruct((m, n_out), BF16 if mode == "swiglu" else F32)
        out_spec = tile
    return pl.pallas_call(
        functools.partial(_matmul_kernel, n_w=n_w, mode=mode, lhs_order=lhs_order, scaled=row_ssq is not None,
                          parts=next(p for p in ROW_PARTS if p == 1 or tm % (16 * p) == 0)),
        out_shape=out_shape,
        grid=grid,
        in_specs=in_specs,
        out_specs=out_spec,
        compiler_params=_params("arbitrary", "arbitrary"),
        name="matmul_" + mode,
    )(*args)


def _rope_kernel(inv_ref, cos_ref, sin_ref, *, pos0, step):
    shape = cos_ref.shape
    row = _iota(shape, 0) + pl.program_id(0) * shape[0]
    ang = (pos0 + step * row.astype(F32)) * inv_ref[...]
    sin = jnp.sin(ang)
    cos_ref[...] = jnp.cos(ang)
    sin_ref[...] = jnp.where(_iota(shape, 1) < HEAD_DIM // 2, -sin, sin)


def _rope_tables(rows, tile, pos0, step):
    inv = ROPE_BASE ** (-jnp.arange(0, HEAD_DIM, 2, dtype=F32) / HEAD_DIM)
    inv = jnp.concatenate([inv, inv]).reshape(1, HEAD_DIM)
    out = jax.ShapeDtypeStruct((rows, HEAD_DIM), F32)
    spec = pl.BlockSpec((tile, HEAD_DIM), lambda i: (i, 0))
    return pl.pallas_call(
        functools.partial(_rope_kernel, pos0=float(pos0), step=float(step)),
        out_shape=(out, out),
        grid=(rows // tile,),
        in_specs=[pl.BlockSpec((1, HEAD_DIM), lambda i: (0, 0))],
        out_specs=(spec, spec),
        compiler_params=_params("arbitrary"),
        name="rope_tables",
    )(inv)


def _rope(x, cos, sin):
    return x * cos + pltpu.roll(x, HEAD_DIM // 2, 1) * sin


def _mixer_call(kernel_fn, name, grid, in_specs, args, outs, scratch, semantics):
    in_specs, args, aliases = list(in_specs), list(args), {}
    for k, (_, _, _, prev) in enumerate(outs):
        if prev is not None:
            aliases[len(args)] = k
            in_specs.append(pl.BlockSpec(memory_space=pl.ANY))
            args.append(prev)
    n_alias = len(aliases)

    def body(*refs):
        n_in = len(args) - n_alias
        kernel_fn(*refs[:n_in], *refs[n_in + n_alias:])

    return pl.pallas_call(
        body,
        out_shape=tuple(jax.ShapeDtypeStruct(s, dt) for s, dt, _, _ in outs),
        grid=grid,
        in_specs=in_specs,
        out_specs=tuple(spec for _, _, spec, _ in outs),
        scratch_shapes=scratch,
        input_output_aliases=aliases,
        compiler_params=_params(*semantics),
        name=name,
    )(*args)


def _state_out(state, layer, prev):
    tail = state.shape[2:]
    zeros = (0,) * len(tail)
    spec = pl.BlockSpec((None, STEP_ROWS) + tail, lambda g: (layer, g) + zeros)
    return (state.shape, F32, spec, prev)


def _hgrn_lower_bound(logits, layer):
    mx = jnp.max(logits, axis=0, keepdims=True)
    e = jnp.exp(logits - mx)
    p = e / jnp.sum(e, axis=0, keepdims=True)
    if layer == 0:
        return jnp.zeros_like(mx)
    return jnp.sum(p[1:layer + 1], axis=0, keepdims=True)


def _hgrn_gates(fz, lb):
    log_lb = jnp.log(lb)
    y = jnp.log1p(-lb) + _log_sigmoid(fz)
    log_f = jnp.maximum(log_lb, y) + _softplus_neg_abs(log_lb - y)
    key = (1.0 - lb) / (1.0 + jnp.exp(fz))
    return log_f, key


def _hgrn_prompt_parts(z_ref, lbl_ref, gn_ref, o_ref, s_ref, st_ref, q_scr, k_scr, b_scr, a_scr, *, layer, n_heads,
                       slab0):
    t = pl.program_id(1)
    n_sub = BLK // SUB
    lb = _hgrn_lower_bound(lbl_ref[...], layer)
    row = _iota((BLK, LANES), 0)
    col = _iota((BLK, LANES), 1)
    pos = row & (SUB - 1)
    seg0 = row - pos

    def init():
        @pl.when(t == 0)
        def _():
            st_ref[...] = jnp.zeros_like(st_ref)

    def seg_row(x, s):
        x3 = x.reshape(n_sub, SUB, LANES)
        return jnp.broadcast_to(x3[:, s:s + 1, :], x3.shape).reshape(BLK, LANES)

    def block(i):
        rows = pl.ds(pl.multiple_of(i * BLK, BLK), BLK)
        span = jnp.zeros((BLK, LANES), F32)
        for h in range(n_heads):
            log_f, k = _hgrn_gates(z_ref[n_heads + h, rows, :], lb[:, h * LANES:(h + 1) * LANES])
            b = _seg_cumsum(log_f, SUB)
            q_scr[h] = _silu(z_ref[h, rows, :])
            k_scr[h] = k
            b_scr[h] = b
            span = jnp.maximum(span, -seg_row(b, SUB - 1))
        mild = jnp.max(span) <= DECAY_SPAN_MAX

        @pl.when(mild)
        def _():
            q_end, k_end = [], []
            for h in range(n_heads):
                b = b_scr[h]
                b_end = seg_row(b, SUB - 1)
                q_end.append((q_scr[h] * jnp.exp(b - b_end)).astype(BF16))
                k_end.append((k_scr[h] * jnp.exp(b_end - b)).astype(BF16))
            a = [lax.dot_general(q_end[h], k_end[h], _TRANS_B, preferred_element_type=F32) for h in range(n_heads)]
            for h in range(n_heads):
                a_scr[h] = jnp.where((col >= seg0) & (col <= row), a[h], 0.0)

        @pl.when(jnp.logical_not(mild))
        def _():
            for h in range(n_heads):
                q, k, b = q_scr[h], k_scr[h], b_scr[h]
                a = jnp.zeros((BLK, LANES), F32)
                for s in range(SUB):
                    e = jnp.exp(jnp.where(pos >= s, b - seg_row(b, s), -jnp.inf))
                    c = jnp.sum(q * seg_row(k, s) * e, axis=-1, keepdims=True)
                    a = jnp.where(col == seg0 + s, c, a)
                a_scr[h] = a

        heads = range(n_heads)
        subs = [slice(n * SUB, (n + 1) * SUB) for n in range(n_sub)]
        vb, q_in, k_out, lam = [], [], [], []
        for h in heads:
            b = b_scr[h]
            b_end = seg_row(b, SUB - 1)
            vb.append(z_ref[2 * n_heads + h, rows, :].astype(BF16))
            q_in.append((q_scr[h] * jnp.exp(b)).astype(BF16))
            k_out.append((k_scr[h] * jnp.exp(b_end - b)).astype(BF16))
            lam.append(jnp.exp(b_end))
        ups = [[lax.dot_general(vb[h][sl], k_out[h][sl], _TRANS_A, preferred_element_type=F32) for sl in subs]
               for h in heads]
        o = [jnp.dot(a_scr[h].astype(BF16), vb[h], preferred_element_type=F32) for h in heads]
        states = []
        for h in heads:
            chain = [st_ref[h]]
            for n in range(n_sub):
                chain.append(chain[n] * lam[h][n * SUB:n * SUB + 1, :] + ups[h][n])
            st_ref[h] = chain[n_sub]
            states.append([s.astype(BF16) for s in chain[:n_sub]])
        parts = [[lax.dot_general(q_in[h][sl], states[h][n], _TRANS_B, preferred_element_type=F32)
                  for n, sl in enumerate(subs)] for h in heads]
        for h in heads:
            gate = _silu(z_ref[3 * n_heads + h, rows, :])
            out = _head_norm_gate(o[h] + jnp.concatenate(parts[h], axis=0), gn_ref[:, h * LANES:(h + 1) * LANES], gate)
            o_ref[slab0 + h, rows, :] = out.astype(o_ref.dtype)

    def final():
        @pl.when(t == pl.num_programs(1) - 1)
        def _():
            for h in range(n_heads):
                s_ref[h] = st_ref[h].T

    return init, block, final


def _hgrn_step_kernel(z_ref, s_ref, lbl_ref, gn_ref, o_ref, so_ref, f_scr, k_scr, *, layer, n_heads):
    g = pl.program_id(0)
    n_rows = z_ref.shape[1]

    @pl.when(g == 0)
    def _():
        lb = _hgrn_lower_bound(lbl_ref[...], layer)
        for h in range(n_heads):
            log_f, k = _hgrn_gates(z_ref[n_heads + h], lb[:, h * LANES:(h + 1) * LANES])
            f_scr[h] = jnp.exp(log_f).T
            k_scr[h] = k.T

    lane = _iota((HEAD_DIM, n_rows), 1)
    grp = pl.ds(pl.multiple_of(g * STEP_ROWS, STEP_ROWS), STEP_ROWS)
    q = [_silu(z_ref[h, grp, :]).astype(BF16) for h in range(n_heads)]

    def body(j, outs):
        b = g * STEP_ROWS + j
        sel = lane == b
        one = pl.ds(b, 1)
        new = []
        for h in range(n_heads):
            s_new = _pick_lane(f_scr[h], sel) * s_ref[j, h] + _pick_lane(k_scr[h], sel) * z_ref[2 * n_heads + h, one, :]
            so_ref[j, h] = s_new
            new.append(_put_row(outs[h], j, jnp.dot(q[h], s_new.astype(BF16), preferred_element_type=F32)))
        return tuple(new)

    zero = jnp.zeros((STEP_ROWS, LANES), F32)
    outs = lax.fori_loop(0, STEP_ROWS, body, (zero,) * n_heads, unroll=STEP_UNROLL)
    for h in range(n_heads):
        gate = _silu(z_ref[3 * n_heads + h, grp, :])
        o_ref[h] = _head_norm_gate(outs[h], gn_ref[:, h * LANES:(h + 1) * LANES], gate).astype(o_ref.dtype)


def _ret_prompt_parts(z_ref, cos_ref, sin_ref, lg_ref, gn_ref, o_ref, s_ref, st_ref, dec_ref, *, n_heads, slab0):
    t = pl.program_id(1)

    def init():
        @pl.when((pl.program_id(0) == 0) & (t == 0))
        def _():
            row = _iota((BLK, LANES), 0).astype(F32)
            diff = row - _iota((BLK, LANES), 1).astype(F32)
            for h in range(n_heads):
                lg = lg_ref[h]
                dec_ref[h, 0] = jnp.where(diff >= 0.0, jnp.exp(diff * lg), 0.0)
                dec_ref[h, 1] = jnp.exp((row + 1.0) * lg)
                dec_ref[h, 2] = jnp.exp((BLK - 1.0 - row) * lg)

        @pl.when(t == 0)
        def _():
            st_ref[...] = jnp.zeros_like(st_ref)

    def block(i):
        rows = pl.ds(pl.multiple_of(i * BLK, BLK), BLK)
        tab = pl.ds(pl.multiple_of(t * z_ref.shape[1] + i * BLK, BLK), BLK)
        cos = cos_ref[tab, :]
        sin = sin_ref[tab, :]
        heads = range(n_heads)
        q = [_rope(z_ref[h, rows, :], cos, sin) for h in heads]
        k = [_rope(z_ref[n_heads + h, rows, :], cos, sin) * K_SCALE for h in heads]
        vb = [z_ref[2 * n_heads + h, rows, :].astype(BF16) for h in heads]
        a = [lax.dot_general(q[h].astype(BF16), k[h].astype(BF16), _TRANS_B, preferred_element_type=F32)
             for h in heads]
        carried = [jnp.dot((q[h] * dec_ref[h, 1]).astype(BF16), st_ref[h].astype(BF16), preferred_element_type=F32)
                   for h in heads]
        update = [jnp.dot((k[h] * dec_ref[h, 2]).T.astype(BF16), vb[h], preferred_element_type=F32) for h in heads]
        o = [jnp.dot((a[h] * dec_ref[h, 0]).astype(BF16), vb[h], preferred_element_type=F32) for h in heads]
        for h in heads:
            s_dec = jnp.exp(jnp.full((1, LANES), BLK, F32) * lg_ref[h])
            st_ref[h] = s_dec * st_ref[h] + update[h]
            gate = _silu(z_ref[3 * n_heads + h, rows, :])
            out = _head_norm_gate(o[h] + carried[h], gn_ref[:, h * LANES:(h + 1) * LANES], gate)
            o_ref[slab0 + h, rows, :] = out.astype(o_ref.dtype)

    def final():
        @pl.when(t == pl.num_programs(1) - 1)
        def _():
            s_ref[...] = st_ref[...]

    return init, block, final


def _ret_step_kernel(z_ref, s_ref, cos_ref, sin_ref, lg_ref, gn_ref, o_ref, so_ref, k_scr, *, n_heads):
    g = pl.program_id(0)
    n_rows = z_ref.shape[1]
    cos = cos_ref[0:1, :]
    sin = sin_ref[0:1, :]

    @pl.when(g == 0)
    def _():
        for h in range(n_heads):
            k_scr[h] = (_rope(z_ref[n_heads + h], cos, sin) * K_SCALE).T

    lane = _iota((HEAD_DIM, n_rows), 1)
    grp = pl.ds(pl.multiple_of(g * STEP_ROWS, STEP_ROWS), STEP_ROWS)
    q = [_rope(z_ref[h, grp, :], cos, sin).astype(BF16) for h in range(n_heads)]

    def body(j, outs):
        b = g * STEP_ROWS + j
        sel = lane == b
        one = pl.ds(b, 1)
        new = []
        for h in range(n_heads):
            gamma = jnp.exp(jnp.full((1, LANES), lg_ref[h], F32))
            s_new = gamma * s_ref[j, h] + _pick_lane(k_scr[h], sel) * z_ref[2 * n_heads + h, one, :]
            so_ref[j, h] = s_new
            new.append(_put_row(outs[h], j, jnp.dot(q[h], s_new.astype(BF16), preferred_element_type=F32)))
        return tuple(new)

    zero = jnp.zeros((STEP_ROWS, LANES), F32)
    outs = lax.fori_loop(0, STEP_ROWS, body, (zero,) * n_heads, unroll=STEP_UNROLL)
    for h in range(n_heads):
        gate = _silu(z_ref[3 * n_heads + h, grp, :])
        o_ref[h] = _head_norm_gate(outs[h], gn_ref[:, h * LANES:(h + 1) * LANES], gate).astype(o_ref.dtype)


def _mlstm_prompt_parts(z_ref, zg_ref, bias_ref, gn_ref, o_ref, c_ref, n_ref, m_ref, cn_scr, m_scr, *, n_heads, slab0):
    t = pl.program_id(1)
    lane = _iota((BLK, LANES), 1)
    causal = lane <= _iota((BLK, LANES), 0)
    ones = jnp.ones((BLK, LANES), BF16)

    def twice(x):
        return jnp.concatenate([x, x], axis=1)

    def init():
        @pl.when(t == 0)
        def _():
            cn_scr[...] = jnp.zeros_like(cn_scr)
            m_scr[...] = jnp.zeros_like(m_scr)

    def block(i):
        rows = pl.ds(pl.multiple_of(i * BLK, BLK), BLK)
        gates = zg_ref[rows, :] + bias_ref[...]
        log_f_all = pltpu.roll(_log_sigmoid(gates), LANES - n_heads, 1)
        bt_all = _seg_cumsum(log_f_all, BLK)
        w_t = (gates - bt_all).T
        heads = range(n_heads)
        d_mat, w_inter, m_t, w_c, kw_t, qb, kb, v_ones = [], [], [], [], [], [], [], []
        for h in heads:
            bt = _pick_lane(bt_all, lane == h)
            w_row = w_t[h:h + 1, :]
            log_d = jnp.where(causal, bt + w_row, -jnp.inf)
            m_prev = m_scr[h]
            log_inter = bt + m_prev
            m_t.append(jnp.maximum(log_inter, jnp.max(log_d, axis=-1, keepdims=True)))
            d_mat.append(jnp.exp(log_d - m_t[h]))
            w_inter.append(jnp.exp(log_inter - m_t[h]))
            m_end = m_t[h][BLK - 1:BLK, :]
            b_end = bt[BLK - 1:BLK, :]
            k = z_ref[n_heads + h, rows, :] * K_SCALE
            kb.append(k.astype(BF16))
            kw_t.append((k.T * jnp.exp(b_end + w_row - m_end)).astype(BF16))
            w_c.append(jnp.exp(b_end + m_prev - m_end))
            qb.append(z_ref[h, rows, :].astype(BF16))
            v_ones.append(jnp.concatenate([z_ref[2 * n_heads + h, rows, :].astype(BF16), ones], axis=1))
        sc = [lax.dot_general(qb[h], kb[h], _TRANS_B, preferred_element_type=F32) for h in heads]
        carried = [jnp.dot(qb[h], cn_scr[h].astype(BF16), preferred_element_type=F32) for h in heads]
        update = [jnp.dot(kw_t[h], v_ones[h], preferred_element_type=F32) for h in heads]
        both = [jnp.dot((sc[h] * d_mat[h]).astype(BF16), v_ones[h], preferred_element_type=F32) for h in heads]
        for h in heads:
            tot = both[h] + carried[h] * twice(w_inter[h])
            num, den = tot[:, :LANES], tot[:, LANES:]
            hid = num / jnp.maximum(jnp.abs(den), jnp.exp(-m_t[h]))
            cn_scr[h] = twice(w_c[h]) * cn_scr[h] + update[h]
            m_scr[h] = m_t[h][BLK - 1:BLK, :]
            gate = jax.nn.sigmoid(z_ref[3 * n_heads + h, rows, :])
            out = _head_norm_gate(hid, gn_ref[:, h * LANES:(h + 1) * LANES], gate)
            o_ref[slab0 + h, rows, :] = out.astype(o_ref.dtype)

    def final():
        @pl.when(t == pl.num_programs(1) - 1)
        def _():
            for h in range(n_heads):
                c_ref[h] = cn_scr[h, :, :LANES]
                n_ref[h] = cn_scr[h, :, LANES:].T[0:1, :]
            m_ref[...] = m_scr[...]

    return init, block, final


def _mlstm_step_kernel(z_ref, zg_ref, bias_ref, c_ref, n_ref, m_ref, gn_ref,
                       o_ref, co_ref, no_ref, mo_ref, k_scr, wc_scr, wk_scr, em_scr, *, n_heads):
    g = pl.program_id(0)
    n_rows = z_ref.shape[1]

    @pl.when(g == 0)
    def _():
        lanes = _iota((n_rows, LANES), 1)
        gates = zg_ref[...] + bias_ref[...]
        log_f = pltpu.roll(_log_sigmoid(gates), LANES - n_heads, 1)
        m_old = m_ref[...]
        m_new = jnp.maximum(log_f + m_old, gates)
        mo_ref[...] = m_new
        w_c = jnp.exp(log_f + m_old - m_new)
        w_k = jnp.exp(gates - m_new)
        e_m = jnp.exp(-m_new)
        for h in range(n_heads):
            sel = lanes == h
            wc_scr[h] = jnp.broadcast_to(_pick_lane(w_c, sel), (n_rows, LANES))
            wk_scr[h] = jnp.broadcast_to(_pick_lane(w_k, sel), (n_rows, LANES))
            em_scr[h] = jnp.broadcast_to(_pick_lane(e_m, sel), (n_rows, LANES))
            k_scr[h] = (z_ref[n_heads + h] * K_SCALE).T

    lane = _iota((HEAD_DIM, n_rows), 1)
    grp = pl.ds(pl.multiple_of(g * STEP_ROWS, STEP_ROWS), STEP_ROWS)
    q = [z_ref[h, grp, :].astype(BF16) for h in range(n_heads)]

    def body(j, outs):
        b = g * STEP_ROWS + j
        sel = lane == b
        one = pl.ds(b, 1)
        new = []
        for h in range(n_heads):
            w_c = wc_scr[h, one, :]
            w_k = wk_scr[h, one, :]
            c_new = w_c * c_ref[j, h] + (w_k * _pick_lane(k_scr[h], sel)) * z_ref[2 * n_heads + h, one, :]
            co_ref[j, h] = c_new
            n_new = w_c * n_ref[j, h:h + 1, :] + w_k * (z_ref[n_heads + h, one, :] * K_SCALE)
            no_ref[j, h:h + 1, :] = n_new
            num = jnp.dot(q[h], c_new.astype(BF16), preferred_element_type=F32)
            new.append(_put_row(outs[2 * h], j, num))
            new.append(_put_row(outs[2 * h + 1], j, n_new))
        return tuple(new)

    zero = jnp.zeros((STEP_ROWS, LANES), F32)
    outs = lax.fori_loop(0, STEP_ROWS, body, (zero,) * (2 * n_heads), unroll=STEP_UNROLL)
    for h in range(n_heads):
        num, n_new = outs[2 * h], outs[2 * h + 1]
        den = jnp.sum(z_ref[h, grp, :] * n_new, axis=-1, keepdims=True)
        hid = num / jnp.maximum(jnp.abs(den), em_scr[h, grp, :])
        gate = jax.nn.sigmoid(z_ref[3 * n_heads + h, grp, :])
        o_ref[h] = _head_norm_gate(hid, gn_ref[:, h * LANES:(h + 1) * LANES], gate).astype(o_ref.dtype)


def _mixers_prompt_kernel(zh_ref, zm_ref, zr_ref, lbl_ref, bias_ref, cos_ref, sin_ref, lg_ref, gnh_ref, gnm_ref,
                          gnr_ref, o_ref, sh_ref, c_ref, n_ref, m_ref, sr_ref,
                          h_st, h_q, h_k, h_b, h_a, m_cn, m_m, r_st, r_dec, *, layer, heads, slabs, gate_slab):
    h_hgrn, h_mlstm, h_ret = heads
    o_hgrn, o_mlstm, o_ret = slabs
    mixers = [
        _hgrn_prompt_parts(zh_ref, lbl_ref, gnh_ref, o_ref, sh_ref, h_st, h_q, h_k, h_b, h_a,
                           layer=layer, n_heads=h_hgrn, slab0=o_hgrn),
        _mlstm_prompt_parts(zm_ref, zh_ref.at[gate_slab], bias_ref, gnm_ref, o_ref, c_ref, n_ref, m_ref, m_cn, m_m,
                            n_heads=h_mlstm, slab0=o_mlstm),
        _ret_prompt_parts(zr_ref, cos_ref, sin_ref, lg_ref, gnr_ref, o_ref, sr_ref, r_st, r_dec,
                          n_heads=h_ret, slab0=o_ret),
    ]
    for init, _, _ in mixers:
        init()

    def block(i, carry):
        for _, run, _ in mixers:
            run(i)
        return carry

    lax.fori_loop(0, zh_ref.shape[1] // BLK, block, 0)
    for _, _, final in mixers:
        final()


def kernel(x_prompt, x_sample, state_hgrn, state_mlstm_c, state_mlstm_n, state_mlstm_m, state_ret, norm_ffn1, ffn1_w_gate, ffn1_w_up, ffn1_w_down, norm_mix, w_in, hgrn_lb_logits, mlstm_gate_bias, gn_hgrn, gn_mlstm, gn_ret, w_out, norm_ffn2, ffn2_w_gate, ffn2_w_up, ffn2_w_down, norm_final):
    bp, t_len, d = x_prompt.shape
    bs, dec_len, _ = x_sample.shape
    depth = w_in.shape[0]
    h_hgrn, h_mlstm, h_ret = state_hgrn.shape[2], state_mlstm_c.shape[2], state_ret.shape[2]
    w_hgrn, w_mlstm, w_ret = h_hgrn * HEAD_DIM, h_mlstm * HEAD_DIM, h_ret * HEAD_DIM
    d_ff = ffn1_w_gate.shape[2]
    n_prompt = bp * t_len
    m_rows = n_prompt + bs
    seq_tile = min(SEQ_TILE, t_len)
    n_seq = t_len // seq_tile
    assert dec_len == 1 and bs == LANES and n_prompt % bs == 0 and t_len % seq_tile == 0 and seq_tile % BLK == 0
    row_block = n_prompt // bs
    step_block = n_prompt // STEP_ROWS
    tm_wide = _row_tile(m_rows, WIDE_ROW_TILE)
    tm_deep = _row_tile(m_rows, DEEP_ROW_TILE)
    assert n_prompt % STACK_TILE == 0
    o_mlstm, o_ret, o_hgrn = 0, h_mlstm, h_mlstm + h_ret
    assert o_ret % h_ret == 0 and o_hgrn % h_hgrn == 0
    n_slabs = h_hgrn + h_mlstm + h_ret
    o_order = tuple(range(o_hgrn, n_slabs)) + tuple(range(o_mlstm, o_ret)) + tuple(range(o_ret, o_hgrn))

    gate0 = 4 * w_hgrn + 4 * w_mlstm
    ret0 = gate0 + 2 * h_mlstm
    w_in_hg = jnp.concatenate(
        [w_in[:, :, :4 * w_hgrn], jnp.pad(w_in[:, :, gate0:ret0], ((0, 0), (0, 0), (0, 2 * LANES - 2 * h_mlstm)))],
        axis=2).astype(BF16)
    w_in_m = w_in[:, :, 4 * w_hgrn:gate0].astype(BF16)
    w_in_r = w_in[:, :, ret0:].astype(BF16)
    gate_slab = 4 * h_hgrn
    bias_pad = jnp.pad(mlstm_gate_bias, ((0, 0), (0, LANES - 2 * h_mlstm))).reshape(depth, 1, LANES)
    m_pad = jnp.pad(state_mlstm_m, ((0, 0), (0, 0), (0, LANES - h_mlstm)))
    gn_h = gn_hgrn.reshape(depth, 1, w_hgrn)
    gn_m = gn_mlstm.reshape(depth, 1, w_mlstm)
    gn_r = gn_ret.reshape(depth, 1, w_ret)
    log_gamma = jnp.log1p(-jnp.exp2(-5.0 - jnp.arange(h_ret, dtype=F32)))
    cos_p, sin_p = _rope_tables(t_len, min(ROPE_TILE, t_len), 0.0, 1.0)
    cos_s, sin_s = _rope_tables(8, 8, PAST_LEN, 0.0)

    smem = pl.BlockSpec(memory_space=pltpu.SMEM)
    o3_shape = (n_slabs, m_rows, LANES)
    head_sq = (HEAD_DIM, HEAD_DIM)

    def z_prompt(n_blocks):
        return pl.BlockSpec((n_blocks, seq_tile, LANES), lambda b, t: (0, b * n_seq + t, 0))

    def o_step(heads, slab0):
        return pl.BlockSpec((heads, STEP_ROWS, LANES), lambda g: (slab0 // heads, step_block + g, 0))

    x, n = _stack_rows(x_prompt.reshape(n_prompt, d), x_sample.reshape(bs, d), norm_ffn1[0])
    new_h_p, new_c_p, new_n_p, new_m_p, new_r_p = [], [], [], [], []
    new_n_s, new_m_s = [], []
    st_h = st_c = st_r = None

    def ffn(x, n, ssq, w_gate, w_up, w_down, layer, next_norm):
        mid = _matmul(n, [w_gate, w_up], layer, mode="swiglu", tm=tm_wide, tn=512, n_out=d_ff, row_ssq=ssq)
        down = dict(tm=tm_deep, tn=1024, n_out=d, resid=x, weight_buffers=1)
        if next_norm is None:
            return _matmul(mid, [w_down], layer, mode="resid", **down), None, None
        return _matmul(mid, [w_down], layer, mode="resid_gain", norm=next_norm, **down)

    ssq = None
    for layer in range(depth):
        x, n, ssq = ffn(x, n, ssq, ffn1_w_gate, ffn1_w_up, ffn1_w_down, layer, (norm_mix, layer))

        cols = dict(mode="cols", tm=tm_wide, row_ssq=ssq)
        z_h = _matmul(n, [w_in_hg], layer, mode="cols", tm=tm_deep, tn=w_in_hg.shape[2], n_out=w_in_hg.shape[2],
                      row_ssq=ssq)
        z_m = _matmul(n, [w_in_m], layer, tn=1024, n_out=4 * w_mlstm, **cols)
        z_r = _matmul(n, [w_in_r], layer, tn=1024, n_out=4 * w_ret, **cols)

        def gain2(width):
            return pl.BlockSpec((None, 1, width), lambda b, t: (layer, 0, 0))

        def gain1(width):
            return pl.BlockSpec((None, 1, width), lambda g: (layer, 0, 0))

        def state_p(heads):
            return ((bp, heads) + head_sq, F32, pl.BlockSpec((None, heads) + head_sq, lambda b, t: (b, 0, 0, 0)), None)

        vec = ((bp, h_mlstm, 1, LANES), F32, pl.BlockSpec((None, h_mlstm, 1, LANES), lambda b, t: (b, 0, 0, 0)), None)
        table = pl.BlockSpec((t_len, LANES), lambda b, t: (0, 0))
        o3, s_hp, c_p, n_p, m_p, s_rp = _mixer_call(
            functools.partial(_mixers_prompt_kernel, layer=layer, heads=(h_hgrn, h_mlstm, h_ret),
                              slabs=(o_hgrn, o_mlstm, o_ret), gate_slab=gate_slab),
            "mixers_prompt", (bp, n_seq),
            [z_prompt(z_h.shape[0]), z_prompt(z_m.shape[0]), z_prompt(z_r.shape[0]),
             pl.BlockSpec(hgrn_lb_logits.shape, lambda b, t: (0, 0)),
             pl.BlockSpec((None, 1, LANES), lambda b, t: (layer, 0, 0)),
             table, table, smem, gain2(w_hgrn), gain2(w_mlstm), gain2(w_ret)],
            [z_h, z_m, z_r, hgrn_lb_logits, bias_pad, cos_p, sin_p, log_gamma, gn_h, gn_m, gn_r],
            [(o3_shape, BF16, z_prompt(n_slabs), None), state_p(h_hgrn), state_p(h_mlstm), vec, vec, state_p(h_ret)],
            [pltpu.VMEM((h_hgrn,) + head_sq, F32)] + [pltpu.VMEM((h_hgrn, BLK, LANES), F32) for _ in range(4)]
            + [pltpu.VMEM((h_mlstm, HEAD_DIM, 2 * HEAD_DIM), F32), pltpu.VMEM((h_mlstm, 1, LANES), F32),
               pltpu.VMEM((h_ret,) + head_sq, F32), pltpu.VMEM((h_ret, 3, BLK, LANES), F32)],
            ("arbitrary", "arbitrary"))

        def z_step(n_blocks):
            return pl.BlockSpec((n_blocks, bs, LANES), lambda g: (0, row_block, 0))

        def state_in(state):
            tail = state.shape[2:]
            zeros = (0,) * len(tail)
            return pl.BlockSpec((None, STEP_ROWS) + tail, lambda g: (layer, g) + zeros)

        tposed = pltpu.VMEM((h_hgrn, HEAD_DIM, bs), F32)
        o3, st_h = _mixer_call(
            functools.partial(_hgrn_step_kernel, layer=layer, n_heads=h_hgrn), "hgrn_step", (bs // STEP_ROWS,),
            [z_step(4 * h_hgrn), state_in(state_hgrn), pl.BlockSpec(hgrn_lb_logits.shape, lambda g: (0, 0)), gain1(w_hgrn)],
            [z_h, state_hgrn, hgrn_lb_logits, gn_h],
            [(o3_shape, BF16, o_step(h_hgrn, o_hgrn), o3), _state_out(state_hgrn, layer, st_h)],
            [tposed, tposed], ("arbitrary",))

        tposed = pltpu.VMEM((h_mlstm, HEAD_DIM, bs), F32)
        rowrep = pltpu.VMEM((h_mlstm, bs, LANES), F32)
        tile = pl.BlockSpec((bs, LANES), lambda g: (0, 0))
        o3, st_c, n_s, m_s = _mixer_call(
            functools.partial(_mlstm_step_kernel, n_heads=h_mlstm), "mlstm_step", (bs // STEP_ROWS,),
            [z_step(4 * h_mlstm), pl.BlockSpec((None, bs, LANES), lambda g: (gate_slab, row_block, 0)),
             pl.BlockSpec((None, 1, LANES), lambda g: (layer, 0, 0)),
             state_in(state_mlstm_c), state_in(state_mlstm_n), tile, gain1(w_mlstm)],
            [z_m, z_h, bias_pad, state_mlstm_c, state_mlstm_n, m_pad[layer], gn_m],
            [(o3_shape, BF16, o_step(h_mlstm, o_mlstm), o3), _state_out(state_mlstm_c, layer, st_c),
             (state_mlstm_n.shape[1:], F32, pl.BlockSpec((STEP_ROWS, h_mlstm, HEAD_DIM), lambda g: (g, 0, 0)), None),
             ((bs, LANES), F32, tile, None)],
            [tposed, rowrep, rowrep, rowrep], ("arbitrary",))

        tposed = pltpu.VMEM((h_ret, HEAD_DIM, bs), F32)
        table = pl.BlockSpec(cos_s.shape, lambda g: (0, 0))
        o3, st_r = _mixer_call(
            functools.partial(_ret_step_kernel, n_heads=h_ret), "ret_step", (bs // STEP_ROWS,),
            [z_step(4 * h_ret), state_in(state_ret), table, table, smem, gain1(w_ret)],
            [z_r, state_ret, cos_s, sin_s, log_gamma, gn_r],
            [(o3_shape, BF16, o_step(h_ret, o_ret), o3), _state_out(state_ret, layer, st_r)],
            [tposed], ("arbitrary",))

        new_h_p.append(s_hp)
        new_c_p.append(c_p)
        new_n_p.append(n_p[:, :, 0, :])
        new_m_p.append(m_p[:, :, 0, 0])
        new_r_p.append(s_rp)
        new_n_s.append(n_s)
        new_m_s.append(m_s[:, :h_mlstm])

        x, n = _matmul(o3, [w_out], layer, mode="resid_norm", tm=tm_deep, tn=d, n_out=d, resid=x,
                       norm=(norm_ffn2, layer), weight_buffers=1, lhs_order=o_order)

        next_norm = (norm_ffn1, layer + 1) if layer + 1 < depth else None
        x, n, ssq = ffn(x, n, None, ffn2_w_gate, ffn2_w_up, ffn2_w_down, layer, next_norm)

    y_p, y_s = _rmsnorm_split(x, norm_final, n_prompt)
    return (
        y_p.reshape(bp, t_len, d), y_s.reshape(bs, dec_len, d),
        jnp.stack(new_h_p), jnp.stack(new_c_p), jnp.stack(new_n_p), jnp.stack(new_m_p), jnp.stack(new_r_p),
        st_h, st_c, jnp.stack(new_n_s), jnp.stack(new_m_s), st_r,
    )
```

```python
import functools

import jax
import jax.numpy as jnp
from jax import lax
from jax.experimental import pallas as pl
from jax.experimental.pallas import tpu as pltpu

F32 = jnp.float32
BF16 = jnp.bfloat16

HEAD_DIM = 128
LANES = 128
EPS = 1e-6
ROPE_BASE = 10000.0
K_SCALE = HEAD_DIM ** -0.5
PAST_LEN = 16384

BLK = 128
SUB = 32
SEQ_TILE = 512
STEP_ROWS = 16
STEP_UNROLL = 4
STACK_TILE = 1024
ROW_PARTS = (5, 2, 1)
WIDE_ROW_TILE = 2080
DEEP_ROW_TILE = 640
ROPE_TILE = 256
VMEM_LIMIT_BYTES = 60 * 1024 * 1024

DECAY_SPAN_MAX = 60.0

_TRANS_B = (((1,), (1,)), ((), ()))
_TRANS_A = (((0,), (0,)), ((), ()))


def _row_tile(rows, limit):
    return max(t for t in range(16, min(rows, limit) + 1, 16) if rows % t == 0)


def _params(*semantics):
    return pltpu.CompilerParams(dimension_semantics=semantics, vmem_limit_bytes=VMEM_LIMIT_BYTES)


def _silu(x):
    return x * jax.nn.sigmoid(x)


def _softplus_neg_abs(x):
    return jnp.log(1.0 + jnp.exp(-jnp.abs(x)))


def _log_sigmoid(x):
    return jnp.minimum(x, 0.0) - _softplus_neg_abs(x)


def _iota(shape, dim):
    return lax.broadcasted_iota(jnp.int32, shape, dim)


def _seg_cumsum(x, seg):
    pos = _iota(x.shape, 0) & (seg - 1)
    shift = 1
    while shift < seg:
        x = x + jnp.where(pos >= shift, pltpu.roll(x, shift, 0), 0.0)
        shift *= 2
    return x


def _head_norm_gate(o, gain, gate):
    return o * lax.rsqrt(jnp.mean(o * o, axis=-1, keepdims=True) + EPS) * gain * gate


def _pick_lane(x, lane_mask):
    return jnp.sum(jnp.where(lane_mask, x, 0.0), axis=-1, keepdims=True)


def _put_row(tile, j, new):
    return jnp.where(_iota(tile.shape, 0) == j, new, tile)


def _stack_rows_kernel(a_ref, b_ref, g_ref, o_ref, n_ref, *, n_a):
    i = pl.program_id(0)

    def emit(x, rows):
        o_ref[rows, :] = x
        y = x * lax.rsqrt(jnp.mean(x * x, axis=-1, keepdims=True) + EPS)
        n_ref[rows, :] = (y * g_ref[...]).astype(n_ref.dtype)

    @pl.when(i < n_a)
    def _():
        emit(a_ref[...], slice(None))

    @pl.when(i == n_a)
    def _():
        emit(b_ref[...], slice(0, b_ref.shape[0]))


def _stack_rows(a, b, gain):
    (ma, d), mb = a.shape, b.shape[0]
    n_a = ma // STACK_TILE
    tile = pl.BlockSpec((STACK_TILE, d), lambda i: (i, 0))
    return pl.pallas_call(
        functools.partial(_stack_rows_kernel, n_a=n_a),
        out_shape=(jax.ShapeDtypeStruct((ma + mb, d), a.dtype), jax.ShapeDtypeStruct((ma + mb, d), BF16)),
        grid=(n_a + 1,),
        in_specs=[
            pl.BlockSpec((STACK_TILE, d), lambda i: (jnp.minimum(i, n_a - 1), 0)),
            pl.BlockSpec((mb, d), lambda i: (0, 0)),
            pl.BlockSpec((1, d), lambda i: (0, 0)),
        ],
        out_specs=(tile, tile),
        compiler_params=_params("arbitrary"),
        name="stack_rows",
    )(a, b, gain.reshape(1, d))


def _rmsnorm_split_kernel(x_ref, g_ref, a_ref, b_ref, *, n_a):
    i = pl.program_id(0)
    x = x_ref[...]
    y = x * lax.rsqrt(jnp.mean(x * x, axis=-1, keepdims=True) + EPS) * g_ref[...]

    @pl.when(i < n_a)
    def _():
        a_ref[...] = y

    @pl.when(i == n_a)
    def _():
        b_ref[...] = y[0:b_ref.shape[0], :]


def _rmsnorm_split(x, gain, rows_a):
    m, d = x.shape
    n_a = rows_a // STACK_TILE
    rows_b = m - rows_a
    return pl.pallas_call(
        functools.partial(_rmsnorm_split_kernel, n_a=n_a),
        out_shape=(jax.ShapeDtypeStruct((rows_a, d), F32), jax.ShapeDtypeStruct((rows_b, d), F32)),
        grid=(n_a + 1,),
        in_specs=[
            pl.BlockSpec((STACK_TILE, d), lambda i: (i, 0)),
            pl.BlockSpec((1, d), lambda i: (0, 0)),
        ],
        out_specs=(
            pl.BlockSpec((STACK_TILE, d), lambda i: (jnp.minimum(i, n_a - 1), 0)),
            pl.BlockSpec((rows_b, d), lambda i: (0, 0)),
        ),
        compiler_params=_params("arbitrary"),
        name="rmsnorm_split",
    )(x, gain.reshape(1, d))


def _matmul_kernel(*refs, n_w, mode, lhs_order, scaled, parts):
    lhs_ref = refs[0]
    w_refs = refs[1:1 + n_w]
    rest = refs[1 + n_w:]
    if scaled:
        ssq_ref, rest = rest[0], rest[1:]
    tm = lhs_ref.shape[0] if lhs_order is None else lhs_ref.shape[1]
    row_parts = [slice(p * (tm // parts), (p + 1) * (tm // parts)) for p in range(parts)]

    def lhs_rows(rs):
        if lhs_order is None:
            return lhs_ref[rs, :]
        return jnp.concatenate([lhs_ref[c, rs, :] for c in lhs_order], axis=1)

    ws = [w_ref[...].astype(BF16) for w_ref in w_refs]
    accs = [[jnp.dot(lhs_rows(rs), w, preferred_element_type=F32) for w in ws] for rs in row_parts]
    for rs, acc in zip(row_parts, accs):
        if scaled:
            total = ssq_ref[0, rs, :]
            for part in range(1, ssq_ref.shape[0]):
                total = total + ssq_ref[part, rs, :]
            r = lax.rsqrt(total[:, 0:1] * (1.0 / ws[0].shape[0]) + EPS)
            acc = [a * r for a in acc]
        if mode == "swiglu":
            (o_ref,) = rest
            gate, up = acc
            o_ref[rs, :] = (0.5 * _silu(gate) * up).astype(o_ref.dtype)
        elif mode == "resid":
            x_ref, o_ref = rest
            o_ref[rs, :] = x_ref[rs, :] + acc[0]
        elif mode == "resid_gain":
            x_ref, g_ref, o_ref, xg_ref, sq_ref = rest
            x = x_ref[rs, :] + acc[0]
            o_ref[rs, :] = x
            xg_ref[rs, :] = (x * g_ref[...]).astype(xg_ref.dtype)
            sq_ref[rs, :] = jnp.broadcast_to(jnp.sum(x * x, axis=-1, keepdims=True), (x.shape[0], LANES))
        elif mode == "resid_norm":
            x_ref, g_ref, o_ref, n_ref = rest
            x = x_ref[rs, :] + acc[0]
            o_ref[rs, :] = x
            y = x * lax.rsqrt(jnp.mean(x * x, axis=-1, keepdims=True) + EPS)
            n_ref[rs, :] = (y * g_ref[...]).astype(n_ref.dtype)
        else:
            (o_ref,) = rest
            for c in range(o_ref.shape[0]):
                o_ref[c, rs, :] = acc[0][:, c * LANES:(c + 1) * LANES]


def _matmul(lhs, weights, layer, *, mode, tm, tn, n_out, col_off=0, resid=None, norm=None, row_ssq=None,
            weight_buffers=2, lhs_order=None):
    if lhs_order is None:
        m, k = lhs.shape
        lhs_spec = pl.BlockSpec((tm, k), lambda j, i: (i, 0))
    else:
        m, k = lhs.shape[1], lhs.shape[0] * LANES
        lhs_spec = pl.BlockSpec((lhs.shape[0], tm, LANES), lambda j, i: (0, i, 0))
    n_w = len(weights)
    n_tiles = pl.cdiv(n_out, tn)
    grid = (n_tiles, m // tm)
    in_specs = [lhs_spec]
    in_specs += [pl.BlockSpec((None, k, tn), lambda j, i: (layer, 0, j + col_off),
                              pipeline_mode=pl.Buffered(weight_buffers)) for _ in weights]
    args = [lhs, *weights]
    if row_ssq is not None:
        in_specs.append(pl.BlockSpec((row_ssq.shape[0], tm, LANES), lambda j, i: (0, i, 0)))
        args.append(row_ssq)
    tile = pl.BlockSpec((tm, tn), lambda j, i: (i, j))
    if mode in ("resid", "resid_gain", "resid_norm"):
        in_specs.append(tile)
        args.append(resid)
    if mode in ("resid_gain", "resid_norm"):
        gains, norm_layer = norm
        in_specs.append(pl.BlockSpec((None, 1, tn), lambda j, i: (norm_layer, 0, j)))
        args.append(gains.reshape(-1, 1, n_out))
    if mode == "resid_gain":
        out_shape = (jax.ShapeDtypeStruct((m, n_out), F32), jax.ShapeDtypeStruct((m, n_out), BF16),
                     jax.ShapeDtypeStruct((n_tiles, m, LANES), F32))
        out_spec = (tile, tile, pl.BlockSpec((None, tm, LANES), lambda j, i: (j, i, 0)))
    elif mode == "resid_norm":
        assert tn == n_out
        out_shape = (jax.ShapeDtypeStruct((m, n_out), F32), jax.ShapeDtypeStruct((m, n_out), BF16))
        out_spec = (tile, tile)
    elif mode == "cols":
        out_shape = jax.ShapeDtypeStruct((n_out // LANES, m, LANES), F32)
        out_spec = pl.BlockSpec((tn // LANES, tm, LANES), lambda j, i: (j, i, 0))
    else:
        out_shape = jax.ShapeDtypeStruct((m, n_out), BF16 if mode == "swiglu" else F32)
        out_spec = tile
    return pl.pallas_call(
        functools.partial(_matmul_kernel, n_w=n_w, mode=mode, lhs_order=lhs_order, scaled=row_ssq is not None,
                          parts=next(p for p in ROW_PARTS if p == 1 or tm % (16 * p) == 0)),
        out_shape=out_shape,
        grid=grid,
        in_specs=in_specs,
        out_specs=out_spec,
        compiler_params=_params("arbitrary", "arbitrary"),
        name="matmul_" + mode,
    )(*args)


def _rope_kernel(inv_ref, cos_ref, sin_ref, *, pos0, step):
    shape = cos_ref.shape
    row = _iota(shape, 0) + pl.program_id(0) * shape[0]
    ang = (pos0 + step * row.astype(F32)) * inv_ref[...]
    sin = jnp.sin(ang)
    cos_ref[...] = jnp.cos(ang)
    sin_ref[...] = jnp.where(_iota(shape, 1) < HEAD_DIM // 2, -sin, sin)


def _rope_tables(rows, tile, pos0, step):
    inv = ROPE_BASE ** (-jnp.arange(0, HEAD_DIM, 2, dtype=F32) / HEAD_DIM)
    inv = jnp.concatenate([inv, inv]).reshape(1, HEAD_DIM)
    out = jax.ShapeDtypeStruct((rows, HEAD_DIM), F32)
    spec = pl.BlockSpec((tile, HEAD_DIM), lambda i: (i, 0))
    return pl.pallas_call(
        functools.partial(_rope_kernel, pos0=float(pos0), step=float(step)),
        out_shape=(out, out),
        grid=(rows // tile,),
        in_specs=[pl.BlockSpec((1, HEAD_DIM), lambda i: (0, 0))],
        out_specs=(spec, spec),
        compiler_params=_params("arbitrary"),
        name="rope_tables",
    )(inv)


def _rope(x, cos, sin):
    return x * cos + pltpu.roll(x, HEAD_DIM // 2, 1) * sin


def _mixer_call(kernel_fn, name, grid, in_specs, args, outs, scratch, semantics):
    in_specs, args, aliases = list(in_specs), list(args), {}
    for k, (_, _, _, prev) in enumerate(outs):
        if prev is not None:
            aliases[len(args)] = k
            in_specs.append(pl.BlockSpec(memory_space=pl.ANY))
            args.append(prev)
    n_alias = len(aliases)

    def body(*refs):
        n_in = len(args) - n_alias
        kernel_fn(*refs[:n_in], *refs[n_in + n_alias:])

    return pl.pallas_call(
        body,
        out_shape=tuple(jax.ShapeDtypeStruct(s, dt) for s, dt, _, _ in outs),
        grid=grid,
        in_specs=in_specs,
        out_specs=tuple(spec for _, _, spec, _ in outs),
        scratch_shapes=scratch,
        input_output_aliases=aliases,
        compiler_params=_params(*semantics),
        name=name,
    )(*args)


def _state_out(state, layer, prev):
    tail = state.shape[2:]
    zeros = (0,) * len(tail)
    spec = pl.BlockSpec((None, STEP_ROWS) + tail, lambda g: (layer, g) + zeros)
    return (state.shape, F32, spec, prev)


def _hgrn_lower_bound(logits, layer):
    mx = jnp.max(logits, axis=0, keepdims=True)
    e = jnp.exp(logits - mx)
    p = e / jnp.sum(e, axis=0, keepdims=True)
    if layer == 0:
        return jnp.zeros_like(mx)
    return jnp.sum(p[1:layer + 1], axis=0, keepdims=True)


def _hgrn_gates(fz, lb):
    log_lb = jnp.log(lb)
    y = jnp.log1p(-lb) + _log_sigmoid(fz)
    log_f = jnp.maximum(log_lb, y) + _softplus_neg_abs(log_lb - y)
    key = (1.0 - lb) / (1.0 + jnp.exp(fz))
    return log_f, key


def _hgrn_prompt_parts(z_ref, lbl_ref, gn_ref, o_ref, s_ref, st_ref, q_scr, k_scr, b_scr, a_scr, *, layer, n_heads,
                       slab0):
    t = pl.program_id(1)
    n_sub = BLK // SUB
    lb = _hgrn_lower_bound(lbl_ref[...], layer)
    row = _iota((BLK, LANES), 0)
    col = _iota((BLK, LANES), 1)
    pos = row & (SUB - 1)
    seg0 = row - pos

    def init():
        @pl.when(t == 0)
        def _():
            st_ref[...] = jnp.zeros_like(st_ref)

    def seg_row(x, s):
        x3 = x.reshape(n_sub, SUB, LANES)
        return jnp.broadcast_to(x3[:, s:s + 1, :], x3.shape).reshape(BLK, LANES)

    def block(i):
        rows = pl.ds(pl.multiple_of(i * BLK, BLK), BLK)
        span = jnp.zeros((BLK, LANES), F32)
        for h in range(n_heads):
            log_f, k = _hgrn_gates(z_ref[n_heads + h, rows, :], lb[:, h * LANES:(h + 1) * LANES])
            b = _seg_cumsum(log_f, SUB)
            q_scr[h] = _silu(z_ref[h, rows, :])
            k_scr[h] = k
            b_scr[h] = b
            span = jnp.maximum(span, -seg_row(b, SUB - 1))
        mild = jnp.max(span) <= DECAY_SPAN_MAX

        @pl.when(mild)
        def _():
            q_end, k_end = [], []
            for h in range(n_heads):
                b = b_scr[h]
                b_end = seg_row(b, SUB - 1)
                q_end.append((q_scr[h] * jnp.exp(b - b_end)).astype(BF16))
                k_end.append((k_scr[h] * jnp.exp(b_end - b)).astype(BF16))
            a = [lax.dot_general(q_end[h], k_end[h], _TRANS_B, preferred_element_type=F32) for h in range(n_heads)]
            for h in range(n_heads):
                a_scr[h] = jnp.where((col >= seg0) & (col <= row), a[h], 0.0)

        @pl.when(jnp.logical_not(mild))
        def _():
            for h in range(n_heads):
                q, k, b = q_scr[h], k_scr[h], b_scr[h]
                a = jnp.zeros((BLK, LANES), F32)
                for s in range(SUB):
                    e = jnp.exp(jnp.where(pos >= s, b - seg_row(b, s), -jnp.inf))
                    c = jnp.sum(q * seg_row(k, s) * e, axis=-1, keepdims=True)
                    a = jnp.where(col == seg0 + s, c, a)
                a_scr[h] = a

        heads = range(n_heads)
        subs = [slice(n * SUB, (n + 1) * SUB) for n in range(n_sub)]
        vb, q_in, k_out, lam = [], [], [], []
        for h in heads:
            b = b_scr[h]
            b_end = seg_row(b, SUB - 1)
            vb.append(z_ref[2 * n_heads + h, rows, :].astype(BF16))
            q_in.append((q_scr[h] * jnp.exp(b)).astype(BF16))
            k_out.append((k_scr[h] * jnp.exp(b_end - b)).astype(BF16))
            lam.append(jnp.exp(b_end))
        ups = [[lax.dot_general(vb[h][sl], k_out[h][sl], _TRANS_A, preferred_element_type=F32) for sl in subs]
               for h in heads]
        o = [jnp.dot(a_scr[h].astype(BF16), vb[h], preferred_element_type=F32) for h in heads]
        states = []
        for h in heads:
            chain = [st_ref[h]]
            for n in range(n_sub):
                chain.append(chain[n] * lam[h][n * SUB:n * SUB + 1, :] + ups[h][n])
            st_ref[h] = chain[n_sub]
            states.append([s.astype(BF16) for s in chain[:n_sub]])
        parts = [[lax.dot_general(q_in[h][sl], states[h][n], _TRANS_B, preferred_element_type=F32)
                  for n, sl in enumerate(subs)] for h in heads]
        for h in heads:
            gate = _silu(z_ref[3 * n_heads + h, rows, :])
            out = _head_norm_gate(o[h] + jnp.concatenate(parts[h], axis=0), gn_ref[:, h * LANES:(h + 1) * LANES], gate)
            o_ref[slab0 + h, rows, :] = out.astype(o_ref.dtype)

    def final():
        @pl.when(t == pl.num_programs(1) - 1)
        def _():
            for h in range(n_heads):
                s_ref[h] = st_ref[h].T

    return init, block, final


def _hgrn_step_kernel(z_ref, s_ref, lbl_ref, gn_ref, o_ref, so_ref, f_scr, k_scr, *, layer, n_heads):
    g = pl.program_id(0)
    n_rows = z_ref.shape[1]

    @pl.when(g == 0)
    def _():
        lb = _hgrn_lower_bound(lbl_ref[...], layer)
        for h in range(n_heads):
            log_f, k = _hgrn_gates(z_ref[n_heads + h], lb[:, h * LANES:(h + 1) * LANES])
            f_scr[h] = jnp.exp(log_f).T
            k_scr[h] = k.T

    lane = _iota((HEAD_DIM, n_rows), 1)
    grp = pl.ds(pl.multiple_of(g * STEP_ROWS, STEP_ROWS), STEP_ROWS)
    q = [_silu(z_ref[h, grp, :]).astype(BF16) for h in range(n_heads)]

    def body(j, outs):
        b = g * STEP_ROWS + j
        sel = lane == b
        one = pl.ds(b, 1)
        new = []
        for h in range(n_heads):
            s_new = _pick_lane(f_scr[h], sel) * s_ref[j, h] + _pick_lane(k_scr[h], sel) * z_ref[2 * n_heads + h, one, :]
            so_ref[j, h] = s_new
            new.append(_put_row(outs[h], j, jnp.dot(q[h], s_new.astype(BF16), preferred_element_type=F32)))
        return tuple(new)

    zero = jnp.zeros((STEP_ROWS, LANES), F32)
    outs = lax.fori_loop(0, STEP_ROWS, body, (zero,) * n_heads, unroll=STEP_UNROLL)
    for h in range(n_heads):
        gate = _silu(z_ref[3 * n_heads + h, grp, :])
        o_ref[h] = _head_norm_gate(outs[h], gn_ref[:, h * LANES:(h + 1) * LANES], gate).astype(o_ref.dtype)


def _ret_prompt_parts(z_ref, cos_ref, sin_ref, lg_ref, gn_ref, o_ref, s_ref, st_ref, dec_ref, *, n_heads, slab0):
    t = pl.program_id(1)

    def init():
        @pl.when((pl.program_id(0) == 0) & (t == 0))
        def _():
            row = _iota((BLK, LANES), 0).astype(F32)
            diff = row - _iota((BLK, LANES), 1).astype(F32)
            for h in range(n_heads):
                lg = lg_ref[h]
                dec_ref[h, 0] = jnp.where(diff >= 0.0, jnp.exp(diff * lg), 0.0)
                dec_ref[h, 1] = jnp.exp((row + 1.0) * lg)
                dec_ref[h, 2] = jnp.exp((BLK - 1.0 - row) * lg)

        @pl.when(t == 0)
        def _():
            st_ref[...] = jnp.zeros_like(st_ref)

    def block(i):
        rows = pl.ds(pl.multiple_of(i * BLK, BLK), BLK)
        tab = pl.ds(pl.multiple_of(t * z_ref.shape[1] + i * BLK, BLK), BLK)
        cos = cos_ref[tab, :]
        sin = sin_ref[tab, :]
        heads = range(n_heads)
        q = [_rope(z_ref[h, rows, :], cos, sin) for h in heads]
        k = [_rope(z_ref[n_heads + h, rows, :], cos, sin) * K_SCALE for h in heads]
        vb = [z_ref[2 * n_heads + h, rows, :].astype(BF16) for h in heads]
        a = [lax.dot_general(q[h].astype(BF16), k[h].astype(BF16), _TRANS_B, preferred_element_type=F32)
             for h in heads]
        carried = [jnp.dot((q[h] * dec_ref[h, 1]).astype(BF16), st_ref[h].astype(BF16), preferred_element_type=F32)
                   for h in heads]
        update = [jnp.dot((k[h] * dec_ref[h, 2]).T.astype(BF16), vb[h], preferred_element_type=F32) for h in heads]
        o = [jnp.dot((a[h] * dec_ref[h, 0]).astype(BF16), vb[h], preferred_element_type=F32) for h in heads]
        for h in heads:
            s_dec = jnp.exp(jnp.full((1, LANES), BLK, F32) * lg_ref[h])
            st_ref[h] = s_dec * st_ref[h] + update[h]
            gate = _silu(z_ref[3 * n_heads + h, rows, :])
            out = _head_norm_gate(o[h] + carried[h], gn_ref[:, h * LANES:(h + 1) * LANES], gate)
            o_ref[slab0 + h, rows, :] = out.astype(o_ref.dtype)

    def final():
        @pl.when(t == pl.num_programs(1) - 1)
        def _():
            s_ref[...] = st_ref[...]

    return init, block, final


def _ret_step_kernel(z_ref, s_ref, cos_ref, sin_ref, lg_ref, gn_ref, o_ref, so_ref, k_scr, *, n_heads):
    g = pl.program_id(0)
    n_rows = z_ref.shape[1]
    cos = cos_ref[0:1, :]
    sin = sin_ref[0:1, :]

    @pl.when(g == 0)
    def _():
        for h in range(n_heads):
            k_scr[h] = (_rope(z_ref[n_heads + h], cos, sin) * K_SCALE).T

    lane = _iota((HEAD_DIM, n_rows), 1)
    grp = pl.ds(pl.multiple_of(g * STEP_ROWS, STEP_ROWS), STEP_ROWS)
    q = [_rope(z_ref[h, grp, :], cos, sin).astype(BF16) for h in range(n_heads)]

    def body(j, outs):
        b = g * STEP_ROWS + j
        sel = lane == b
        one = pl.ds(b, 1)
        new = []
        for h in range(n_heads):
            gamma = jnp.exp(jnp.full((1, LANES), lg_ref[h], F32))
            s_new = gamma * s_ref[j, h] + _pick_lane(k_scr[h], sel) * z_ref[2 * n_heads + h, one, :]
            so_ref[j, h] = s_new
            new.append(_put_row(outs[h], j, jnp.dot(q[h], s_new.astype(BF16), preferred_element_type=F32)))
        return tuple(new)

    zero = jnp.zeros((STEP_ROWS, LANES), F32)
    outs = lax.fori_loop(0, STEP_ROWS, body, (zero,) * n_heads, unroll=STEP_UNROLL)
    for h in range(n_heads):
        gate = _silu(z_ref[3 * n_heads + h, grp, :])
        o_ref[h] = _head_norm_gate(outs[h], gn_ref[:, h * LANES:(h + 1) * LANES], gate).astype(o_ref.dtype)


def _mlstm_prompt_parts(z_ref, zg_ref, bias_ref, gn_ref, o_ref, c_ref, n_ref, m_ref, cn_scr, m_scr, *, n_heads, slab0):
    t = pl.program_id(1)
    lane = _iota((BLK, LANES), 1)
    causal = lane <= _iota((BLK, LANES), 0)
    ones = jnp.ones((BLK, LANES), BF16)

    def twice(x):
        return jnp.concatenate([x, x], axis=1)

    def init():
        @pl.when(t == 0)
        def _():
            cn_scr[...] = jnp.zeros_like(cn_scr)
            m_scr[...] = jnp.zeros_like(m_scr)

    def block(i):
        rows = pl.ds(pl.multiple_of(i * BLK, BLK), BLK)
        gates = zg_ref[rows, :] + bias_ref[...]
        log_f_all = pltpu.roll(_log_sigmoid(gates), LANES - n_heads, 1)
        bt_all = _seg_cumsum(log_f_all, BLK)
        w_t = (gates - bt_all).T
        heads = range(n_heads)
        d_mat, w_inter, m_t, w_c, kw_t, qb, kb, v_ones = [], [], [], [], [], [], [], []
        for h in heads:
            bt = _pick_lane(bt_all, lane == h)
            w_row = w_t[h:h + 1, :]
            log_d = jnp.where(causal, bt + w_row, -jnp.inf)
            m_prev = m_scr[h]
            log_inter = bt + m_prev
            m_t.append(jnp.maximum(log_inter, jnp.max(log_d, axis=-1, keepdims=True)))
            d_mat.append(jnp.exp(log_d - m_t[h]))
            w_inter.append(jnp.exp(log_inter - m_t[h]))
            m_end = m_t[h][BLK - 1:BLK, :]
            b_end = bt[BLK - 1:BLK, :]
            k = z_ref[n_heads + h, rows, :] * K_SCALE
            kb.append(k.astype(BF16))
            kw_t.append((k.T * jnp.exp(b_end + w_row - m_end)).astype(BF16))
            w_c.append(jnp.exp(b_end + m_prev - m_end))
            qb.append(z_ref[h, rows, :].astype(BF16))
            v_ones.append(jnp.concatenate([z_ref[2 * n_heads + h, rows, :].astype(BF16), ones], axis=1))
        sc = [lax.dot_general(qb[h], kb[h], _TRANS_B, preferred_element_type=F32) for h in heads]
        carried = [jnp.dot(qb[h], cn_scr[h].astype(BF16), preferred_element_type=F32) for h in heads]
        update = [jnp.dot(kw_t[h], v_ones[h], preferred_element_type=F32) for h in heads]
        both = [jnp.dot((sc[h] * d_mat[h]).astype(BF16), v_ones[h], preferred_element_type=F32) for h in heads]
        for h in heads:
            tot = both[h] + carried[h] * twice(w_inter[h])
            num, den = tot[:, :LANES], tot[:, LANES:]
            hid = num / jnp.maximum(jnp.abs(den), jnp.exp(-m_t[h]))
            cn_scr[h] = twice(w_c[h]) * cn_scr[h] + update[h]
            m_scr[h] = m_t[h][BLK - 1:BLK, :]
            gate = jax.nn.sigmoid(z_ref[3 * n_heads + h, rows, :])
            out = _head_norm_gate(hid, gn_ref[:, h * LANES:(h + 1) * LANES], gate)
            o_ref[slab0 + h, rows, :] = out.astype(o_ref.dtype)

    def final():
        @pl.when(t == pl.num_programs(1) - 1)
        def _():
            for h in range(n_heads):
                c_ref[h] = cn_scr[h, :, :LANES]
                n_ref[h] = cn_scr[h, :, LANES:].T[0:1, :]
            m_ref[...] = m_scr[...]

    return init, block, final


def _mlstm_step_kernel(z_ref, zg_ref, bias_ref, c_ref, n_ref, m_ref, gn_ref,
                       o_ref, co_ref, no_ref, mo_ref, k_scr, wc_scr, wk_scr, em_scr, *, n_heads):
    g = pl.program_id(0)
    n_rows = z_ref.shape[1]

    @pl.when(g == 0)
    def _():
        lanes = _iota((n_rows, LANES), 1)
        gates = zg_ref[...] + bias_ref[...]
        log_f = pltpu.roll(_log_sigmoid(gates), LANES - n_heads, 1)
        m_old = m_ref[...]
        m_new = jnp.maximum(log_f + m_old, gates)
        mo_ref[...] = m_new
        w_c = jnp.exp(log_f + m_old - m_new)
        w_k = jnp.exp(gates - m_new)
        e_m = jnp.exp(-m_new)
        for h in range(n_heads):
            sel = lanes == h
            wc_scr[h] = jnp.broadcast_to(_pick_lane(w_c, sel), (n_rows, LANES))
            wk_scr[h] = jnp.broadcast_to(_pick_lane(w_k, sel), (n_rows, LANES))
            em_scr[h] = jnp.broadcast_to(_pick_lane(e_m, sel), (n_rows, LANES))
            k_scr[h] = (z_ref[n_heads + h] * K_SCALE).T

    lane = _iota((HEAD_DIM, n_rows), 1)
    grp = pl.ds(pl.multiple_of(g * STEP_ROWS, STEP_ROWS), STEP_ROWS)
    q = [z_ref[h, grp, :].astype(BF16) for h in range(n_heads)]

    def body(j, outs):
        b = g * STEP_ROWS + j
        sel = lane == b
        one = pl.ds(b, 1)
        new = []
        for h in range(n_heads):
            w_c = wc_scr[h, one, :]
            w_k = wk_scr[h, one, :]
            c_new = w_c * c_ref[j, h] + (w_k * _pick_lane(k_scr[h], sel)) * z_ref[2 * n_heads + h, one, :]
            co_ref[j, h] = c_new
            n_new = w_c * n_ref[j, h:h + 1, :] + w_k * (z_ref[n_heads + h, one, :] * K_SCALE)
            no_ref[j, h:h + 1, :] = n_new
            num = jnp.dot(q[h], c_new.astype(BF16), preferred_element_type=F32)
            new.append(_put_row(outs[2 * h], j, num))
            new.append(_put_row(outs[2 * h + 1], j, n_new))
        return tuple(new)

    zero = jnp.zeros((STEP_ROWS, LANES), F32)
    outs = lax.fori_loop(0, STEP_ROWS, body, (zero,) * (2 * n_heads), unroll=STEP_UNROLL)
    for h in range(n_heads):
        num, n_new = outs[2 * h], outs[2 * h + 1]
        den = jnp.sum(z_ref[h, grp, :] * n_new, axis=-1, keepdims=True)
        hid = num / jnp.maximum(jnp.abs(den), em_scr[h, grp, :])
        gate = jax.nn.sigmoid(z_ref[3 * n_heads + h, grp, :])
        o_ref[h] = _head_norm_gate(hid, gn_ref[:, h * LANES:(h + 1) * LANES], gate).astype(o_ref.dtype)


def _mixers_prompt_kernel(zh_ref, zm_ref, zr_ref, zg_ref, lbl_ref, bias_ref, cos_ref, sin_ref, lg_ref, gnh_ref, gnm_ref,
                          gnr_ref, o_ref, sh_ref, c_ref, n_ref, m_ref, sr_ref,
                          h_st, h_q, h_k, h_b, h_a, m_cn, m_m, r_st, r_dec, *, layer, heads, slabs):
    h_hgrn, h_mlstm, h_ret = heads
    o_hgrn, o_mlstm, o_ret = slabs
    mixers = [
        _hgrn_prompt_parts(zh_ref, lbl_ref, gnh_ref, o_ref, sh_ref, h_st, h_q, h_k, h_b, h_a,
                           layer=layer, n_heads=h_hgrn, slab0=o_hgrn),
        _mlstm_prompt_parts(zm_ref, zg_ref, bias_ref, gnm_ref, o_ref, c_ref, n_ref, m_ref, m_cn, m_m,
                            n_heads=h_mlstm, slab0=o_mlstm),
        _ret_prompt_parts(zr_ref, cos_ref, sin_ref, lg_ref, gnr_ref, o_ref, sr_ref, r_st, r_dec,
                          n_heads=h_ret, slab0=o_ret),
    ]
    for init, _, _ in mixers:
        init()

    def block(i, carry):
        for _, run, _ in mixers:
            run(i)
        return carry

    lax.fori_loop(0, zh_ref.shape[1] // BLK, block, 0)
    for _, _, final in mixers:
        final()


def kernel(x_prompt, x_sample, state_hgrn, state_mlstm_c, state_mlstm_n, state_mlstm_m, state_ret, norm_ffn1, ffn1_w_gate, ffn1_w_up, ffn1_w_down, norm_mix, w_in, hgrn_lb_logits, mlstm_gate_bias, gn_hgrn, gn_mlstm, gn_ret, w_out, norm_ffn2, ffn2_w_gate, ffn2_w_up, ffn2_w_down, norm_final):
    bp, t_len, d = x_prompt.shape
    bs, dec_len, _ = x_sample.shape
    depth = w_in.shape[0]
    h_hgrn, h_mlstm, h_ret = state_hgrn.shape[2], state_mlstm_c.shape[2], state_ret.shape[2]
    w_hgrn, w_mlstm, w_ret = h_hgrn * HEAD_DIM, h_mlstm * HEAD_DIM, h_ret * HEAD_DIM
    d_ff = ffn1_w_gate.shape[2]
    n_prompt = bp * t_len
    m_rows = n_prompt + bs
    seq_tile = min(SEQ_TILE, t_len)
    n_seq = t_len // seq_tile
    assert dec_len == 1 and bs == LANES and n_prompt % bs == 0 and t_len % seq_tile == 0 and seq_tile % BLK == 0
    row_block = n_prompt // bs
    step_block = n_prompt // STEP_ROWS
    tm_wide = _row_tile(m_rows, WIDE_ROW_TILE)
    tm_deep = _row_tile(m_rows, DEEP_ROW_TILE)
    assert n_prompt % STACK_TILE == 0
    o_mlstm, o_ret, o_hgrn = 0, h_mlstm, h_mlstm + h_ret
    assert o_ret % h_ret == 0 and o_hgrn % h_hgrn == 0
    n_slabs = h_hgrn + h_mlstm + h_ret
    o_order = tuple(range(o_hgrn, n_slabs)) + tuple(range(o_mlstm, o_ret)) + tuple(range(o_ret, o_hgrn))

    gate0 = 4 * w_hgrn + 4 * w_mlstm
    ret0 = gate0 + 2 * h_mlstm
    gate_pad = ((0, 0), (0, 0), (0, 2 * LANES - 2 * h_mlstm))
    w_in_hm = jnp.concatenate([w_in[:, :, :gate0], jnp.pad(w_in[:, :, gate0:ret0], gate_pad)], axis=2).astype(BF16)
    w_in_r = jnp.pad(w_in[:, :, ret0:], ((0, 0), (0, 0), (0, 2 * LANES))).astype(BF16)
    bias_pad = jnp.pad(mlstm_gate_bias, ((0, 0), (0, LANES - 2 * h_mlstm))).reshape(depth, 1, LANES)
    m_pad = jnp.pad(state_mlstm_m, ((0, 0), (0, 0), (0, LANES - h_mlstm)))
    gn_h = gn_hgrn.reshape(depth, 1, w_hgrn)
    gn_m = gn_mlstm.reshape(depth, 1, w_mlstm)
    gn_r = gn_ret.reshape(depth, 1, w_ret)
    log_gamma = jnp.log1p(-jnp.exp2(-5.0 - jnp.arange(h_ret, dtype=F32)))
    cos_p, sin_p = _rope_tables(t_len, min(ROPE_TILE, t_len), 0.0, 1.0)
    cos_s, sin_s = _rope_tables(8, 8, PAST_LEN, 0.0)

    smem = pl.BlockSpec(memory_space=pltpu.SMEM)
    o3_shape = (n_slabs, m_rows, LANES)
    head_sq = (HEAD_DIM, HEAD_DIM)

    def z_prompt(n_blocks):
        return pl.BlockSpec((n_blocks, seq_tile, LANES), lambda b, t: (0, b * n_seq + t, 0))

    def o_step(heads, slab0):
        return pl.BlockSpec((heads, STEP_ROWS, LANES), lambda g: (slab0 // heads, step_block + g, 0))

    x, n = _stack_rows(x_prompt.reshape(n_prompt, d), x_sample.reshape(bs, d), norm_ffn1[0])
    new_h_p, new_c_p, new_n_p, new_m_p, new_r_p = [], [], [], [], []
    new_n_s, new_m_s = [], []
    st_h = st_c = st_r = None

    def ffn(x, n, ssq, w_gate, w_up, w_down, layer, next_norm):
        mid = _matmul(n, [w_gate, w_up], layer, mode="swiglu", tm=tm_wide, tn=512, n_out=d_ff, row_ssq=ssq)
        down = dict(tm=tm_deep, tn=1024, n_out=d, resid=x, weight_buffers=1)
        if next_norm is None:
            return _matmul(mid, [w_down], layer, mode="resid", **down), None, None
        return _matmul(mid, [w_down], layer, mode="resid_gain", norm=next_norm, **down)

    ssq = None
    for layer in range(depth):
        x, n, ssq = ffn(x, n, ssq, ffn1_w_gate, ffn1_w_up, ffn1_w_down, layer, (norm_mix, layer))

        cols = dict(mode="cols", tm=tm_wide, row_ssq=ssq)
        z_h = _matmul(n, [w_in_hm], layer, tn=1024, n_out=4 * w_hgrn, **cols)
        z_m = _matmul(n, [w_in_hm], layer, tn=1024, n_out=4 * w_mlstm, col_off=4 * w_hgrn // 1024, **cols)
        z_r = _matmul(n, [w_in_r], layer, tn=1024, n_out=4 * w_ret, **cols)
        z_g = _matmul(n, [w_in_hm], layer, tn=2 * LANES, n_out=2 * LANES, col_off=gate0 // (2 * LANES), **cols)

        def gain2(width):
            return pl.BlockSpec((None, 1, width), lambda b, t: (layer, 0, 0))

        def gain1(width):
            return pl.BlockSpec((None, 1, width), lambda g: (layer, 0, 0))

        def state_p(heads):
            return ((bp, heads) + head_sq, F32, pl.BlockSpec((None, heads) + head_sq, lambda b, t: (b, 0, 0, 0)), None)

        vec = ((bp, h_mlstm, 1, LANES), F32, pl.BlockSpec((None, h_mlstm, 1, LANES), lambda b, t: (b, 0, 0, 0)), None)
        table = pl.BlockSpec((t_len, LANES), lambda b, t: (0, 0))
        o3, s_hp, c_p, n_p, m_p, s_rp = _mixer_call(
            functools.partial(_mixers_prompt_kernel, layer=layer, heads=(h_hgrn, h_mlstm, h_ret),
                              slabs=(o_hgrn, o_mlstm, o_ret)),
            "mixers_prompt", (bp, n_seq),
            [z_prompt(z_h.shape[0]), z_prompt(z_m.shape[0]), z_prompt(z_r.shape[0]),
             pl.BlockSpec((None, seq_tile, LANES), lambda b, t: (0, b * n_seq + t, 0)),
             pl.BlockSpec(hgrn_lb_logits.shape, lambda b, t: (0, 0)),
             pl.BlockSpec((None, 1, LANES), lambda b, t: (layer, 0, 0)),
             table, table, smem, gain2(w_hgrn), gain2(w_mlstm), gain2(w_ret)],
            [z_h, z_m, z_r, z_g, hgrn_lb_logits, bias_pad, cos_p, sin_p, log_gamma, gn_h, gn_m, gn_r],
            [(o3_shape, BF16, z_prompt(n_slabs), None), state_p(h_hgrn), state_p(h_mlstm), vec, vec, state_p(h_ret)],
            [pltpu.VMEM((h_hgrn,) + head_sq, F32)] + [pltpu.VMEM((h_hgrn, BLK, LANES), F32) for _ in range(4)]
            + [pltpu.VMEM((h_mlstm, HEAD_DIM, 2 * HEAD_DIM), F32), pltpu.VMEM((h_mlstm, 1, LANES), F32),
               pltpu.VMEM((h_ret,) + head_sq, F32), pltpu.VMEM((h_ret, 3, BLK, LANES), F32)],
            ("arbitrary", "arbitrary"))

        def z_step(n_blocks):
            return pl.BlockSpec((n_blocks, bs, LANES), lambda g: (0, row_block, 0))

        def state_in(state):
            tail = state.shape[2:]
            zeros = (0,) * len(tail)
            return pl.BlockSpec((None, STEP_ROWS) + tail, lambda g: (layer, g) + zeros)

        tposed = pltpu.VMEM((h_hgrn, HEAD_DIM, bs), F32)
        o3, st_h = _mixer_call(
            functools.partial(_hgrn_step_kernel, layer=layer, n_heads=h_hgrn), "hgrn_step", (bs // STEP_ROWS,),
            [z_step(4 * h_hgrn), state_in(state_hgrn), pl.BlockSpec(hgrn_lb_logits.shape, lambda g: (0, 0)), gain1(w_hgrn)],
            [z_h, state_hgrn, hgrn_lb_logits, gn_h],
            [(o3_shape, BF16, o_step(h_hgrn, o_hgrn), o3), _state_out(state_hgrn, layer, st_h)],
            [tposed, tposed], ("arbitrary",))

        tposed = pltpu.VMEM((h_mlstm, HEAD_DIM, bs), F32)
        rowrep = pltpu.VMEM((h_mlstm, bs, LANES), F32)
        tile = pl.BlockSpec((bs, LANES), lambda g: (0, 0))
        o3, st_c, n_s, m_s = _mixer_call(
            functools.partial(_mlstm_step_kernel, n_heads=h_mlstm), "mlstm_step", (bs // STEP_ROWS,),
            [z_step(4 * h_mlstm), pl.BlockSpec((None, bs, LANES), lambda g: (0, row_block, 0)),
             pl.BlockSpec((None, 1, LANES), lambda g: (layer, 0, 0)),
             state_in(state_mlstm_c), state_in(state_mlstm_n), tile, gain1(w_mlstm)],
            [z_m, z_g, bias_pad, state_mlstm_c, state_mlstm_n, m_pad[layer], gn_m],
            [(o3_shape, BF16, o_step(h_mlstm, o_mlstm), o3), _state_out(state_mlstm_c, layer, st_c),
             (state_mlstm_n.shape[1:], F32, pl.BlockSpec((STEP_ROWS, h_mlstm, HEAD_DIM), lambda g: (g, 0, 0)), None),
             ((bs, LANES), F32, tile, None)],
            [tposed, rowrep, rowrep, rowrep], ("arbitrary",))

        tposed = pltpu.VMEM((h_ret, HEAD_DIM, bs), F32)
        table = pl.BlockSpec(cos_s.shape, lambda g: (0, 0))
        o3, st_r = _mixer_call(
            functools.partial(_ret_step_kernel, n_heads=h_ret), "ret_step", (bs // STEP_ROWS,),
            [z_step(4 * h_ret), state_in(state_ret), table, table, smem, gain1(w_ret)],
            [z_r, state_ret, cos_s, sin_s, log_gamma, gn_r],
            [(o3_shape, BF16, o_step(h_ret, o_ret), o3), _state_out(state_ret, layer, st_r)],
            [tposed], ("arbitrary",))

        new_h_p.append(s_hp)
        new_c_p.append(c_p)
        new_n_p.append(n_p[:, :, 0, :])
        new_m_p.append(m_p[:, :, 0, 0])
        new_r_p.append(s_rp)
        new_n_s.append(n_s)
        new_m_s.append(m_s[:, :h_mlstm])

        x, n = _matmul(o3, [w_out], layer, mode="resid_norm", tm=tm_deep, tn=d, n_out=d, resid=x,
                       norm=(norm_ffn2, layer), weight_buffers=1, lhs_order=o_order)

        next_norm = (norm_ffn1, layer + 1) if layer + 1 < depth else None
        x, n, ssq = ffn(x, n, None, ffn2_w_gate, ffn2_w_up, ffn2_w_down, layer, next_norm)

    y_p, y_s = _rmsnorm_split(x, norm_final, n_prompt)
    return (
        y_p.reshape(bp, t_len, d), y_s.reshape(bs, dec_len, d),
        jnp.stack(new_h_p), jnp.stack(new_c_p), jnp.stack(new_n_p), jnp.stack(new_m_p), jnp.stack(new_r_p),
        st_h, st_c, jnp.stack(new_n_s), jnp.stack(new_m_s), st_r,
    )
```

```python
import functools

import jax
import jax.numpy as jnp
from jax import lax
from jax.experimental import pallas as pl
from jax.experimental.pallas import tpu as pltpu

F32 = jnp.float32
BF16 = jnp.bfloat16

HEAD_DIM = 128
LANES = 128
EPS = 1e-6
ROPE_BASE = 10000.0
K_SCALE = HEAD_DIM ** -0.5
PAST_LEN = 16384

BLK = 128
SUB = 32
SEQ_TILE = 512
STEP_ROWS = 16
STEP_UNROLL = 4
STACK_TILE = 1024
ROW_PARTS = (5, 2, 1)
WIDE_ROW_TILE = 2080
DEEP_ROW_TILE = 640
ROPE_TILE = 256
VMEM_LIMIT_BYTES = 60 * 1024 * 1024

DECAY_SPAN_MAX = 60.0

_TRANS_B = (((1,), (1,)), ((), ()))
_TRANS_A = (((0,), (0,)), ((), ()))


def _row_tile(rows, limit):
    return max(t for t in range(16, min(rows, limit) + 1, 16) if rows % t == 0)


def _params(*semantics):
    return pltpu.CompilerParams(dimension_semantics=semantics, vmem_limit_bytes=VMEM_LIMIT_BYTES)


def _silu(x):
    return x * jax.nn.sigmoid(x)


def _softplus_neg_abs(x):
    return jnp.log(1.0 + jnp.exp(-jnp.abs(x)))


def _log_sigmoid(x):
    return jnp.minimum(x, 0.0) - _softplus_neg_abs(x)


def _iota(shape, dim):
    return lax.broadcasted_iota(jnp.int32, shape, dim)


def _seg_cumsum(x, seg):
    pos = _iota(x.shape, 0) & (seg - 1)
    shift = 1
    while shift < seg:
        x = x + jnp.where(pos >= shift, pltpu.roll(x, shift, 0), 0.0)
        shift *= 2
    return x


def _head_norm_gate(o, gain, gate):
    return o * lax.rsqrt(jnp.mean(o * o, axis=-1, keepdims=True) + EPS) * gain * gate


def _pick_lane(x, lane_mask):
    return jnp.sum(jnp.where(lane_mask, x, 0.0), axis=-1, keepdims=True)


def _put_row(tile, j, new):
    return jnp.where(_iota(tile.shape, 0) == j, new, tile)


def _stack_rows_kernel(a_ref, b_ref, g_ref, o_ref, n_ref, *, n_a):
    i = pl.program_id(0)

    def emit(x, rows):
        o_ref[rows, :] = x
        y = x * lax.rsqrt(jnp.mean(x * x, axis=-1, keepdims=True) + EPS)
        n_ref[rows, :] = (y * g_ref[...]).astype(n_ref.dtype)

    @pl.when(i < n_a)
    def _():
        emit(a_ref[...], slice(None))

    @pl.when(i == n_a)
    def _():
        emit(b_ref[...], slice(0, b_ref.shape[0]))


def _stack_rows(a, b, gain):
    (ma, d), mb = a.shape, b.shape[0]
    n_a = ma // STACK_TILE
    tile = pl.BlockSpec((STACK_TILE, d), lambda i: (i, 0))
    return pl.pallas_call(
        functools.partial(_stack_rows_kernel, n_a=n_a),
        out_shape=(jax.ShapeDtypeStruct((ma + mb, d), a.dtype), jax.ShapeDtypeStruct((ma + mb, d), BF16)),
        grid=(n_a + 1,),
        in_specs=[
            pl.BlockSpec((STACK_TILE, d), lambda i: (jnp.minimum(i, n_a - 1), 0)),
            pl.BlockSpec((mb, d), lambda i: (0, 0)),
            pl.BlockSpec((1, d), lambda i: (0, 0)),
        ],
        out_specs=(tile, tile),
        compiler_params=_params("arbitrary"),
        name="stack_rows",
    )(a, b, gain.reshape(1, d))


def _rmsnorm_split_kernel(x_ref, g_ref, a_ref, b_ref, *, n_a):
    i = pl.program_id(0)
    x = x_ref[...]
    y = x * lax.rsqrt(jnp.mean(x * x, axis=-1, keepdims=True) + EPS) * g_ref[...]

    @pl.when(i < n_a)
    def _():
        a_ref[...] = y

    @pl.when(i == n_a)
    def _():
        b_ref[...] = y[0:b_ref.shape[0], :]


def _rmsnorm_split(x, gain, rows_a):
    m, d = x.shape
    n_a = rows_a // STACK_TILE
    rows_b = m - rows_a
    return pl.pallas_call(
        functools.partial(_rmsnorm_split_kernel, n_a=n_a),
        out_shape=(jax.ShapeDtypeStruct((rows_a, d), F32), jax.ShapeDtypeStruct((rows_b, d), F32)),
        grid=(n_a + 1,),
        in_specs=[
            pl.BlockSpec((STACK_TILE, d), lambda i: (i, 0)),
            pl.BlockSpec((1, d), lambda i: (0, 0)),
        ],
        out_specs=(
            pl.BlockSpec((STACK_TILE, d), lambda i: (jnp.minimum(i, n_a - 1), 0)),
            pl.BlockSpec((rows_b, d), lambda i: (0, 0)),
        ),
        compiler_params=_params("arbitrary"),
        name="rmsnorm_split",
    )(x, gain.reshape(1, d))


def _matmul_kernel(*refs, n_w, mode, lhs_order, scaled, parts, recast):
    lhs_ref = refs[0]
    w_refs = refs[1:1 + n_w]
    rest = refs[1 + n_w:]
    if scaled:
        ssq_ref, rest = rest[0], rest[1:]
    if recast:
        rest[-1][...] = rest[0][...].astype(BF16)
        rest = rest[1:-1]
    tm = lhs_ref.shape[0] if lhs_order is None else lhs_ref.shape[1]
    row_parts = [slice(p * (tm // parts), (p + 1) * (tm // parts)) for p in range(parts)]

    def lhs_rows(rs):
        if lhs_order is None:
            return lhs_ref[rs, :]
        return jnp.concatenate([lhs_ref[c, rs, :] for c in lhs_order], axis=1)

    ws = [w_ref[...].astype(BF16) for w_ref in w_refs]
    accs = [[jnp.dot(lhs_rows(rs), w, preferred_element_type=F32) for w in ws] for rs in row_parts]
    for rs, acc in zip(row_parts, accs):
        if scaled:
            total = ssq_ref[0, rs, :]
            for part in range(1, ssq_ref.shape[0]):
                total = total + ssq_ref[part, rs, :]
            r = lax.rsqrt(total[:, 0:1] * (1.0 / ws[0].shape[0]) + EPS)
            acc = [a * r for a in acc]
        if mode == "swiglu":
            (o_ref,) = rest
            gate, up = acc
            o_ref[rs, :] = (0.5 * _silu(gate) * up).astype(o_ref.dtype)
        elif mode == "resid":
            x_ref, o_ref = rest
            o_ref[rs, :] = x_ref[rs, :] + acc[0]
        elif mode == "resid_gain":
            x_ref, g_ref, o_ref, xg_ref, sq_ref = rest
            x = x_ref[rs, :] + acc[0]
            o_ref[rs, :] = x
            xg_ref[rs, :] = (x * g_ref[...]).astype(xg_ref.dtype)
            sq_ref[rs, :] = jnp.broadcast_to(jnp.sum(x * x, axis=-1, keepdims=True), (x.shape[0], LANES))
        elif mode == "resid_norm":
            x_ref, g_ref, o_ref, n_ref = rest
            x = x_ref[rs, :] + acc[0]
            o_ref[rs, :] = x
            y = x * lax.rsqrt(jnp.mean(x * x, axis=-1, keepdims=True) + EPS)
            n_ref[rs, :] = (y * g_ref[...]).astype(n_ref.dtype)
        else:
            (o_ref,) = rest
            for c in range(o_ref.shape[0]):
                o_ref[c, rs, :] = acc[0][:, c * LANES:(c + 1) * LANES]


def _matmul(lhs, weights, layer, *, mode, tm, tn, n_out, col_off=0, resid=None, norm=None, row_ssq=None,
            recast=None, weight_buffers=2, lhs_order=None):
    if lhs_order is None:
        m, k = lhs.shape
        lhs_spec = pl.BlockSpec((tm, k), lambda j, i: (i, 0))
    else:
        m, k = lhs.shape[1], lhs.shape[0] * LANES
        lhs_spec = pl.BlockSpec((lhs.shape[0], tm, LANES), lambda j, i: (0, i, 0))
    n_w = len(weights)
    n_tiles = pl.cdiv(n_out, tn)
    grid = (n_tiles, m // tm)
    in_specs = [lhs_spec]
    in_specs += [pl.BlockSpec((None, k, tn), lambda j, i: (layer, 0, j + col_off),
                              pipeline_mode=pl.Buffered(weight_buffers)) for _ in weights]
    args = [lhs, *weights]
    if row_ssq is not None:
        in_specs.append(pl.BlockSpec((row_ssq.shape[0], tm, LANES), lambda j, i: (0, i, 0)))
        args.append(row_ssq)
    if recast is not None:
        other, other_layer = recast
        rows, width = other.shape[1:]
        n_m = m // tm
        chunk = -(-pl.cdiv(rows, n_tiles * n_m) // 16) * 16
        last = pl.cdiv(rows, chunk) - 1
        in_specs.append(pl.BlockSpec((None, chunk, width),
                                     lambda j, i: (other_layer, jnp.minimum(j * n_m + i, last), 0)))
        args.append(other)
        recast_shape = jax.ShapeDtypeStruct((1, rows, width), BF16)
        recast_spec = pl.BlockSpec((None, chunk, width), lambda j, i: (0, jnp.minimum(j * n_m + i, last), 0))
    tile = pl.BlockSpec((tm, tn), lambda j, i: (i, j))
    if mode in ("resid", "resid_gain", "resid_norm"):
        in_specs.append(tile)
        args.append(resid)
    if mode in ("resid_gain", "resid_norm"):
        gains, norm_layer = norm
        in_specs.append(pl.BlockSpec((None, 1, tn), lambda j, i: (norm_layer, 0, j)))
        args.append(gains.reshape(-1, 1, n_out))
    if mode == "resid_gain":
        out_shape = (jax.ShapeDtypeStruct((m, n_out), F32), jax.ShapeDtypeStruct((m, n_out), BF16),
                     jax.ShapeDtypeStruct((n_tiles, m, LANES), F32))
        out_spec = (tile, tile, pl.BlockSpec((None, tm, LANES), lambda j, i: (j, i, 0)))
    elif mode == "resid_norm":
        assert tn == n_out
        out_shape = (jax.ShapeDtypeStruct((m, n_out), F32), jax.ShapeDtypeStruct((m, n_out), BF16))
        out_spec = (tile, tile)
    elif mode == "cols":
        out_shape = jax.ShapeDtypeStruct((n_out // LANES, m, LANES), F32)
        out_spec = pl.BlockSpec((tn // LANES, tm, LANES), lambda j, i: (j, i, 0))
    else:
        out_shape = jax.ShapeDtypeStruct((m, n_out), BF16 if mode == "swiglu" else F32)
        out_spec = tile
    if recast is not None:
        outs = out_shape if isinstance(out_shape, tuple) else (out_shape,)
        specs = out_spec if isinstance(out_spec, tuple) else (out_spec,)
        out_shape, out_spec = outs + (recast_shape,), specs + (recast_spec,)
    return pl.pallas_call(
        functools.partial(_matmul_kernel, n_w=n_w, mode=mode, lhs_order=lhs_order, scaled=row_ssq is not None,
                          parts=next(p for p in ROW_PARTS if p == 1 or tm % (16 * p) == 0),
                          recast=recast is not None),
        out_shape=out_shape,
        grid=grid,
        in_specs=in_specs,
        out_specs=out_spec,
        compiler_params=_params("arbitrary", "arbitrary"),
        name="matmul_" + mode,
    )(*args)


def _rope_kernel(inv_ref, cos_ref, sin_ref, *, pos0, step):
    shape = cos_ref.shape
    row = _iota(shape, 0) + pl.program_id(0) * shape[0]
    ang = (pos0 + step * row.astype(F32)) * inv_ref[...]
    sin = jnp.sin(ang)
    cos_ref[...] = jnp.cos(ang)
    sin_ref[...] = jnp.where(_iota(shape, 1) < HEAD_DIM // 2, -sin, sin)


def _rope_tables(rows, tile, pos0, step):
    inv = ROPE_BASE ** (-jnp.arange(0, HEAD_DIM, 2, dtype=F32) / HEAD_DIM)
    inv = jnp.concatenate([inv, inv]).reshape(1, HEAD_DIM)
    out = jax.ShapeDtypeStruct((rows, HEAD_DIM), F32)
    spec = pl.BlockSpec((tile, HEAD_DIM), lambda i: (i, 0))
    return pl.pallas_call(
        functools.partial(_rope_kernel, pos0=float(pos0), step=float(step)),
        out_shape=(out, out),
        grid=(rows // tile,),
        in_specs=[pl.BlockSpec((1, HEAD_DIM), lambda i: (0, 0))],
        out_specs=(spec, spec),
        compiler_params=_params("arbitrary"),
        name="rope_tables",
    )(inv)


def _rope(x, cos, sin):
    return x * cos + pltpu.roll(x, HEAD_DIM // 2, 1) * sin


def _mixer_call(kernel_fn, name, grid, in_specs, args, outs, scratch, semantics):
    in_specs, args, aliases = list(in_specs), list(args), {}
    for k, (_, _, _, prev) in enumerate(outs):
        if prev is not None:
            aliases[len(args)] = k
            in_specs.append(pl.BlockSpec(memory_space=pl.ANY))
            args.append(prev)
    n_alias = len(aliases)

    def body(*refs):
        n_in = len(args) - n_alias
        kernel_fn(*refs[:n_in], *refs[n_in + n_alias:])

    return pl.pallas_call(
        body,
        out_shape=tuple(jax.ShapeDtypeStruct(s, dt) for s, dt, _, _ in outs),
        grid=grid,
        in_specs=in_specs,
        out_specs=tuple(spec for _, _, spec, _ in outs),
        scratch_shapes=scratch,
        input_output_aliases=aliases,
        compiler_params=_params(*semantics),
        name=name,
    )(*args)


def _state_out(state, layer, prev):
    tail = state.shape[2:]
    zeros = (0,) * len(tail)
    spec = pl.BlockSpec((None, STEP_ROWS) + tail, lambda g: (layer, g) + zeros)
    return (state.shape, F32, spec, prev)


def _hgrn_lower_bound(logits, layer):
    mx = jnp.max(logits, axis=0, keepdims=True)
    e = jnp.exp(logits - mx)
    p = e / jnp.sum(e, axis=0, keepdims=True)
    if layer == 0:
        return jnp.zeros_like(mx)
    return jnp.sum(p[1:layer + 1], axis=0, keepdims=True)


def _hgrn_gates(fz, lb):
    log_lb = jnp.log(lb)
    y = jnp.log1p(-lb) + _log_sigmoid(fz)
    log_f = jnp.maximum(log_lb, y) + _softplus_neg_abs(log_lb - y)
    key = (1.0 - lb) / (1.0 + jnp.exp(fz))
    return log_f, key


def _hgrn_prompt_parts(z_ref, lbl_ref, gn_ref, o_ref, s_ref, st_ref, q_scr, k_scr, b_scr, a_scr, *, layer, n_heads,
                       slab0):
    t = pl.program_id(1)
    n_sub = BLK // SUB
    lb = _hgrn_lower_bound(lbl_ref[...], layer)
    row = _iota((BLK, LANES), 0)
    col = _iota((BLK, LANES), 1)
    pos = row & (SUB - 1)
    seg0 = row - pos

    def init():
        @pl.when(t == 0)
        def _():
            st_ref[...] = jnp.zeros_like(st_ref)

    def seg_row(x, s):
        x3 = x.reshape(n_sub, SUB, LANES)
        return jnp.broadcast_to(x3[:, s:s + 1, :], x3.shape).reshape(BLK, LANES)

    def block(i):
        rows = pl.ds(pl.multiple_of(i * BLK, BLK), BLK)
        span = jnp.zeros((BLK, LANES), F32)
        for h in range(n_heads):
            log_f, k = _hgrn_gates(z_ref[n_heads + h, rows, :], lb[:, h * LANES:(h + 1) * LANES])
            b = _seg_cumsum(log_f, SUB)
            q_scr[h] = _silu(z_ref[h, rows, :])
            k_scr[h] = k
            b_scr[h] = b
            span = jnp.maximum(span, -seg_row(b, SUB - 1))
        mild = jnp.max(span) <= DECAY_SPAN_MAX

        @pl.when(mild)
        def _():
            q_end, k_end = [], []
            for h in range(n_heads):
                b = b_scr[h]
                b_end = seg_row(b, SUB - 1)
                q_end.append((q_scr[h] * jnp.exp(b - b_end)).astype(BF16))
                k_end.append((k_scr[h] * jnp.exp(b_end - b)).astype(BF16))
            a = [lax.dot_general(q_end[h], k_end[h], _TRANS_B, preferred_element_type=F32) for h in range(n_heads)]
            for h in range(n_heads):
                a_scr[h] = jnp.where((col >= seg0) & (col <= row), a[h], 0.0)

        @pl.when(jnp.logical_not(mild))
        def _():
            for h in range(n_heads):
                q, k, b = q_scr[h], k_scr[h], b_scr[h]
                a = jnp.zeros((BLK, LANES), F32)
                for s in range(SUB):
                    e = jnp.exp(jnp.where(pos >= s, b - seg_row(b, s), -jnp.inf))
                    c = jnp.sum(q * seg_row(k, s) * e, axis=-1, keepdims=True)
                    a = jnp.where(col == seg0 + s, c, a)
                a_scr[h] = a

        heads = range(n_heads)
        subs = [slice(n * SUB, (n + 1) * SUB) for n in range(n_sub)]
        vb, q_in, k_out, lam = [], [], [], []
        for h in heads:
            b = b_scr[h]
            b_end = seg_row(b, SUB - 1)
            vb.append(z_ref[2 * n_heads + h, rows, :].astype(BF16))
            q_in.append((q_scr[h] * jnp.exp(b)).astype(BF16))
            k_out.append((k_scr[h] * jnp.exp(b_end - b)).astype(BF16))
            lam.append(jnp.exp(b_end))
        ups = [[lax.dot_general(vb[h][sl], k_out[h][sl], _TRANS_A, preferred_element_type=F32) for sl in subs]
               for h in heads]
        o = [jnp.dot(a_scr[h].astype(BF16), vb[h], preferred_element_type=F32) for h in heads]
        states = []
        for h in heads:
            chain = [st_ref[h]]
            for n in range(n_sub):
                chain.append(chain[n] * lam[h][n * SUB:n * SUB + 1, :] + ups[h][n])
            st_ref[h] = chain[n_sub]
            states.append([s.astype(BF16) for s in chain[:n_sub]])
        parts = [[lax.dot_general(q_in[h][sl], states[h][n], _TRANS_B, preferred_element_type=F32)
                  for n, sl in enumerate(subs)] for h in heads]
        for h in heads:
            gate = _silu(z_ref[3 * n_heads + h, rows, :])
            out = _head_norm_gate(o[h] + jnp.concatenate(parts[h], axis=0), gn_ref[:, h * LANES:(h + 1) * LANES], gate)
            o_ref[slab0 + h, rows, :] = out.astype(o_ref.dtype)

    def final():
        @pl.when(t == pl.num_programs(1) - 1)
        def _():
            for h in range(n_heads):
                s_ref[h] = st_ref[h].T

    return init, block, final


def _hgrn_step_kernel(z_ref, s_ref, lbl_ref, gn_ref, o_ref, so_ref, f_scr, k_scr, *, layer, n_heads):
    g = pl.program_id(0)
    n_rows = z_ref.shape[1]

    @pl.when(g == 0)
    def _():
        lb = _hgrn_lower_bound(lbl_ref[...], layer)
        for h in range(n_heads):
            log_f, k = _hgrn_gates(z_ref[n_heads + h], lb[:, h * LANES:(h + 1) * LANES])
            f_scr[h] = jnp.exp(log_f).T
            k_scr[h] = k.T

    lane = _iota((HEAD_DIM, n_rows), 1)
    grp = pl.ds(pl.multiple_of(g * STEP_ROWS, STEP_ROWS), STEP_ROWS)
    q = [_silu(z_ref[h, grp, :]).astype(BF16) for h in range(n_heads)]

    def body(j, outs):
        b = g * STEP_ROWS + j
        sel = lane == b
        one = pl.ds(b, 1)
        new = []
        for h in range(n_heads):
            s_new = _pick_lane(f_scr[h], sel) * s_ref[j, h] + _pick_lane(k_scr[h], sel) * z_ref[2 * n_heads + h, one, :]
            so_ref[j, h] = s_new
            new.append(_put_row(outs[h], j, jnp.dot(q[h], s_new.astype(BF16), preferred_element_type=F32)))
        return tuple(new)

    zero = jnp.zeros((STEP_ROWS, LANES), F32)
    outs = lax.fori_loop(0, STEP_ROWS, body, (zero,) * n_heads, unroll=STEP_UNROLL)
    for h in range(n_heads):
        gate = _silu(z_ref[3 * n_heads + h, grp, :])
        o_ref[h] = _head_norm_gate(outs[h], gn_ref[:, h * LANES:(h + 1) * LANES], gate).astype(o_ref.dtype)


def _ret_prompt_parts(z_ref, cos_ref, sin_ref, lg_ref, gn_ref, o_ref, s_ref, st_ref, dec_ref, *, n_heads, slab0):
    t = pl.program_id(1)

    def init():
        @pl.when((pl.program_id(0) == 0) & (t == 0))
        def _():
            row = _iota((BLK, LANES), 0).astype(F32)
            diff = row - _iota((BLK, LANES), 1).astype(F32)
            for h in range(n_heads):
                lg = lg_ref[h]
                dec_ref[h, 0] = jnp.where(diff >= 0.0, jnp.exp(diff * lg), 0.0)
                dec_ref[h, 1] = jnp.exp((row + 1.0) * lg)
                dec_ref[h, 2] = jnp.exp((BLK - 1.0 - row) * lg)

        @pl.when(t == 0)
        def _():
            st_ref[...] = jnp.zeros_like(st_ref)

    def block(i):
        rows = pl.ds(pl.multiple_of(i * BLK, BLK), BLK)
        tab = pl.ds(pl.multiple_of(t * z_ref.shape[1] + i * BLK, BLK), BLK)
        cos = cos_ref[tab, :]
        sin = sin_ref[tab, :]
        heads = range(n_heads)
        q = [_rope(z_ref[h, rows, :], cos, sin) for h in heads]
        k = [_rope(z_ref[n_heads + h, rows, :], cos, sin) * K_SCALE for h in heads]
        vb = [z_ref[2 * n_heads + h, rows, :].astype(BF16) for h in heads]
        a = [lax.dot_general(q[h].astype(BF16), k[h].astype(BF16), _TRANS_B, preferred_element_type=F32)
             for h in heads]
        carried = [jnp.dot((q[h] * dec_ref[h, 1]).astype(BF16), st_ref[h].astype(BF16), preferred_element_type=F32)
                   for h in heads]
        update = [jnp.dot((k[h] * dec_ref[h, 2]).T.astype(BF16), vb[h], preferred_element_type=F32) for h in heads]
        o = [jnp.dot((a[h] * dec_ref[h, 0]).astype(BF16), vb[h], preferred_element_type=F32) for h in heads]
        for h in heads:
            s_dec = jnp.exp(jnp.full((1, LANES), BLK, F32) * lg_ref[h])
            st_ref[h] = s_dec * st_ref[h] + update[h]
            gate = _silu(z_ref[3 * n_heads + h, rows, :])
            out = _head_norm_gate(o[h] + carried[h], gn_ref[:, h * LANES:(h + 1) * LANES], gate)
            o_ref[slab0 + h, rows, :] = out.astype(o_ref.dtype)

    def final():
        @pl.when(t == pl.num_programs(1) - 1)
        def _():
            s_ref[...] = st_ref[...]

    return init, block, final


def _ret_step_kernel(z_ref, s_ref, cos_ref, sin_ref, lg_ref, gn_ref, o_ref, so_ref, k_scr, *, n_heads):
    g = pl.program_id(0)
    n_rows = z_ref.shape[1]
    cos = cos_ref[0:1, :]
    sin = sin_ref[0:1, :]

    @pl.when(g == 0)
    def _():
        for h in range(n_heads):
            k_scr[h] = (_rope(z_ref[n_heads + h], cos, sin) * K_SCALE).T

    lane = _iota((HEAD_DIM, n_rows), 1)
    grp = pl.ds(pl.multiple_of(g * STEP_ROWS, STEP_ROWS), STEP_ROWS)
    q = [_rope(z_ref[h, grp, :], cos, sin).astype(BF16) for h in range(n_heads)]

    def body(j, outs):
        b = g * STEP_ROWS + j
        sel = lane == b
        one = pl.ds(b, 1)
        new = []
        for h in range(n_heads):
            gamma = jnp.exp(jnp.full((1, LANES), lg_ref[h], F32))
            s_new = gamma * s_ref[j, h] + _pick_lane(k_scr[h], sel) * z_ref[2 * n_heads + h, one, :]
            so_ref[j, h] = s_new
            new.append(_put_row(outs[h], j, jnp.dot(q[h], s_new.astype(BF16), preferred_element_type=F32)))
        return tuple(new)

    zero = jnp.zeros((STEP_ROWS, LANES), F32)
    outs = lax.fori_loop(0, STEP_ROWS, body, (zero,) * n_heads, unroll=STEP_UNROLL)
    for h in range(n_heads):
        gate = _silu(z_ref[3 * n_heads + h, grp, :])
        o_ref[h] = _head_norm_gate(outs[h], gn_ref[:, h * LANES:(h + 1) * LANES], gate).astype(o_ref.dtype)


def _mlstm_prompt_parts(z_ref, zg_ref, bias_ref, gn_ref, o_ref, c_ref, n_ref, m_ref, cn_scr, m_scr, *, n_heads, slab0):
    t = pl.program_id(1)
    lane = _iota((BLK, LANES), 1)
    causal = lane <= _iota((BLK, LANES), 0)
    ones = jnp.ones((BLK, LANES), BF16)

    def twice(x):
        return jnp.concatenate([x, x], axis=1)

    def init():
        @pl.when(t == 0)
        def _():
            cn_scr[...] = jnp.zeros_like(cn_scr)
            m_scr[...] = jnp.zeros_like(m_scr)

    def block(i):
        rows = pl.ds(pl.multiple_of(i * BLK, BLK), BLK)
        gates = zg_ref[rows, :] + bias_ref[...]
        log_f_all = pltpu.roll(_log_sigmoid(gates), LANES - n_heads, 1)
        bt_all = _seg_cumsum(log_f_all, BLK)
        w_t = (gates - bt_all).T
        heads = range(n_heads)
        d_mat, w_inter, m_t, w_c, kw_t, qb, kb, v_ones = [], [], [], [], [], [], [], []
        for h in heads:
            bt = _pick_lane(bt_all, lane == h)
            w_row = w_t[h:h + 1, :]
            log_d = jnp.where(causal, bt + w_row, -jnp.inf)
            m_prev = m_scr[h]
            log_inter = bt + m_prev
            m_t.append(jnp.maximum(log_inter, jnp.max(log_d, axis=-1, keepdims=True)))
            d_mat.append(jnp.exp(log_d - m_t[h]))
            w_inter.append(jnp.exp(log_inter - m_t[h]))
            m_end = m_t[h][BLK - 1:BLK, :]
            b_end = bt[BLK - 1:BLK, :]
            k = z_ref[n_heads + h, rows, :] * K_SCALE
            kb.append(k.astype(BF16))
            kw_t.append((k.T * jnp.exp(b_end + w_row - m_end)).astype(BF16))
            w_c.append(jnp.exp(b_end + m_prev - m_end))
            qb.append(z_ref[h, rows, :].astype(BF16))
            v_ones.append(jnp.concatenate([z_ref[2 * n_heads + h, rows, :].astype(BF16), ones], axis=1))
        sc = [lax.dot_general(qb[h], kb[h], _TRANS_B, preferred_element_type=F32) for h in heads]
        carried = [jnp.dot(qb[h], cn_scr[h].astype(BF16), preferred_element_type=F32) for h in heads]
        update = [jnp.dot(kw_t[h], v_ones[h], preferred_element_type=F32) for h in heads]
        both = [jnp.dot((sc[h] * d_mat[h]).astype(BF16), v_ones[h], preferred_element_type=F32) for h in heads]
        for h in heads:
            tot = both[h] + carried[h] * twice(w_inter[h])
            num, den = tot[:, :LANES], tot[:, LANES:]
            hid = num / jnp.maximum(jnp.abs(den), jnp.exp(-m_t[h]))
            cn_scr[h] = twice(w_c[h]) * cn_scr[h] + update[h]
            m_scr[h] = m_t[h][BLK - 1:BLK, :]
            gate = jax.nn.sigmoid(z_ref[3 * n_heads + h, rows, :])
            out = _head_norm_gate(hid, gn_ref[:, h * LANES:(h + 1) * LANES], gate)
            o_ref[slab0 + h, rows, :] = out.astype(o_ref.dtype)

    def final():
        @pl.when(t == pl.num_programs(1) - 1)
        def _():
            for h in range(n_heads):
                c_ref[h] = cn_scr[h, :, :LANES]
                n_ref[h] = cn_scr[h, :, LANES:].T[0:1, :]
            m_ref[...] = m_scr[...]

    return init, block, final


def _mlstm_step_kernel(z_ref, zg_ref, bias_ref, c_ref, n_ref, m_ref, gn_ref,
                       o_ref, co_ref, no_ref, mo_ref, k_scr, wc_scr, wk_scr, em_scr, *, n_heads):
    g = pl.program_id(0)
    n_rows = z_ref.shape[1]

    @pl.when(g == 0)
    def _():
        lanes = _iota((n_rows, LANES), 1)
        gates = zg_ref[...] + bias_ref[...]
        log_f = pltpu.roll(_log_sigmoid(gates), LANES - n_heads, 1)
        m_old = m_ref[...]
        m_new = jnp.maximum(log_f + m_old, gates)
        mo_ref[...] = m_new
        w_c = jnp.exp(log_f + m_old - m_new)
        w_k = jnp.exp(gates - m_new)
        e_m = jnp.exp(-m_new)
        for h in range(n_heads):
            sel = lanes == h
            wc_scr[h] = jnp.broadcast_to(_pick_lane(w_c, sel), (n_rows, LANES))
            wk_scr[h] = jnp.broadcast_to(_pick_lane(w_k, sel), (n_rows, LANES))
            em_scr[h] = jnp.broadcast_to(_pick_lane(e_m, sel), (n_rows, LANES))
            k_scr[h] = (z_ref[n_heads + h] * K_SCALE).T

    lane = _iota((HEAD_DIM, n_rows), 1)
    grp = pl.ds(pl.multiple_of(g * STEP_ROWS, STEP_ROWS), STEP_ROWS)
    q = [z_ref[h, grp, :].astype(BF16) for h in range(n_heads)]

    def body(j, outs):
        b = g * STEP_ROWS + j
        sel = lane == b
        one = pl.ds(b, 1)
        new = []
        for h in range(n_heads):
            w_c = wc_scr[h, one, :]
            w_k = wk_scr[h, one, :]
            c_new = w_c * c_ref[j, h] + (w_k * _pick_lane(k_scr[h], sel)) * z_ref[2 * n_heads + h, one, :]
            co_ref[j, h] = c_new
            n_new = w_c * n_ref[j, h:h + 1, :] + w_k * (z_ref[n_heads + h, one, :] * K_SCALE)
            no_ref[j, h:h + 1, :] = n_new
            num = jnp.dot(q[h], c_new.astype(BF16), preferred_element_type=F32)
            new.append(_put_row(outs[2 * h], j, num))
            new.append(_put_row(outs[2 * h + 1], j, n_new))
        return tuple(new)

    zero = jnp.zeros((STEP_ROWS, LANES), F32)
    outs = lax.fori_loop(0, STEP_ROWS, body, (zero,) * (2 * n_heads), unroll=STEP_UNROLL)
    for h in range(n_heads):
        num, n_new = outs[2 * h], outs[2 * h + 1]
        den = jnp.sum(z_ref[h, grp, :] * n_new, axis=-1, keepdims=True)
        hid = num / jnp.maximum(jnp.abs(den), em_scr[h, grp, :])
        gate = jax.nn.sigmoid(z_ref[3 * n_heads + h, grp, :])
        o_ref[h] = _head_norm_gate(hid, gn_ref[:, h * LANES:(h + 1) * LANES], gate).astype(o_ref.dtype)


def _mixers_prompt_kernel(zh_ref, zm_ref, zr_ref, zg_ref, lbl_ref, bias_ref, cos_ref, sin_ref, lg_ref, gnh_ref, gnm_ref,
                          gnr_ref, o_ref, sh_ref, c_ref, n_ref, m_ref, sr_ref,
                          h_st, h_q, h_k, h_b, h_a, m_cn, m_m, r_st, r_dec, *, layer, heads, slabs):
    h_hgrn, h_mlstm, h_ret = heads
    o_hgrn, o_mlstm, o_ret = slabs
    mixers = [
        _hgrn_prompt_parts(zh_ref, lbl_ref, gnh_ref, o_ref, sh_ref, h_st, h_q, h_k, h_b, h_a,
                           layer=layer, n_heads=h_hgrn, slab0=o_hgrn),
        _mlstm_prompt_parts(zm_ref, zg_ref, bias_ref, gnm_ref, o_ref, c_ref, n_ref, m_ref, m_cn, m_m,
                            n_heads=h_mlstm, slab0=o_mlstm),
        _ret_prompt_parts(zr_ref, cos_ref, sin_ref, lg_ref, gnr_ref, o_ref, sr_ref, r_st, r_dec,
                          n_heads=h_ret, slab0=o_ret),
    ]
    for init, _, _ in mixers:
        init()

    def block(i, carry):
        for _, run, _ in mixers:
            run(i)
        return carry

    lax.fori_loop(0, zh_ref.shape[1] // BLK, block, 0)
    for _, _, final in mixers:
        final()


def kernel(x_prompt, x_sample, state_hgrn, state_mlstm_c, state_mlstm_n, state_mlstm_m, state_ret, norm_ffn1, ffn1_w_gate, ffn1_w_up, ffn1_w_down, norm_mix, w_in, hgrn_lb_logits, mlstm_gate_bias, gn_hgrn, gn_mlstm, gn_ret, w_out, norm_ffn2, ffn2_w_gate, ffn2_w_up, ffn2_w_down, norm_final):
    bp, t_len, d = x_prompt.shape
    bs, dec_len, _ = x_sample.shape
    depth = w_in.shape[0]
    h_hgrn, h_mlstm, h_ret = state_hgrn.shape[2], state_mlstm_c.shape[2], state_ret.shape[2]
    w_hgrn, w_mlstm, w_ret = h_hgrn * HEAD_DIM, h_mlstm * HEAD_DIM, h_ret * HEAD_DIM
    d_ff = ffn1_w_gate.shape[2]
    n_prompt = bp * t_len
    m_rows = n_prompt + bs
    seq_tile = min(SEQ_TILE, t_len)
    n_seq = t_len // seq_tile
    assert dec_len == 1 and bs == LANES and n_prompt % bs == 0 and t_len % seq_tile == 0 and seq_tile % BLK == 0
    row_block = n_prompt // bs
    step_block = n_prompt // STEP_ROWS
    tm_wide = _row_tile(m_rows, WIDE_ROW_TILE)
    tm_deep = _row_tile(m_rows, DEEP_ROW_TILE)
    assert n_prompt % STACK_TILE == 0
    o_mlstm, o_ret, o_hgrn = 0, h_mlstm, h_mlstm + h_ret
    assert o_ret % h_ret == 0 and o_hgrn % h_hgrn == 0
    n_slabs = h_hgrn + h_mlstm + h_ret
    o_order = tuple(range(o_hgrn, n_slabs)) + tuple(range(o_mlstm, o_ret)) + tuple(range(o_ret, o_hgrn))

    gate0 = 4 * w_hgrn + 4 * w_mlstm
    ret0 = gate0 + 2 * h_mlstm
    gate_pad = ((0, 0), (0, 0), (0, 2 * LANES - 2 * h_mlstm))
    w_in_hm = jnp.concatenate([w_in[:, :, :gate0], jnp.pad(w_in[:, :, gate0:ret0], gate_pad)], axis=2).astype(BF16)
    w_in_r = jnp.pad(w_in[:, :, ret0:], ((0, 0), (0, 0), (0, 2 * LANES))).astype(BF16)
    bias_pad = jnp.pad(mlstm_gate_bias, ((0, 0), (0, LANES - 2 * h_mlstm))).reshape(depth, 1, LANES)
    m_pad = jnp.pad(state_mlstm_m, ((0, 0), (0, 0), (0, LANES - h_mlstm)))
    gn_h = gn_hgrn.reshape(depth, 1, w_hgrn)
    gn_m = gn_mlstm.reshape(depth, 1, w_mlstm)
    gn_r = gn_ret.reshape(depth, 1, w_ret)
    log_gamma = jnp.log1p(-jnp.exp2(-5.0 - jnp.arange(h_ret, dtype=F32)))
    cos_p, sin_p = _rope_tables(t_len, min(ROPE_TILE, t_len), 0.0, 1.0)
    cos_s, sin_s = _rope_tables(8, 8, PAST_LEN, 0.0)

    smem = pl.BlockSpec(memory_space=pltpu.SMEM)
    o3_shape = (n_slabs, m_rows, LANES)
    head_sq = (HEAD_DIM, HEAD_DIM)

    def z_prompt(n_blocks):
        return pl.BlockSpec((n_blocks, seq_tile, LANES), lambda b, t: (0, b * n_seq + t, 0))

    def o_step(heads, slab0):
        return pl.BlockSpec((heads, STEP_ROWS, LANES), lambda g: (slab0 // heads, step_block + g, 0))

    x, n = _stack_rows(x_prompt.reshape(n_prompt, d), x_sample.reshape(bs, d), norm_ffn1[0])
    new_h_p, new_c_p, new_n_p, new_m_p, new_r_p = [], [], [], [], []
    new_n_s, new_m_s = [], []
    st_h = st_c = st_r = None

    def ffn(x, n, ssq, w_gate, w_up, w_down, layer, next_norm):
        mid, w_down16 = _matmul(n, [w_gate, w_up], layer, mode="swiglu", tm=tm_wide, tn=512, n_out=d_ff, row_ssq=ssq,
                                recast=(w_down, layer))
        down = dict(tm=tm_deep, tn=1024, n_out=d, resid=x)
        if next_norm is None:
            return _matmul(mid, [w_down16], 0, mode="resid", **down), None, None
        return _matmul(mid, [w_down16], 0, mode="resid_gain", norm=next_norm, **down)

    ssq = None
    for layer in range(depth):
        x, n, ssq = ffn(x, n, ssq, ffn1_w_gate, ffn1_w_up, ffn1_w_down, layer, (norm_mix, layer))

        cols = dict(mode="cols", tm=tm_wide, row_ssq=ssq)
        z_h = _matmul(n, [w_in_hm], layer, tn=1024, n_out=4 * w_hgrn, **cols)
        z_m = _matmul(n, [w_in_hm], layer, tn=1024, n_out=4 * w_mlstm, col_off=4 * w_hgrn // 1024, **cols)
        z_r = _matmul(n, [w_in_r], layer, tn=1024, n_out=4 * w_ret, **cols)
        z_g = _matmul(n, [w_in_hm], layer, tn=2 * LANES, n_out=2 * LANES, col_off=gate0 // (2 * LANES), **cols)

        def gain2(width):
            return pl.BlockSpec((None, 1, width), lambda b, t: (layer, 0, 0))

        def gain1(width):
            return pl.BlockSpec((None, 1, width), lambda g: (layer, 0, 0))

        def state_p(heads):
            return ((bp, heads) + head_sq, F32, pl.BlockSpec((None, heads) + head_sq, lambda b, t: (b, 0, 0, 0)), None)

        vec = ((bp, h_mlstm, 1, LANES), F32, pl.BlockSpec((None, h_mlstm, 1, LANES), lambda b, t: (b, 0, 0, 0)), None)
        table = pl.BlockSpec((t_len, LANES), lambda b, t: (0, 0))
        o3, s_hp, c_p, n_p, m_p, s_rp = _mixer_call(
            functools.partial(_mixers_prompt_kernel, layer=layer, heads=(h_hgrn, h_mlstm, h_ret),
                              slabs=(o_hgrn, o_mlstm, o_ret)),
            "mixers_prompt", (bp, n_seq),
            [z_prompt(z_h.shape[0]), z_prompt(z_m.shape[0]), z_prompt(z_r.shape[0]),
             pl.BlockSpec((None, seq_tile, LANES), lambda b, t: (0, b * n_seq + t, 0)),
             pl.BlockSpec(hgrn_lb_logits.shape, lambda b, t: (0, 0)),
             pl.BlockSpec((None, 1, LANES), lambda b, t: (layer, 0, 0)),
             table, table, smem, gain2(w_hgrn), gain2(w_mlstm), gain2(w_ret)],
            [z_h, z_m, z_r, z_g, hgrn_lb_logits, bias_pad, cos_p, sin_p, log_gamma, gn_h, gn_m, gn_r],
            [(o3_shape, BF16, z_prompt(n_slabs), None), state_p(h_hgrn), state_p(h_mlstm), vec, vec, state_p(h_ret)],
            [pltpu.VMEM((h_hgrn,) + head_sq, F32)] + [pltpu.VMEM((h_hgrn, BLK, LANES), F32) for _ in range(4)]
            + [pltpu.VMEM((h_mlstm, HEAD_DIM, 2 * HEAD_DIM), F32), pltpu.VMEM((h_mlstm, 1, LANES), F32),
               pltpu.VMEM((h_ret,) + head_sq, F32), pltpu.VMEM((h_ret, 3, BLK, LANES), F32)],
            ("arbitrary", "arbitrary"))

        def z_step(n_blocks):
            return pl.BlockSpec((n_blocks, bs, LANES), lambda g: (0, row_block, 0))

        def state_in(state):
            tail = state.shape[2:]
            zeros = (0,) * len(tail)
            return pl.BlockSpec((None, STEP_ROWS) + tail, lambda g: (layer, g) + zeros)

        tposed = pltpu.VMEM((h_hgrn, HEAD_DIM, bs), F32)
        o3, st_h = _mixer_call(
            functools.partial(_hgrn_step_kernel, layer=layer, n_heads=h_hgrn), "hgrn_step", (bs // STEP_ROWS,),
            [z_step(4 * h_hgrn), state_in(state_hgrn), pl.BlockSpec(hgrn_lb_logits.shape, lambda g: (0, 0)), gain1(w_hgrn)],
            [z_h, state_hgrn, hgrn_lb_logits, gn_h],
            [(o3_shape, BF16, o_step(h_hgrn, o_hgrn), o3), _state_out(state_hgrn, layer, st_h)],
            [tposed, tposed], ("arbitrary",))

        tposed = pltpu.VMEM((h_mlstm, HEAD_DIM, bs), F32)
        rowrep = pltpu.VMEM((h_mlstm, bs, LANES), F32)
        tile = pl.BlockSpec((bs, LANES), lambda g: (0, 0))
        o3, st_c, n_s, m_s = _mixer_call(
            functools.partial(_mlstm_step_kernel, n_heads=h_mlstm), "mlstm_step", (bs // STEP_ROWS,),
            [z_step(4 * h_mlstm), pl.BlockSpec((None, bs, LANES), lambda g: (0, row_block, 0)),
             pl.BlockSpec((None, 1, LANES), lambda g: (layer, 0, 0)),
             state_in(state_mlstm_c), state_in(state_mlstm_n), tile, gain1(w_mlstm)],
            [z_m, z_g, bias_pad, state_mlstm_c, state_mlstm_n, m_pad[layer], gn_m],
            [(o3_shape, BF16, o_step(h_mlstm, o_mlstm), o3), _state_out(state_mlstm_c, layer, st_c),
             (state_mlstm_n.shape[1:], F32, pl.BlockSpec((STEP_ROWS, h_mlstm, HEAD_DIM), lambda g: (g, 0, 0)), None),
             ((bs, LANES), F32, tile, None)],
            [tposed, rowrep, rowrep, rowrep], ("arbitrary",))

        tposed = pltpu.VMEM((h_ret, HEAD_DIM, bs), F32)
        table = pl.BlockSpec(cos_s.shape, lambda g: (0, 0))
        o3, st_r = _mixer_call(
            functools.partial(_ret_step_kernel, n_heads=h_ret), "ret_step", (bs // STEP_ROWS,),
            [z_step(4 * h_ret), state_in(state_ret), table, table, smem, gain1(w_ret)],
            [z_r, state_ret, cos_s, sin_s, log_gamma, gn_r],
            [(o3_shape, BF16, o_step(h_ret, o_ret), o3), _state_out(state_ret, layer, st_r)],
            [tposed], ("arbitrary",))

        new_h_p.append(s_hp)
        new_c_p.append(c_p)
        new_n_p.append(n_p[:, :, 0, :])
        new_m_p.append(m_p[:, :, 0, 0])
        new_r_p.append(s_rp)
        new_n_s.append(n_s)
        new_m_s.append(m_s[:, :h_mlstm])

        x, n = _matmul(o3, [w_out], layer, mode="resid_norm", tm=tm_deep, tn=d, n_out=d, resid=x,
                       norm=(norm_ffn2, layer), weight_buffers=1, lhs_order=o_order)

        next_norm = (norm_ffn1, layer + 1) if layer + 1 < depth else None
        x, n, ssq = ffn(x, n, None, ffn2_w_gate, ffn2_w_up, ffn2_w_down, layer, next_norm)

    y_p, y_s = _rmsnorm_split(x, norm_final, n_prompt)
    return (
        y_p.reshape(bp, t_len, d), y_s.reshape(bs, dec_len, d),
        jnp.stack(new_h_p), jnp.stack(new_c_p), jnp.stack(new_n_p), jnp.stack(new_m_p), jnp.stack(new_r_p),
        st_h, st_c, jnp.stack(new_n_s), jnp.stack(new_m_s), st_r,
    )
```

```python
import functools

import jax
import jax.numpy as jnp
from jax import lax
from jax.experimental import pallas as pl
from jax.experimental.pallas import tpu as pltpu

F32 = jnp.float32
BF16 = jnp.bfloat16

HEAD_DIM = 128
LANES = 128
EPS = 1e-6
ROPE_BASE = 10000.0
K_SCALE = HEAD_DIM ** -0.5
PAST_LEN = 16384

BLK = 128
SUB = 32
SEQ_TILE = 512
STEP_ROWS = 16
STEP_UNROLL = 4
STACK_TILE = 1024
ROW_PARTS = (5, 2, 1)
WIDE_ROW_TILE = 2080
DEEP_ROW_TILE = 640
ROPE_TILE = 256
VMEM_LIMIT_BYTES = 60 * 1024 * 1024

DECAY_SPAN_MAX = 60.0

_TRANS_B = (((1,), (1,)), ((), ()))
_TRANS_A = (((0,), (0,)), ((), ()))


def _row_tile(rows, limit):
    return max(t for t in range(16, min(rows, limit) + 1, 16) if rows % t == 0)


def _params(*semantics):
    return pltpu.CompilerParams(dimension_semantics=semantics, vmem_limit_bytes=VMEM_LIMIT_BYTES)


def _silu(x):
    return x * jax.nn.sigmoid(x)


def _softplus_neg_abs(x):
    return jnp.log(1.0 + jnp.exp(-jnp.abs(x)))


def _log_sigmoid(x):
    return jnp.minimum(x, 0.0) - _softplus_neg_abs(x)


def _iota(shape, dim):
    return lax.broadcasted_iota(jnp.int32, shape, dim)


def _seg_cumsum(x, seg):
    pos = _iota(x.shape, 0) & (seg - 1)
    shift = 1
    while shift < seg:
        x = x + jnp.where(pos >= shift, pltpu.roll(x, shift, 0), 0.0)
        shift *= 2
    return x


def _head_norm_gate(o, gain, gate):
    return o * lax.rsqrt(jnp.mean(o * o, axis=-1, keepdims=True) + EPS) * gain * gate


def _pick_lane(x, lane_mask):
    return jnp.sum(jnp.where(lane_mask, x, 0.0), axis=-1, keepdims=True)


def _put_row(tile, j, new):
    return jnp.where(_iota(tile.shape, 0) == j, new, tile)


def _stack_rows_kernel(a_ref, b_ref, g_ref, o_ref, n_ref, *, n_a):
    i = pl.program_id(0)

    def emit(x, rows):
        o_ref[rows, :] = x
        y = x * lax.rsqrt(jnp.mean(x * x, axis=-1, keepdims=True) + EPS)
        n_ref[rows, :] = (y * g_ref[...]).astype(n_ref.dtype)

    @pl.when(i < n_a)
    def _():
        emit(a_ref[...], slice(None))

    @pl.when(i == n_a)
    def _():
        emit(b_ref[...], slice(0, b_ref.shape[0]))


def _stack_rows(a, b, gain):
    (ma, d), mb = a.shape, b.shape[0]
    n_a = ma // STACK_TILE
    tile = pl.BlockSpec((STACK_TILE, d), lambda i: (i, 0))
    return pl.pallas_call(
        functools.partial(_stack_rows_kernel, n_a=n_a),
        out_shape=(jax.ShapeDtypeStruct((ma + mb, d), a.dtype), jax.ShapeDtypeStruct((ma + mb, d), BF16)),
        grid=(n_a + 1,),
        in_specs=[
            pl.BlockSpec((STACK_TILE, d), lambda i: (jnp.minimum(i, n_a - 1), 0)),
            pl.BlockSpec((mb, d), lambda i: (0, 0)),
            pl.BlockSpec((1, d), lambda i: (0, 0)),
        ],
        out_specs=(tile, tile),
        compiler_params=_params("arbitrary"),
        name="stack_rows",
    )(a, b, gain.reshape(1, d))


def _rmsnorm_split_kernel(x_ref, g_ref, a_ref, b_ref, *, n_a):
    i = pl.program_id(0)
    x = x_ref[...]
    y = x * lax.rsqrt(jnp.mean(x * x, axis=-1, keepdims=True) + EPS) * g_ref[...]

    @pl.when(i < n_a)
    def _():
        a_ref[...] = y

    @pl.when(i == n_a)
    def _():
        b_ref[...] = y[0:b_ref.shape[0], :]


def _rmsnorm_split(x, gain, rows_a):
    m, d = x.shape
    n_a = rows_a // STACK_TILE
    rows_b = m - rows_a
    return pl.pallas_call(
        functools.partial(_rmsnorm_split_kernel, n_a=n_a),
        out_shape=(jax.ShapeDtypeStruct((rows_a, d), F32), jax.ShapeDtypeStruct((rows_b, d), F32)),
        grid=(n_a + 1,),
        in_specs=[
            pl.BlockSpec((STACK_TILE, d), lambda i: (i, 0)),
            pl.BlockSpec((1, d), lambda i: (0, 0)),
        ],
        out_specs=(
            pl.BlockSpec((STACK_TILE, d), lambda i: (jnp.minimum(i, n_a - 1), 0)),
            pl.BlockSpec((rows_b, d), lambda i: (0, 0)),
        ),
        compiler_params=_params("arbitrary"),
        name="rmsnorm_split",
    )(x, gain.reshape(1, d))


def _matmul_kernel(*refs, n_w, mode, lhs_order, scaled, parts, recast):
    lhs_ref = refs[0]
    w_refs = refs[1:1 + n_w]
    rest = refs[1 + n_w:]
    if scaled:
        ssq_ref, rest = rest[0], rest[1:]
    if recast:
        rest[-1][...] = rest[0][...].astype(BF16)
        rest = rest[1:-1]
    tm = lhs_ref.shape[0] if lhs_order is None else lhs_ref.shape[1]
    row_parts = [slice(p * (tm // parts), (p + 1) * (tm // parts)) for p in range(parts)]

    def lhs_rows(rs):
        if lhs_order is None:
            return lhs_ref[rs, :]
        return jnp.concatenate([lhs_ref[c, rs, :] for c in lhs_order], axis=1)

    ws = [w_ref[...].astype(BF16) for w_ref in w_refs]
    accs = [[jnp.dot(lhs_rows(rs), w, preferred_element_type=F32) for w in ws] for rs in row_parts]
    for rs, acc in zip(row_parts, accs):
        if scaled:
            total = ssq_ref[0, rs, :]
            for part in range(1, ssq_ref.shape[0]):
                total = total + ssq_ref[part, rs, :]
            r = lax.rsqrt(total[:, 0:1] * (1.0 / ws[0].shape[0]) + EPS)
            acc = [a * r for a in acc]
        if mode == "swiglu":
            (o_ref,) = rest
            gate, up = acc
            o_ref[rs, :] = (0.5 * _silu(gate) * up).astype(o_ref.dtype)
        elif mode == "resid":
            x_ref, o_ref = rest
            o_ref[rs, :] = x_ref[rs, :] + acc[0]
        elif mode == "resid_gain":
            x_ref, g_ref, o_ref, xg_ref, sq_ref = rest
            x = x_ref[rs, :] + acc[0]
            o_ref[rs, :] = x
            xg_ref[rs, :] = (x * g_ref[...]).astype(xg_ref.dtype)
            sq_ref[rs, :] = jnp.broadcast_to(jnp.sum(x * x, axis=-1, keepdims=True), (x.shape[0], LANES))
        elif mode == "resid_norm":
            x_ref, g_ref, o_ref, n_ref = rest
            x = x_ref[rs, :] + acc[0]
            o_ref[rs, :] = x
            y = x * lax.rsqrt(jnp.mean(x * x, axis=-1, keepdims=True) + EPS)
            n_ref[rs, :] = (y * g_ref[...]).astype(n_ref.dtype)
        else:
            (o_ref,) = rest
            for c in range(o_ref.shape[0]):
                o_ref[c, rs, :] = acc[0][:, c * LANES:(c + 1) * LANES].astype(o_ref.dtype)


def _matmul(lhs, weights, layer, *, mode, tm, tn, n_out, col_off=0, resid=None, norm=None, row_ssq=None,
            recast=None, weight_buffers=2, lhs_order=None, out_dtype=F32):
    if lhs_order is None:
        m, k = lhs.shape
        lhs_spec = pl.BlockSpec((tm, k), lambda j, i: (i, 0))
    else:
        m, k = lhs.shape[1], lhs.shape[0] * LANES
        lhs_spec = pl.BlockSpec((lhs.shape[0], tm, LANES), lambda j, i: (0, i, 0))
    n_w = len(weights)
    n_tiles = pl.cdiv(n_out, tn)
    grid = (n_tiles, m // tm)
    in_specs = [lhs_spec]
    in_specs += [pl.BlockSpec((None, k, tn), lambda j, i: (layer, 0, j + col_off),
                              pipeline_mode=pl.Buffered(weight_buffers)) for _ in weights]
    args = [lhs, *weights]
    if row_ssq is not None:
        in_specs.append(pl.BlockSpec((row_ssq.shape[0], tm, LANES), lambda j, i: (0, i, 0)))
        args.append(row_ssq)
    if recast is not None:
        other, other_layer = recast
        rows, width = other.shape[1:]
        n_m = m // tm
        chunk = -(-pl.cdiv(rows, n_tiles * n_m) // 16) * 16
        last = pl.cdiv(rows, chunk) - 1
        in_specs.append(pl.BlockSpec((None, chunk, width),
                                     lambda j, i: (other_layer, jnp.minimum(j * n_m + i, last), 0)))
        args.append(other)
        recast_shape = jax.ShapeDtypeStruct((1, rows, width), BF16)
        recast_spec = pl.BlockSpec((None, chunk, width), lambda j, i: (0, jnp.minimum(j * n_m + i, last), 0))
    tile = pl.BlockSpec((tm, tn), lambda j, i: (i, j))
    if mode in ("resid", "resid_gain", "resid_norm"):
        in_specs.append(tile)
        args.append(resid)
    if mode in ("resid_gain", "resid_norm"):
        gains, norm_layer = norm
        in_specs.append(pl.BlockSpec((None, 1, tn), lambda j, i: (norm_layer, 0, j)))
        args.append(gains.reshape(-1, 1, n_out))
    if mode == "resid_gain":
        out_shape = (jax.ShapeDtypeStruct((m, n_out), F32), jax.ShapeDtypeStruct((m, n_out), BF16),
                     jax.ShapeDtypeStruct((n_tiles, m, LANES), F32))
        out_spec = (tile, tile, pl.BlockSpec((None, tm, LANES), lambda j, i: (j, i, 0)))
    elif mode == "resid_norm":
        assert tn == n_out
        out_shape = (jax.ShapeDtypeStruct((m, n_out), F32), jax.ShapeDtypeStruct((m, n_out), BF16))
        out_spec = (tile, tile)
    elif mode == "cols":
        out_shape = jax.ShapeDtypeStruct((n_out // LANES, m, LANES), out_dtype)
        out_spec = pl.BlockSpec((tn // LANES, tm, LANES), lambda j, i: (j, i, 0))
    else:
        out_shape = jax.ShapeDtypeStruct((m, n_out), BF16 if mode == "swiglu" else F32)
        out_spec = tile
    if recast is not None:
        outs = out_shape if isinstance(out_shape, tuple) else (out_shape,)
        specs = out_spec if isinstance(out_spec, tuple) else (out_spec,)
        out_shape, out_spec = outs + (recast_shape,), specs + (recast_spec,)
    return pl.pallas_call(
        functools.partial(_matmul_kernel, n_w=n_w, mode=mode, lhs_order=lhs_order, scaled=row_ssq is not None,
                          parts=next(p for p in ROW_PARTS if p == 1 or tm % (16 * p) == 0),
                          recast=recast is not None),
        out_shape=out_shape,
        grid=grid,
        in_specs=in_specs,
        out_specs=out_spec,
        compiler_params=_params("arbitrary", "arbitrary"),
        name="matmul_" + mode,
    )(*args)


def _rope_kernel(inv_ref, cos_ref, sin_ref, *, pos0, step):
    shape = cos_ref.shape
    row = _iota(shape, 0) + pl.program_id(0) * shape[0]
    ang = (pos0 + step * row.astype(F32)) * inv_ref[...]
    sin = jnp.sin(ang)
    cos_ref[...] = jnp.cos(ang)
    sin_ref[...] = jnp.where(_iota(shape, 1) < HEAD_DIM // 2, -sin, sin)


def _rope_tables(rows, tile, pos0, step):
    inv = ROPE_BASE ** (-jnp.arange(0, HEAD_DIM, 2, dtype=F32) / HEAD_DIM)
    inv = jnp.concatenate([inv, inv]).reshape(1, HEAD_DIM)
    out = jax.ShapeDtypeStruct((rows, HEAD_DIM), F32)
    spec = pl.BlockSpec((tile, HEAD_DIM), lambda i: (i, 0))
    return pl.pallas_call(
        functools.partial(_rope_kernel, pos0=float(pos0), step=float(step)),
        out_shape=(out, out),
        grid=(rows // tile,),
        in_specs=[pl.BlockSpec((1, HEAD_DIM), lambda i: (0, 0))],
        out_specs=(spec, spec),
        compiler_params=_params("arbitrary"),
        name="rope_tables",
    )(inv)


def _rope(x, cos, sin):
    return x * cos + pltpu.roll(x, HEAD_DIM // 2, 1) * sin


def _mixer_call(kernel_fn, name, grid, in_specs, args, outs, scratch, semantics):
    in_specs, args, aliases = list(in_specs), list(args), {}
    for k, (_, _, _, prev) in enumerate(outs):
        if prev is not None:
            aliases[len(args)] = k
            in_specs.append(pl.BlockSpec(memory_space=pl.ANY))
            args.append(prev)
    n_alias = len(aliases)

    def body(*refs):
        n_in = len(args) - n_alias
        kernel_fn(*refs[:n_in], *refs[n_in + n_alias:])

    return pl.pallas_call(
        body,
        out_shape=tuple(jax.ShapeDtypeStruct(s, dt) for s, dt, _, _ in outs),
        grid=grid,
        in_specs=in_specs,
        out_specs=tuple(spec for _, _, spec, _ in outs),
        scratch_shapes=scratch,
        input_output_aliases=aliases,
        compiler_params=_params(*semantics),
        name=name,
    )(*args)


def _state_out(state, layer, prev):
    tail = state.shape[2:]
    zeros = (0,) * len(tail)
    spec = pl.BlockSpec((None, STEP_ROWS) + tail, lambda g: (layer, g) + zeros)
    return (state.shape, F32, spec, prev)


def _hgrn_lower_bound(logits, layer):
    mx = jnp.max(logits, axis=0, keepdims=True)
    e = jnp.exp(logits - mx)
    p = e / jnp.sum(e, axis=0, keepdims=True)
    if layer == 0:
        return jnp.zeros_like(mx)
    return jnp.sum(p[1:layer + 1], axis=0, keepdims=True)


def _hgrn_gates(fz, lb):
    log_lb = jnp.log(lb)
    y = jnp.log1p(-lb) + _log_sigmoid(fz)
    log_f = jnp.maximum(log_lb, y) + _softplus_neg_abs(log_lb - y)
    key = (1.0 - lb) / (1.0 + jnp.exp(fz))
    return log_f, key


def _hgrn_prompt_parts(z_ref, lbl_ref, gn_ref, o_ref, s_ref, st_ref, q_scr, k_scr, b_scr, a_scr, *, layer, n_heads,
                       slab0):
    t = pl.program_id(1)
    n_sub = BLK // SUB
    lb = _hgrn_lower_bound(lbl_ref[...], layer)
    row = _iota((BLK, LANES), 0)
    col = _iota((BLK, LANES), 1)
    pos = row & (SUB - 1)
    seg0 = row - pos

    def init():
        @pl.when(t == 0)
        def _():
            st_ref[...] = jnp.zeros_like(st_ref)

    def seg_row(x, s):
        x3 = x.reshape(n_sub, SUB, LANES)
        return jnp.broadcast_to(x3[:, s:s + 1, :], x3.shape).reshape(BLK, LANES)

    def block(i):
        rows = pl.ds(pl.multiple_of(i * BLK, BLK), BLK)
        span = jnp.zeros((BLK, LANES), F32)
        for h in range(n_heads):
            log_f, k = _hgrn_gates(z_ref[n_heads + h, rows, :], lb[:, h * LANES:(h + 1) * LANES])
            b = _seg_cumsum(log_f, SUB)
            q_scr[h] = _silu(z_ref[h, rows, :])
            k_scr[h] = k
            b_scr[h] = b
            span = jnp.maximum(span, -seg_row(b, SUB - 1))
        mild = jnp.max(span) <= DECAY_SPAN_MAX

        @pl.when(mild)
        def _():
            q_end, k_end = [], []
            for h in range(n_heads):
                b = b_scr[h]
                b_end = seg_row(b, SUB - 1)
                q_end.append((q_scr[h] * jnp.exp(b - b_end)).astype(BF16))
                k_end.append((k_scr[h] * jnp.exp(b_end - b)).astype(BF16))
            a = [lax.dot_general(q_end[h], k_end[h], _TRANS_B, preferred_element_type=F32) for h in range(n_heads)]
            for h in range(n_heads):
                a_scr[h] = jnp.where((col >= seg0) & (col <= row), a[h], 0.0)

        @pl.when(jnp.logical_not(mild))
        def _():
            for h in range(n_heads):
                q, k, b = q_scr[h], k_scr[h], b_scr[h]
                a = jnp.zeros((BLK, LANES), F32)
                for s in range(SUB):
                    e = jnp.exp(jnp.where(pos >= s, b - seg_row(b, s), -jnp.inf))
                    c = jnp.sum(q * seg_row(k, s) * e, axis=-1, keepdims=True)
                    a = jnp.where(col == seg0 + s, c, a)
                a_scr[h] = a

        heads = range(n_heads)
        subs = [slice(n * SUB, (n + 1) * SUB) for n in range(n_sub)]
        vb, q_in, k_out, lam = [], [], [], []
        for h in heads:
            b = b_scr[h]
            b_end = seg_row(b, SUB - 1)
            vb.append(z_ref[2 * n_heads + h, rows, :].astype(BF16))
            q_in.append((q_scr[h] * jnp.exp(b)).astype(BF16))
            k_out.append((k_scr[h] * jnp.exp(b_end - b)).astype(BF16))
            lam.append(jnp.exp(b_end))
        ups = [[lax.dot_general(vb[h][sl], k_out[h][sl], _TRANS_A, preferred_element_type=F32) for sl in subs]
               for h in heads]
        o = [jnp.dot(a_scr[h].astype(BF16), vb[h], preferred_element_type=F32) for h in heads]
        states = []
        for h in heads:
            chain = [st_ref[h]]
            for n in range(n_sub):
                chain.append(chain[n] * lam[h][n * SUB:n * SUB + 1, :] + ups[h][n])
            st_ref[h] = chain[n_sub]
            states.append([s.astype(BF16) for s in chain[:n_sub]])
        parts = [[lax.dot_general(q_in[h][sl], states[h][n], _TRANS_B, preferred_element_type=F32)
                  for n, sl in enumerate(subs)] for h in heads]
        for h in heads:
            gate = _silu(z_ref[3 * n_heads + h, rows, :])
            out = _head_norm_gate(o[h] + jnp.concatenate(parts[h], axis=0), gn_ref[:, h * LANES:(h + 1) * LANES], gate)
            o_ref[slab0 + h, rows, :] = out.astype(o_ref.dtype)

    def final():
        @pl.when(t == pl.num_programs(1) - 1)
        def _():
            for h in range(n_heads):
                s_ref[h] = st_ref[h].T

    return init, block, final


def _hgrn_step_kernel(z_ref, s_ref, lbl_ref, gn_ref, o_ref, so_ref, f_scr, k_scr, *, layer, n_heads):
    g = pl.program_id(0)
    n_rows = z_ref.shape[1]

    @pl.when(g == 0)
    def _():
        lb = _hgrn_lower_bound(lbl_ref[...], layer)
        for h in range(n_heads):
            log_f, k = _hgrn_gates(z_ref[n_heads + h], lb[:, h * LANES:(h + 1) * LANES])
            f_scr[h] = jnp.exp(log_f).T
            k_scr[h] = k.T

    lane = _iota((HEAD_DIM, n_rows), 1)
    grp = pl.ds(pl.multiple_of(g * STEP_ROWS, STEP_ROWS), STEP_ROWS)
    q = [_silu(z_ref[h, grp, :]).astype(BF16) for h in range(n_heads)]

    def body(j, outs):
        b = g * STEP_ROWS + j
        sel = lane == b
        one = pl.ds(b, 1)
        new = []
        for h in range(n_heads):
            s_new = _pick_lane(f_scr[h], sel) * s_ref[j, h] + _pick_lane(k_scr[h], sel) * z_ref[2 * n_heads + h, one, :]
            so_ref[j, h] = s_new
            new.append(_put_row(outs[h], j, jnp.dot(q[h], s_new.astype(BF16), preferred_element_type=F32)))
        return tuple(new)

    zero = jnp.zeros((STEP_ROWS, LANES), F32)
    outs = lax.fori_loop(0, STEP_ROWS, body, (zero,) * n_heads, unroll=STEP_UNROLL)
    for h in range(n_heads):
        gate = _silu(z_ref[3 * n_heads + h, grp, :])
        o_ref[h] = _head_norm_gate(outs[h], gn_ref[:, h * LANES:(h + 1) * LANES], gate).astype(o_ref.dtype)


def _ret_prompt_parts(z_ref, cos_ref, sin_ref, lg_ref, gn_ref, o_ref, s_ref, st_ref, dec_ref, *, n_heads, slab0):
    t = pl.program_id(1)

    def init():
        @pl.when((pl.program_id(0) == 0) & (t == 0))
        def _():
            row = _iota((BLK, LANES), 0).astype(F32)
            diff = row - _iota((BLK, LANES), 1).astype(F32)
            for h in range(n_heads):
                lg = lg_ref[h]
                dec_ref[h, 0] = jnp.where(diff >= 0.0, jnp.exp(diff * lg), 0.0)
                dec_ref[h, 1] = jnp.exp((row + 1.0) * lg)
                dec_ref[h, 2] = jnp.exp((BLK - 1.0 - row) * lg)

        @pl.when(t == 0)
        def _():
            st_ref[...] = jnp.zeros_like(st_ref)

    def block(i):
        rows = pl.ds(pl.multiple_of(i * BLK, BLK), BLK)
        tab = pl.ds(pl.multiple_of(t * z_ref.shape[1] + i * BLK, BLK), BLK)
        cos = cos_ref[tab, :]
        sin = sin_ref[tab, :]
        heads = range(n_heads)
        q = [_rope(z_ref[h, rows, :].astype(F32), cos, sin) for h in heads]
        k = [_rope(z_ref[n_heads + h, rows, :].astype(F32), cos, sin) * K_SCALE for h in heads]
        vb = [z_ref[2 * n_heads + h, rows, :].astype(BF16) for h in heads]
        a = [lax.dot_general(q[h].astype(BF16), k[h].astype(BF16), _TRANS_B, preferred_element_type=F32)
             for h in heads]
        carried = [jnp.dot((q[h] * dec_ref[h, 1]).astype(BF16), st_ref[h].astype(BF16), preferred_element_type=F32)
                   for h in heads]
        update = [jnp.dot((k[h] * dec_ref[h, 2]).T.astype(BF16), vb[h], preferred_element_type=F32) for h in heads]
        o = [jnp.dot((a[h] * dec_ref[h, 0]).astype(BF16), vb[h], preferred_element_type=F32) for h in heads]
        for h in heads:
            s_dec = jnp.exp(jnp.full((1, LANES), BLK, F32) * lg_ref[h])
            st_ref[h] = s_dec * st_ref[h] + update[h]
            gate = _silu(z_ref[3 * n_heads + h, rows, :].astype(F32))
            out = _head_norm_gate(o[h] + carried[h], gn_ref[:, h * LANES:(h + 1) * LANES], gate)
            o_ref[slab0 + h, rows, :] = out.astype(o_ref.dtype)

    def final():
        @pl.when(t == pl.num_programs(1) - 1)
        def _():
            s_ref[...] = st_ref[...]

    return init, block, final


def _ret_step_kernel(z_ref, s_ref, cos_ref, sin_ref, lg_ref, gn_ref, o_ref, so_ref, k_scr, *, n_heads):
    g = pl.program_id(0)
    n_rows = z_ref.shape[1]
    cos = cos_ref[0:1, :]
    sin = sin_ref[0:1, :]

    @pl.when(g == 0)
    def _():
        for h in range(n_heads):
            k_scr[h] = (_rope(z_ref[n_heads + h], cos, sin) * K_SCALE).T

    lane = _iota((HEAD_DIM, n_rows), 1)
    grp = pl.ds(pl.multiple_of(g * STEP_ROWS, STEP_ROWS), STEP_ROWS)
    q = [_rope(z_ref[h, grp, :], cos, sin).astype(BF16) for h in range(n_heads)]

    def body(j, outs):
        b = g * STEP_ROWS + j
        sel = lane == b
        one = pl.ds(b, 1)
        new = []
        for h in range(n_heads):
            gamma = jnp.exp(jnp.full((1, LANES), lg_ref[h], F32))
            s_new = gamma * s_ref[j, h] + _pick_lane(k_scr[h], sel) * z_ref[2 * n_heads + h, one, :]
            so_ref[j, h] = s_new
            new.append(_put_row(outs[h], j, jnp.dot(q[h], s_new.astype(BF16), preferred_element_type=F32)))
        return tuple(new)

    zero = jnp.zeros((STEP_ROWS, LANES), F32)
    outs = lax.fori_loop(0, STEP_ROWS, body, (zero,) * n_heads, unroll=STEP_UNROLL)
    for h in range(n_heads):
        gate = _silu(z_ref[3 * n_heads + h, grp, :])
        o_ref[h] = _head_norm_gate(outs[h], gn_ref[:, h * LANES:(h + 1) * LANES], gate).astype(o_ref.dtype)


def _mlstm_prompt_parts(z_ref, zg_ref, bias_ref, gn_ref, o_ref, c_ref, n_ref, m_ref, cn_scr, m_scr, *, n_heads, slab0):
    t = pl.program_id(1)
    lane = _iota((BLK, LANES), 1)
    causal = lane <= _iota((BLK, LANES), 0)
    ones = jnp.ones((BLK, LANES), BF16)

    def twice(x):
        return jnp.concatenate([x, x], axis=1)

    def init():
        @pl.when(t == 0)
        def _():
            cn_scr[...] = jnp.zeros_like(cn_scr)
            m_scr[...] = jnp.zeros_like(m_scr)

    def block(i):
        rows = pl.ds(pl.multiple_of(i * BLK, BLK), BLK)
        gates = zg_ref[rows, :] + bias_ref[...]
        log_f_all = pltpu.roll(_log_sigmoid(gates), LANES - n_heads, 1)
        bt_all = _seg_cumsum(log_f_all, BLK)
        w_t = (gates - bt_all).T
        heads = range(n_heads)
        d_mat, w_inter, m_t, w_c, kw_t, qb, kb, v_ones = [], [], [], [], [], [], [], []
        for h in heads:
            bt = _pick_lane(bt_all, lane == h)
            w_row = w_t[h:h + 1, :]
            log_d = jnp.where(causal, bt + w_row, -jnp.inf)
            m_prev = m_scr[h]
            log_inter = bt + m_prev
            m_t.append(jnp.maximum(log_inter, jnp.max(log_d, axis=-1, keepdims=True)))
            d_mat.append(jnp.exp(log_d - m_t[h]))
            w_inter.append(jnp.exp(log_inter - m_t[h]))
            m_end = m_t[h][BLK - 1:BLK, :]
            b_end = bt[BLK - 1:BLK, :]
            k = z_ref[n_heads + h, rows, :].astype(F32) * K_SCALE
            kb.append(k.astype(BF16))
            kw_t.append((k.T * jnp.exp(b_end + w_row - m_end)).astype(BF16))
            w_c.append(jnp.exp(b_end + m_prev - m_end))
            qb.append(z_ref[h, rows, :].astype(BF16))
            v_ones.append(jnp.concatenate([z_ref[2 * n_heads + h, rows, :].astype(BF16), ones], axis=1))
        sc = [lax.dot_general(qb[h], kb[h], _TRANS_B, preferred_element_type=F32) for h in heads]
        carried = [jnp.dot(qb[h], cn_scr[h].astype(BF16), preferred_element_type=F32) for h in heads]
        update = [jnp.dot(kw_t[h], v_ones[h], preferred_element_type=F32) for h in heads]
        both = [jnp.dot((sc[h] * d_mat[h]).astype(BF16), v_ones[h], preferred_element_type=F32) for h in heads]
        for h in heads:
            tot = both[h] + carried[h] * twice(w_inter[h])
            num, den = tot[:, :LANES], tot[:, LANES:]
            hid = num / jnp.maximum(jnp.abs(den), jnp.exp(-m_t[h]))
            cn_scr[h] = twice(w_c[h]) * cn_scr[h] + update[h]
            m_scr[h] = m_t[h][BLK - 1:BLK, :]
            gate = jax.nn.sigmoid(z_ref[3 * n_heads + h, rows, :].astype(F32))
            out = _head_norm_gate(hid, gn_ref[:, h * LANES:(h + 1) * LANES], gate)
            o_ref[slab0 + h, rows, :] = out.astype(o_ref.dtype)

    def final():
        @pl.when(t == pl.num_programs(1) - 1)
        def _():
            for h in range(n_heads):
                c_ref[h] = cn_scr[h, :, :LANES]
                n_ref[h] = cn_scr[h, :, LANES:].T[0:1, :]
            m_ref[...] = m_scr[...]

    return init, block, final


def _mlstm_step_kernel(z_ref, zg_ref, bias_ref, c_ref, n_ref, m_ref, gn_ref,
                       o_ref, co_ref, no_ref, mo_ref, k_scr, wc_scr, wk_scr, em_scr, *, n_heads):
    g = pl.program_id(0)
    n_rows = z_ref.shape[1]

    @pl.when(g == 0)
    def _():
        lanes = _iota((n_rows, LANES), 1)
        gates = zg_ref[...] + bias_ref[...]
        log_f = pltpu.roll(_log_sigmoid(gates), LANES - n_heads, 1)
        m_old = m_ref[...]
        m_new = jnp.maximum(log_f + m_old, gates)
        mo_ref[...] = m_new
        w_c = jnp.exp(log_f + m_old - m_new)
        w_k = jnp.exp(gates - m_new)
        e_m = jnp.exp(-m_new)
        for h in range(n_heads):
            sel = lanes == h
            wc_scr[h] = jnp.broadcast_to(_pick_lane(w_c, sel), (n_rows, LANES))
            wk_scr[h] = jnp.broadcast_to(_pick_lane(w_k, sel), (n_rows, LANES))
            em_scr[h] = jnp.broadcast_to(_pick_lane(e_m, sel), (n_rows, LANES))
            k_scr[h] = (z_ref[n_heads + h] * K_SCALE).T

    lane = _iota((HEAD_DIM, n_rows), 1)
    grp = pl.ds(pl.multiple_of(g * STEP_ROWS, STEP_ROWS), STEP_ROWS)
    q = [z_ref[h, grp, :].astype(BF16) for h in range(n_heads)]

    def body(j, outs):
        b = g * STEP_ROWS + j
        sel = lane == b
        one = pl.ds(b, 1)
        new = []
        for h in range(n_heads):
            w_c = wc_scr[h, one, :]
            w_k = wk_scr[h, one, :]
            c_new = w_c * c_ref[j, h] + (w_k * _pick_lane(k_scr[h], sel)) * z_ref[2 * n_heads + h, one, :]
            co_ref[j, h] = c_new
            n_new = w_c * n_ref[j, h:h + 1, :] + w_k * (z_ref[n_heads + h, one, :] * K_SCALE)
            no_ref[j, h:h + 1, :] = n_new
            num = jnp.dot(q[h], c_new.astype(BF16), preferred_element_type=F32)
            new.append(_put_row(outs[2 * h], j, num))
            new.append(_put_row(outs[2 * h + 1], j, n_new))
        return tuple(new)

    zero = jnp.zeros((STEP_ROWS, LANES), F32)
    outs = lax.fori_loop(0, STEP_ROWS, body, (zero,) * (2 * n_heads), unroll=STEP_UNROLL)
    for h in range(n_heads):
        num, n_new = outs[2 * h], outs[2 * h + 1]
        den = jnp.sum(z_ref[h, grp, :] * n_new, axis=-1, keepdims=True)
        hid = num / jnp.maximum(jnp.abs(den), em_scr[h, grp, :])
        gate = jax.nn.sigmoid(z_ref[3 * n_heads + h, grp, :])
        o_ref[h] = _head_norm_gate(hid, gn_ref[:, h * LANES:(h + 1) * LANES], gate).astype(o_ref.dtype)


def _mixers_prompt_kernel(zh_ref, zm_ref, zr_ref, zg_ref, lbl_ref, bias_ref, cos_ref, sin_ref, lg_ref, gnh_ref, gnm_ref,
                          gnr_ref, o_ref, sh_ref, c_ref, n_ref, m_ref, sr_ref,
                          h_st, h_q, h_k, h_b, h_a, m_cn, m_m, r_st, r_dec, *, layer, heads, slabs):
    h_hgrn, h_mlstm, h_ret = heads
    o_hgrn, o_mlstm, o_ret = slabs
    mixers = [
        _hgrn_prompt_parts(zh_ref, lbl_ref, gnh_ref, o_ref, sh_ref, h_st, h_q, h_k, h_b, h_a,
                           layer=layer, n_heads=h_hgrn, slab0=o_hgrn),
        _mlstm_prompt_parts(zm_ref, zg_ref, bias_ref, gnm_ref, o_ref, c_ref, n_ref, m_ref, m_cn, m_m,
                            n_heads=h_mlstm, slab0=o_mlstm),
        _ret_prompt_parts(zr_ref, cos_ref, sin_ref, lg_ref, gnr_ref, o_ref, sr_ref, r_st, r_dec,
                          n_heads=h_ret, slab0=o_ret),
    ]
    for init, _, _ in mixers:
        init()

    def block(i, carry):
        for _, run, _ in mixers:
            run(i)
        return carry

    lax.fori_loop(0, zh_ref.shape[1] // BLK, block, 0)
    for _, _, final in mixers:
        final()


def kernel(x_prompt, x_sample, state_hgrn, state_mlstm_c, state_mlstm_n, state_mlstm_m, state_ret, norm_ffn1, ffn1_w_gate, ffn1_w_up, ffn1_w_down, norm_mix, w_in, hgrn_lb_logits, mlstm_gate_bias, gn_hgrn, gn_mlstm, gn_ret, w_out, norm_ffn2, ffn2_w_gate, ffn2_w_up, ffn2_w_down, norm_final):
    bp, t_len, d = x_prompt.shape
    bs, dec_len, _ = x_sample.shape
    depth = w_in.shape[0]
    h_hgrn, h_mlstm, h_ret = state_hgrn.shape[2], state_mlstm_c.shape[2], state_ret.shape[2]
    w_hgrn, w_mlstm, w_ret = h_hgrn * HEAD_DIM, h_mlstm * HEAD_DIM, h_ret * HEAD_DIM
    d_ff = ffn1_w_gate.shape[2]
    n_prompt = bp * t_len
    m_rows = n_prompt + bs
    seq_tile = min(SEQ_TILE, t_len)
    n_seq = t_len // seq_tile
    assert dec_len == 1 and bs == LANES and n_prompt % bs == 0 and t_len % seq_tile == 0 and seq_tile % BLK == 0
    row_block = n_prompt // bs
    step_block = n_prompt // STEP_ROWS
    tm_wide = _row_tile(m_rows, WIDE_ROW_TILE)
    tm_deep = _row_tile(m_rows, DEEP_ROW_TILE)
    assert n_prompt % STACK_TILE == 0
    o_mlstm, o_ret, o_hgrn = 0, h_mlstm, h_mlstm + h_ret
    assert o_ret % h_ret == 0 and o_hgrn % h_hgrn == 0
    n_slabs = h_hgrn + h_mlstm + h_ret
    o_order = tuple(range(o_hgrn, n_slabs)) + tuple(range(o_mlstm, o_ret)) + tuple(range(o_ret, o_hgrn))

    gate0 = 4 * w_hgrn + 4 * w_mlstm
    ret0 = gate0 + 2 * h_mlstm
    gate_pad = ((0, 0), (0, 0), (0, 2 * LANES - 2 * h_mlstm))
    w_in_hm = jnp.concatenate([w_in[:, :, :gate0], jnp.pad(w_in[:, :, gate0:ret0], gate_pad)], axis=2).astype(BF16)
    w_in_r = jnp.pad(w_in[:, :, ret0:], ((0, 0), (0, 0), (0, 2 * LANES))).astype(BF16)
    bias_pad = jnp.pad(mlstm_gate_bias, ((0, 0), (0, LANES - 2 * h_mlstm))).reshape(depth, 1, LANES)
    m_pad = jnp.pad(state_mlstm_m, ((0, 0), (0, 0), (0, LANES - h_mlstm)))
    gn_h = gn_hgrn.reshape(depth, 1, w_hgrn)
    gn_m = gn_mlstm.reshape(depth, 1, w_mlstm)
    gn_r = gn_ret.reshape(depth, 1, w_ret)
    log_gamma = jnp.log1p(-jnp.exp2(-5.0 - jnp.arange(h_ret, dtype=F32)))
    cos_p, sin_p = _rope_tables(t_len, min(ROPE_TILE, t_len), 0.0, 1.0)
    cos_s, sin_s = _rope_tables(8, 8, PAST_LEN, 0.0)

    smem = pl.BlockSpec(memory_space=pltpu.SMEM)
    o3_shape = (n_slabs, m_rows, LANES)
    head_sq = (HEAD_DIM, HEAD_DIM)

    def z_prompt(n_blocks):
        return pl.BlockSpec((n_blocks, seq_tile, LANES), lambda b, t: (0, b * n_seq + t, 0))

    def o_step(heads, slab0):
        return pl.BlockSpec((heads, STEP_ROWS, LANES), lambda g: (slab0 // heads, step_block + g, 0))

    x, n = _stack_rows(x_prompt.reshape(n_prompt, d), x_sample.reshape(bs, d), norm_ffn1[0])
    new_h_p, new_c_p, new_n_p, new_m_p, new_r_p = [], [], [], [], []
    new_n_s, new_m_s = [], []
    st_h = st_c = st_r = None

    def ffn(x, n, ssq, w_gate, w_up, w_down, layer, next_norm):
        mid, w_down16 = _matmul(n, [w_gate, w_up], layer, mode="swiglu", tm=tm_wide, tn=512, n_out=d_ff, row_ssq=ssq,
                                recast=(w_down, layer))
        down = dict(tm=tm_deep, tn=1024, n_out=d, resid=x)
        if next_norm is None:
            return _matmul(mid, [w_down16], 0, mode="resid", **down), None, None
        return _matmul(mid, [w_down16], 0, mode="resid_gain", norm=next_norm, **down)

    ssq = None
    for layer in range(depth):
        x, n, ssq = ffn(x, n, ssq, ffn1_w_gate, ffn1_w_up, ffn1_w_down, layer, (norm_mix, layer))

        cols = dict(mode="cols", tm=tm_wide, row_ssq=ssq)
        z_h = _matmul(n, [w_in_hm], layer, tn=1024, n_out=4 * w_hgrn, **cols)
        z_m = _matmul(n, [w_in_hm], layer, tn=1024, n_out=4 * w_mlstm, col_off=4 * w_hgrn // 1024, out_dtype=BF16,
                      **cols)
        z_r = _matmul(n, [w_in_r], layer, tn=1024, n_out=4 * w_ret, out_dtype=BF16, **cols)
        z_m_step = z_m[:, n_prompt:, :].astype(F32)
        z_r_step = z_r[:, n_prompt:, :].astype(F32)
        z_g = _matmul(n, [w_in_hm], layer, tn=2 * LANES, n_out=2 * LANES, col_off=gate0 // (2 * LANES), **cols)

        def gain2(width):
            return pl.BlockSpec((None, 1, width), lambda b, t: (layer, 0, 0))

        def gain1(width):
            return pl.BlockSpec((None, 1, width), lambda g: (layer, 0, 0))

        def state_p(heads):
            return ((bp, heads) + head_sq, F32, pl.BlockSpec((None, heads) + head_sq, lambda b, t: (b, 0, 0, 0)), None)

        vec = ((bp, h_mlstm, 1, LANES), F32, pl.BlockSpec((None, h_mlstm, 1, LANES), lambda b, t: (b, 0, 0, 0)), None)
        table = pl.BlockSpec((t_len, LANES), lambda b, t: (0, 0))
        o3, s_hp, c_p, n_p, m_p, s_rp = _mixer_call(
            functools.partial(_mixers_prompt_kernel, layer=layer, heads=(h_hgrn, h_mlstm, h_ret),
                              slabs=(o_hgrn, o_mlstm, o_ret)),
            "mixers_prompt", (bp, n_seq),
            [z_prompt(z_h.shape[0]), z_prompt(z_m.shape[0]), z_prompt(z_r.shape[0]),
             pl.BlockSpec((None, seq_tile, LANES), lambda b, t: (0, b * n_seq + t, 0)),
             pl.BlockSpec(hgrn_lb_logits.shape, lambda b, t: (0, 0)),
             pl.BlockSpec((None, 1, LANES), lambda b, t: (layer, 0, 0)),
             table, table, smem, gain2(w_hgrn), gain2(w_mlstm), gain2(w_ret)],
            [z_h, z_m, z_r, z_g, hgrn_lb_logits, bias_pad, cos_p, sin_p, log_gamma, gn_h, gn_m, gn_r],
            [(o3_shape, BF16, z_prompt(n_slabs), None), state_p(h_hgrn), state_p(h_mlstm), vec, vec, state_p(h_ret)],
            [pltpu.VMEM((h_hgrn,) + head_sq, F32)] + [pltpu.VMEM((h_hgrn, BLK, LANES), F32) for _ in range(4)]
            + [pltpu.VMEM((h_mlstm, HEAD_DIM, 2 * HEAD_DIM), F32), pltpu.VMEM((h_mlstm, 1, LANES), F32),
               pltpu.VMEM((h_ret,) + head_sq, F32), pltpu.VMEM((h_ret, 3, BLK, LANES), F32)],
            ("arbitrary", "arbitrary"))

        def z_step(n_blocks):
            return pl.BlockSpec((n_blocks, bs, LANES), lambda g: (0, row_block, 0))

        def state_in(state):
            tail = state.shape[2:]
            zeros = (0,) * len(tail)
            return pl.BlockSpec((None, STEP_ROWS) + tail, lambda g: (layer, g) + zeros)

        tposed = pltpu.VMEM((h_hgrn, HEAD_DIM, bs), F32)
        o3, st_h = _mixer_call(
            functools.partial(_hgrn_step_kernel, layer=layer, n_heads=h_hgrn), "hgrn_step", (bs // STEP_ROWS,),
            [z_step(4 * h_hgrn), state_in(state_hgrn), pl.BlockSpec(hgrn_lb_logits.shape, lambda g: (0, 0)), gain1(w_hgrn)],
            [z_h, state_hgrn, hgrn_lb_logits, gn_h],
            [(o3_shape, BF16, o_step(h_hgrn, o_hgrn), o3), _state_out(state_hgrn, layer, st_h)],
            [tposed, tposed], ("arbitrary",))

        tposed = pltpu.VMEM((h_mlstm, HEAD_DIM, bs), F32)
        rowrep = pltpu.VMEM((h_mlstm, bs, LANES), F32)
        tile = pl.BlockSpec((bs, LANES), lambda g: (0, 0))
        o3, st_c, n_s, m_s = _mixer_call(
            functools.partial(_mlstm_step_kernel, n_heads=h_mlstm), "mlstm_step", (bs // STEP_ROWS,),
            [pl.BlockSpec(z_m_step.shape, lambda g: (0, 0, 0)), pl.BlockSpec((None, bs, LANES), lambda g: (0, row_block, 0)),
             pl.BlockSpec((None, 1, LANES), lambda g: (layer, 0, 0)),
             state_in(state_mlstm_c), state_in(state_mlstm_n), tile, gain1(w_mlstm)],
            [z_m_step, z_g, bias_pad, state_mlstm_c, state_mlstm_n, m_pad[layer], gn_m],
            [(o3_shape, BF16, o_step(h_mlstm, o_mlstm), o3), _state_out(state_mlstm_c, layer, st_c),
             (state_mlstm_n.shape[1:], F32, pl.BlockSpec((STEP_ROWS, h_mlstm, HEAD_DIM), lambda g: (g, 0, 0)), None),
             ((bs, LANES), F32, tile, None)],
            [tposed, rowrep, rowrep, rowrep], ("arbitrary",))

        tposed = pltpu.VMEM((h_ret, HEAD_DIM, bs), F32)
        table = pl.BlockSpec(cos_s.shape, lambda g: (0, 0))
        o3, st_r = _mixer_call(
            functools.partial(_ret_step_kernel, n_heads=h_ret), "ret_step", (bs // STEP_ROWS,),
            [pl.BlockSpec(z_r_step.shape, lambda g: (0, 0, 0)), state_in(state_ret), table, table, smem, gain1(w_ret)],
            [z_r_step, state_ret, cos_s, sin_s, log_gamma, gn_r],
            [(o3_shape, BF16, o_step(h_ret, o_ret), o3), _state_out(state_ret, layer, st_r)],
            [tposed], ("arbitrary",))

        new_h_p.append(s_hp)
        new_c_p.append(c_p)
        new_n_p.append(n_p[:, :, 0, :])
        new_m_p.append(m_p[:, :, 0, 0])
        new_r_p.append(s_rp)
        new_n_s.append(n_s)
        new_m_s.append(m_s[:, :h_mlstm])

        x, n = _matmul(o3, [w_out], layer, mode="resid_norm", tm=tm_deep, tn=d, n_out=d, resid=x,
                       norm=(norm_ffn2, layer), weight_buffers=1, lhs_order=o_order)

        next_norm = (norm_ffn1, layer + 1) if layer + 1 < depth else None
        x, n, ssq = ffn(x, n, None, ffn2_w_gate, ffn2_w_up, ffn2_w_down, layer, next_norm)

    y_p, y_s = _rmsnorm_split(x, norm_final, n_prompt)
    return (
        y_p.reshape(bp, t_len, d), y_s.reshape(bs, dec_len, d),
        jnp.stack(new_h_p), jnp.stack(new_c_p), jnp.stack(new_n_p), jnp.stack(new_m_p), jnp.stack(new_r_p),
        st_h, st_c, jnp.stack(new_n_s), jnp.stack(new_m_s), st_r,
    )
```
